```python
import jax, jax.numpy as jnp
from jax import lax
import numpy as np

D_MODEL = 1024
BATCH = 2
SEQ = 8192
DEPTH = 4
DEC_BATCH = 32
DEC_SEQ = 8
PAST_LEN = 8192
PAGE_SIZE = 128

POOL_GROUPS = 4
POOL_GC = D_MODEL // 16
POOL_W = POOL_GROUPS * POOL_GC
POOL_WINDOWS = (2, 4, 8, 16)
POOL_HIST = 15
ATT_HD = 64
DIL_GROUPS = ((128, 1), (512, 4), (2048, 16))
ATT_HPG = 2
ATT_HEADS = ATT_HPG * len(DIL_GROUPS)
ATT_W = ATT_HEADS * ATT_HD
ATT_OUT = ATT_HPG * ATT_HD
BAND_BLK = 128
ROPE_THETA = 10000.0
RWKV_HD = 64
RWKV_HEADS = 6
RWKV_W = RWKV_HEADS * RWKV_HD
DECAY_LORA = 64
AAA_LORA = 64
GATE_LORA = 128
RWKV_PROJ = 3 * RWKV_W + DECAY_LORA + AAA_LORA + GATE_LORA
RWKV_LN_EPS = 64e-5
FFN_HIDDEN = -(-8 * D_MODEL // (3 * 256)) * 256
PLE_DIM = 256
RMS_EPS = 1e-6
IN_SIZES = (POOL_W, ATT_W, ATT_W, ATT_W, RWKV_PROJ, D_MODEL, D_MODEL, D_MODEL)
IN_COLS = POOL_W + 3 * ATT_W + RWKV_PROJ + 3 * D_MODEL
RWKV_SIZES = (RWKV_W, RWKV_W, RWKV_W, DECAY_LORA, AAA_LORA, GATE_LORA)

kernel_name = 'hybrid_pool_dilattn_rwkv7_step'


def split_cols(z, sizes):
    offs, acc = [], 0
    for s in sizes[:-1]:
        acc += s
        offs.append(acc)
    return jnp.split(z, offs, axis=-1)


def rms_norm(x, g):
    xf = x.astype(jnp.float32)
    y = xf * lax.rsqrt(jnp.mean(xf * xf, axis=-1, keepdims=True) + RMS_EPS)
    return (y * g.astype(jnp.float32)).astype(x.dtype)


def rope(x, pos):
    C = x.shape[-1]
    half = C // 2
    inv = ROPE_THETA ** (-2.0 * jnp.arange(half, dtype=jnp.float32) / C)
    ang = pos[:, None] * inv[None, :]
    cos = jnp.cos(ang)[None, :, None, :]
    sin = jnp.sin(ang)[None, :, None, :]
    xf = x.astype(jnp.float32)
    x1, x2 = xf[..., :half], xf[..., half:]
    return jnp.concatenate([x1 * cos - x2 * sin, x2 * cos + x1 * sin], axis=-1).astype(x.dtype)


def pool_mix(u_hist, u_new, pos0, w_grp, scale):
    N, T, _ = u_new.shape
    Hh = u_hist.shape[1]
    ext = jnp.concatenate([u_hist.astype(jnp.float32), u_new.astype(jnp.float32)], axis=1)
    cs = jnp.cumsum(ext, axis=1)
    pos = pos0 + jnp.arange(T)
    means = []
    for g, w in enumerate(POOL_WINDOWS):
        c = cs[..., g * POOL_GC:(g + 1) * POOL_GC]
        lag = jnp.pad(c, ((0, 0), (w, 0), (0, 0)))[:, :c.shape[1]]
        win = (c - lag)[:, Hh:]
        cnt = jnp.minimum(pos + 1, w).astype(jnp.float32)
        means.append(win / cnt[None, :, None])
    d = (jnp.concatenate(means, axis=-1) - ext[:, Hh:]).reshape(N, T, POOL_GROUPS, POOL_GC)
    y = jnp.einsum('ntgc,gcd->ntgd', d, w_grp.astype(jnp.float32)).reshape(N, T, POOL_W)
    return y * scale.astype(jnp.float32), ext[:, -POOL_HIST:]


def band_attn(q, k, v, d, span):
    N, S, H, C = q.shape
    L = S // d
    nb = -(-L // BAND_BLK)
    Lp = nb * BAND_BLK

    def stream(x):
        x = x.astype(jnp.float32).reshape(N, L, d, H, C).transpose(0, 2, 1, 3, 4)
        x = jnp.pad(x, ((0, 0), (0, 0), (0, Lp - L), (0, 0), (0, 0)))
        return x.reshape(N, d, nb, BAND_BLK, H, C)

    def with_prev(x):
        prev = jnp.pad(x, ((0, 0), (0, 0), (1, 0), (0, 0), (0, 0), (0, 0)))[:, :, :nb]
        return jnp.concatenate([prev, x], axis=3)

    qs = stream(q)
    ks = with_prev(stream(k))
    vs = with_prev(stream(v))
    s = jnp.einsum('nrbqhc,nrbkhc->nrbhqk', qs, ks) * (C ** -0.5)
    qi = jnp.arange(BAND_BLK)[:, None]
    kj = jnp.arange(2 * BAND_BLK)[None, :]
    dist = BAND_BLK + qi - kj
    kidx = jnp.arange(nb)[:, None, None] * BAND_BLK + kj[None] - BAND_BLK
    mask = (dist >= 0) & (dist <= span) & (kidx >= 0)
    s = jnp.where(mask[None, None, :, None], s, -jnp.inf)
    m = jnp.max(s, axis=-1, keepdims=True)
    e = jnp.exp(s - m)
    den = jnp.sum(e, axis=-1)
    o = jnp.einsum('nrbhqk,nrbkhc->nrbqhc', e, vs) / den.transpose(0, 1, 2, 4, 3)[..., None]
    lse = (m[..., 0] + jnp.log(den)).transpose(0, 1, 2, 4, 3)
    o = o.reshape(N, d, Lp, H, C)[:, :, :L].transpose(0, 2, 1, 3, 4).reshape(N, S, H, C)
    lse = lse.reshape(N, d, Lp, H)[:, :, :L].transpose(0, 2, 1, 3).reshape(N, S, H)
    return o, lse


def gather_attn(q, k_ext, v_ext, hist_len, d, span):
    N, T, H, C = q.shape
    idx = hist_len + jnp.arange(T)[:, None] - d * jnp.arange(span + 1)[None, :]
    valid = idx >= 0
    idx = jnp.maximum(idx, 0)
    kg = k_ext.astype(jnp.float32)[:, idx]
    vg = v_ext.astype(jnp.float32)[:, idx]
    s = jnp.einsum('nthc,ntkhc->nhtk', q.astype(jnp.float32), kg) * (C ** -0.5)
    s = jnp.where(valid[None, None], s, -jnp.inf)
    m = jnp.max(s, axis=-1, keepdims=True)
    e = jnp.exp(s - m)
    den = jnp.sum(e, axis=-1)
    o = jnp.einsum('nhtk,ntkhc->nthc', e, vg) / den.transpose(0, 2, 1)[..., None]
    lse = (m[..., 0] + jnp.log(den)).transpose(0, 2, 1)
    return o, lse


def merge_dilations(outs, lses):
    o = jnp.stack(outs, axis=0)
    w = jax.nn.softmax(jnp.stack(lses, axis=0), axis=0)
    y = jnp.sum(w[..., None] * o, axis=0)
    return y.reshape(y.shape[0], y.shape[1], ATT_OUT)


def rwkv7_mix(zc, shift_prev, S0, lw):
    N, T, _ = zc.shape
    zc = zc.astype(jnp.float32)
    prev = jnp.concatenate([shift_prev.astype(jnp.float32)[:, None], zc[:, :-1]], axis=1)
    zs = zc + (prev - zc) * lw['rwkv_mu'].astype(jnp.float32)
    r, k, v, zw, za, zg = split_cols(zs, RWKV_SIZES)
    w_log = -jax.nn.softplus(-(lw['rwkv_w0'] + jnp.tanh(zw) @ lw['rwkv_w2'])) - 0.5
    decay = jnp.exp(-jnp.exp(w_log.astype(jnp.float32)))
    a = jax.nn.sigmoid((lw['rwkv_a0'] + za @ lw['rwkv_a2']).astype(jnp.float32))
    g = (jax.nn.sigmoid(zg) @ lw['rwkv_g2']).astype(jnp.float32)
    heads = lambda t: t.astype(jnp.float32).reshape(N, T, RWKV_HEADS, RWKV_HD)
    kk = heads(k * lw['rwkv_k_k'])
    kk = kk * lax.rsqrt(jnp.maximum(jnp.sum(kk * kk, axis=-1, keepdims=True), 1e-24))
    k = k * (1.0 + (a - 1.0) * lw['rwkv_k_a'])
    r, k, v, decay, a = (heads(t) for t in (r, k, v, decay, a))

    def step(S, inp):
        r_t, w_t, k_t, v_t, kk_t, a_t = inp
        sa = jnp.einsum('nhvk,nhk->nhv', S, -kk_t)
        S = (S * w_t[:, :, None, :] + sa[..., None] * (kk_t * a_t)[:, :, None, :]
             + v_t[..., None] * k_t[:, :, None, :])
        return S, jnp.einsum('nhvk,nhk->nhv', S, r_t)

    seq = tuple(t.transpose(1, 0, 2, 3) for t in (r, decay, k, v, kk, a))
    S_T, y = lax.scan(step, S0.astype(jnp.float32), seq)
    y = y.transpose(1, 0, 2, 3)
    mu = jnp.mean(y, axis=-1, keepdims=True)
    var = jnp.mean(jnp.square(y - mu), axis=-1, keepdims=True)
    yn = ((y - mu) * lax.rsqrt(var + RWKV_LN_EPS)).reshape(N, T, RWKV_W)
    yn = yn * lw['rwkv_ln_g'].astype(jnp.float32) + lw['rwkv_ln_b'].astype(jnp.float32)
    bonus = (jnp.sum(r * k * lw['rwkv_r_k'].astype(jnp.float32), axis=-1, keepdims=True) * v).reshape(N, T, RWKV_W)
    return (yn + bonus) * g, zc[:, -1], S_T


def decoder_layer(x, p_l, pos0, pool_hist, shift_prev, wkv0, kv_bufs, lw, prompt):
    N, T, _ = x.shape
    h = rms_norm(x, lw['norm_mix_g'])
    z = h @ lw['w_in']
    z_pool, q, k, v, z_rwkv, g_pool, g_attn, g_rwkv = split_cols(z, IN_SIZES)
    y_pool, new_pool = pool_mix(pool_hist, z_pool, pos0, lw['pool_w_grp'], lw['pool_scale'])
    pos = (pos0 + jnp.arange(T)).astype(jnp.float32)
    q = rope(q.reshape(N, T, ATT_HEADS, ATT_HD), pos)
    k = rope(k.reshape(N, T, ATT_HEADS, ATT_HD), pos)
    v = v.reshape(N, T, ATT_HEADS, ATT_HD)
    outs, lses, new_kv = [], [], []
    for gi, (win, dil) in enumerate(DIL_GROUPS):
        hs = slice(gi * ATT_HPG, (gi + 1) * ATT_HPG)
        qg, kg, vg = q[:, :, hs], k[:, :, hs], v[:, :, hs]
        if prompt:
            o, lse = band_attn(qg, kg, vg, dil, win // dil)
            keep = min(win, T)
            new_kv.append(jnp.stack([kg[:, T - keep:], vg[:, T - keep:]], axis=2))
        else:
            buf = kv_bufs[gi]
            hist = buf.shape[1]
            k_ext = jnp.concatenate([buf[:, :, 0].astype(kg.dtype), kg], axis=1)
            v_ext = jnp.concatenate([buf[:, :, 1].astype(vg.dtype), vg], axis=1)
            o, lse = gather_attn(qg, k_ext, v_ext, hist, dil, win // dil)
            keep = min(win, hist + T)
            new_kv.append(jnp.stack([k_ext[:, -keep:], v_ext[:, -keep:]], axis=2))
        outs.append(o)
        lses.append(lse)
    y_attn = merge_dilations(outs, lses)
    y_rwkv, new_shift, new_wkv = rwkv7_mix(z_rwkv, shift_prev, wkv0, lw)
    merged = (jax.nn.sigmoid(g_pool) * (y_pool @ lw['proj_pool'])
              + jax.nn.sigmoid(g_attn) * (y_attn @ lw['proj_attn'])
              + jax.nn.sigmoid(g_rwkv) * (y_rwkv @ lw['proj_rwkv']))
    x = x + (merged @ lw['w_out']).astype(x.dtype)
    h = rms_norm(x, lw['norm_ffn_g'])
    x = x + ((jax.nn.silu(h @ lw['ffn_w1']) * (h @ lw['ffn_w3'])) @ lw['ffn_w2']).astype(x.dtype)
    gate = jax.nn.sigmoid(rms_norm(x, lw['norm_ple_g']) @ lw['ple_gate'])
    x = x + ((p_l @ lw['ple_proj']) * gate).astype(x.dtype)
    return x, (new_pool, new_shift, new_wkv, new_kv)


def setup_inputs(seed: int = 0) -> dict:
    key = jax.random.key(seed)
    ks = iter(jax.random.split(key, 48))
    f32 = jnp.float32

    def nrm(shape, scale=1.0):
        return jax.random.normal(next(ks), shape, f32) * scale

    def gain(shape):
        return 1.0 + 0.02 * jax.random.normal(next(ks), shape, f32)

    L = DEPTH
    kv_len = [min(w, PAST_LEN) for w, _ in DIL_GROUPS]
    return {
        'x_prompt': nrm((BATCH, SEQ, D_MODEL)),
        'x_sample': nrm((DEC_BATCH, DEC_SEQ, D_MODEL)),
        'state_pool': nrm((L, DEC_BATCH, POOL_HIST, POOL_W)),
        'state_shift': nrm((L, DEC_BATCH, RWKV_PROJ)),
        'state_wkv': nrm((L, DEC_BATCH, RWKV_HEADS, RWKV_HD, RWKV_HD), 0.3),
        'cache_kv_w128': nrm((L, DEC_BATCH, kv_len[0], 2, ATT_HPG, ATT_HD)),
        'cache_kv_w512': nrm((L, DEC_BATCH, kv_len[1], 2, ATT_HPG, ATT_HD)),
        'cache_kv_w2048': nrm((L, DEC_BATCH, kv_len[2], 2, ATT_HPG, ATT_HD)),
        'p_prompt': nrm((L, BATCH, SEQ, PLE_DIM)),
        'p_sample': nrm((L, DEC_BATCH, DEC_SEQ, PLE_DIM)),
        'norm_mix_g': gain((L, D_MODEL)),
        'w_in': nrm((L, D_MODEL, IN_COLS), D_MODEL ** -0.5),
        'pool_w_grp': nrm((L, POOL_GROUPS, POOL_GC, POOL_GC), POOL_GC ** -0.5),
        'pool_scale': 1.0 + nrm((L, POOL_W), 0.1),
        'rwkv_mu': jax.random.uniform(next(ks), (L, RWKV_PROJ), f32),
        'rwkv_w0': nrm((L, RWKV_W), 0.5),
        'rwkv_w2': nrm((L, DECAY_LORA, RWKV_W), 0.5 * DECAY_LORA ** -0.5),
        'rwkv_a0': nrm((L, RWKV_W), 0.1),
        'rwkv_a2': nrm((L, AAA_LORA, RWKV_W), AAA_LORA ** -0.5),
        'rwkv_g2': nrm((L, GATE_LORA, RWKV_W), GATE_LORA ** -0.5),
        'rwkv_k_k': 1.0 + nrm((L, RWKV_W), 0.1),
        'rwkv_k_a': 1.0 + nrm((L, RWKV_W), 0.1),
        'rwkv_r_k': nrm((L, RWKV_HEADS, RWKV_HD), 0.1),
        'rwkv_ln_g': gain((L, RWKV_W)),
        'rwkv_ln_b': nrm((L, RWKV_W), 0.02),
        'proj_pool': nrm((L, POOL_W, D_MODEL), POOL_W ** -0.5),
        'proj_attn': nrm((L, ATT_OUT, D_MODEL), ATT_OUT ** -0.5),
        'proj_rwkv': nrm((L, RWKV_W, D_MODEL), RWKV_W ** -0.5),
        'w_out': nrm((L, D_MODEL, D_MODEL), D_MODEL ** -0.5),
        'norm_ffn_g': gain((L, D_MODEL)),
        'ffn_w1': nrm((L, D_MODEL, FFN_HIDDEN), D_MODEL ** -0.5),
        'ffn_w3': nrm((L, D_MODEL, FFN_HIDDEN), D_MODEL ** -0.5),
        'ffn_w2': nrm((L, FFN_HIDDEN, D_MODEL), FFN_HIDDEN ** -0.5),
        'norm_ple_g': gain((L, D_MODEL)),
        'ple_proj': nrm((L, PLE_DIM, D_MODEL), PLE_DIM ** -0.5),
        'ple_gate': nrm((L, D_MODEL, D_MODEL), D_MODEL ** -0.5),
        'norm_final_g': gain((D_MODEL,)),
    }


def reference(x_prompt, x_sample, state_pool, state_shift, state_wkv, cache_kv_w128, cache_kv_w512,
              cache_kv_w2048, p_prompt, p_sample, norm_mix_g, w_in, pool_w_grp, pool_scale, rwkv_mu,
              rwkv_w0, rwkv_w2, rwkv_a0, rwkv_a2, rwkv_g2, rwkv_k_k, rwkv_k_a, rwkv_r_k, rwkv_ln_g,
              rwkv_ln_b, proj_pool, proj_attn, proj_rwkv, w_out, norm_ffn_g, ffn_w1, ffn_w3, ffn_w2,
              norm_ple_g, ple_proj, ple_gate, norm_final_g):
    xp, xs = x_prompt, x_sample
    Np = xp.shape[0]
    pool_p, pool_s, shift_p, shift_s, wkv_p, wkv_s, kv_p, kv_s = [], [], [], [], [], [], [], []
    for i in range(DEPTH):
        lw = {
            'norm_mix_g': norm_mix_g[i], 'w_in': w_in[i], 'pool_w_grp': pool_w_grp[i],
            'pool_scale': pool_scale[i], 'rwkv_mu': rwkv_mu[i], 'rwkv_w0': rwkv_w0[i],
            'rwkv_w2': rwkv_w2[i], 'rwkv_a0': rwkv_a0[i], 'rwkv_a2': rwkv_a2[i], 'rwkv_g2': rwkv_g2[i],
            'rwkv_k_k': rwkv_k_k[i], 'rwkv_k_a': rwkv_k_a[i], 'rwkv_r_k': rwkv_r_k[i],
            'rwkv_ln_g': rwkv_ln_g[i], 'rwkv_ln_b': rwkv_ln_b[i], 'proj_pool': proj_pool[i],
            'proj_attn': proj_attn[i], 'proj_rwkv': proj_rwkv[i], 'w_out': w_out[i],
            'norm_ffn_g': norm_ffn_g[i], 'ffn_w1': ffn_w1[i], 'ffn_w3': ffn_w3[i], 'ffn_w2': ffn_w2[i],
            'norm_ple_g': norm_ple_g[i], 'ple_proj': ple_proj[i], 'ple_gate': ple_gate[i],
        }
        xp, (a_pool, a_shift, a_wkv, a_kv) = decoder_layer(
            xp, p_prompt[i], 0,
            jnp.zeros((Np, 0, POOL_W), xp.dtype),
            jnp.zeros((Np, RWKV_PROJ), jnp.float32),
            jnp.zeros((Np, RWKV_HEADS, RWKV_HD, RWKV_HD), jnp.float32),
            None, lw, True)
        xs, (b_pool, b_shift, b_wkv, b_kv) = decoder_layer(
            xs, p_sample[i], PAST_LEN, state_pool[i], state_shift[i], state_wkv[i],
            (cache_kv_w128[i], cache_kv_w512[i], cache_kv_w2048[i]), lw, False)
        pool_p.append(a_pool); shift_p.append(a_shift); wkv_p.append(a_wkv); kv_p.append(a_kv)
        pool_s.append(b_pool); shift_s.append(b_shift); wkv_s.append(b_wkv); kv_s.append(b_kv)
    y_prompt = rms_norm(xp, norm_final_g)
    y_sample = rms_norm(xs, norm_final_g)
    new_pool_p = jnp.stack(pool_p)
    new_pool_s = jnp.stack(pool_s)
    new_shift_p = jnp.stack(shift_p)
    new_shift_s = jnp.stack(shift_s)
    new_wkv_p = jnp.stack(wkv_p)
    new_wkv_s = jnp.stack(wkv_s)
    kv128_p = jnp.stack([kv[0] for kv in kv_p])
    kv128_s = jnp.stack([kv[0] for kv in kv_s])
    kv512_p = jnp.stack([kv[1] for kv in kv_p])
    kv512_s = jnp.stack([kv[1] for kv in kv_s])
    kv2048_p = jnp.stack([kv[2] for kv in kv_p])
    kv2048_s = jnp.stack([kv[2] for kv in kv_s])
    return (y_prompt, y_sample, new_pool_p, new_pool_s, new_shift_p, new_shift_s, new_wkv_p, new_wkv_s,
            kv128_p, kv128_s, kv512_p, kv512_s, kv2048_p, kv2048_s)
```

```python
import functools
import math

import jax
import jax.numpy as jnp
from jax import lax
from jax.experimental import pallas as pl
from jax.experimental.pallas import tpu as pltpu

F32 = jnp.float32
BF16 = jnp.bfloat16

POOL_GC = 64
POOL_W = 256
POOL_WINDOWS = (2, 4, 8, 16)
POOL_HIST = 15
POOL_HALO = 16
ATT_HD = 64
ATT_W = 384
ATT_PAIR = 128
DIL_GROUPS = ((128, 1), (512, 4), (2048, 16))
BAND_BLK = 128
ROPE_THETA = 10000.0
RWKV_HD = 64
RWKV_W = 384
RWKV_PROJ = 1408
RWKV_LN_EPS = 64e-5
RMS_EPS = 1e-6
PAST_LEN = 8192
NEG_BIG = -1e30

LANES = 128
SUBLANES = 8
VMEM_LIMIT = 56 * 1024 * 1024

NN = (((1,), (0,)), ((), ()))
NT = (((1,), (1,)), ((), ()))
TN = (((0,), (0,)), ((), ()))


def _params(*sem):
    return pltpu.CompilerParams(dimension_semantics=sem, vmem_limit_bytes=VMEM_LIMIT)


def _const_spec(shape):
    nd = len(shape)
    return pl.BlockSpec(shape, lambda *_: (0,) * nd, pipeline_mode=pl.Buffered(1))


def _bdot(a, b, dims=NN):
    return lax.dot_general(a.astype(BF16), b.astype(BF16), dims, preferred_element_type=F32)


def _split(a):
    hi = a.astype(BF16)
    lo = (a - hi.astype(F32)).astype(BF16)
    return hi, lo


def _mm3(a, b, dims=NN):
    ah, al = _split(a)
    bh, bl = _split(b)
    f = lambda x, y: lax.dot_general(x, y, dims, preferred_element_type=F32)
    return f(ah, bh) + (f(ah, bl) + f(al, bh))


def _mm2_exact_rhs(a, b_bf16, dims=NN):
    ah, al = _split(a)
    f = lambda x: lax.dot_general(x, b_bf16, dims, preferred_element_type=F32)
    return f(ah) + f(al)


def _rms(x, g):
    return x * lax.rsqrt(jnp.mean(x * x, axis=-1, keepdims=True) + RMS_EPS) * g


def _sigmoid(x):
    return 1.0 / (1.0 + jnp.exp(-x))


def _in_proj_kernel(x_ref, g_ref, w_ref, cos_ref, sin_ref,
                    pool_ref, q_ref, k_ref, v_ref, rw_ref, gate_ref):
    h = _rms(x_ref[...], g_ref[...]).astype(BF16)

    def seg(a, b):
        return jnp.dot(h, w_ref[:, a:b], preferred_element_type=F32)

    cos = cos_ref[...]
    sin = sin_ref[...]
    lane = lax.broadcasted_iota(jnp.int32, cos.shape, 1)
    low_half = (lane & 32) == 0

    def rope(t):
        partner = jnp.where(low_half, pltpu.roll(t, 96, 1), pltpu.roll(t, 32, 1))
        return t * cos + partner * sin

    o = 0
    pool_ref[...] = seg(o, o + POOL_W)
    o += POOL_W
    qz = seg(o, o + ATT_W)
    o += ATT_W
    kz = seg(o, o + ATT_W)
    o += ATT_W
    for c in range(ATT_W // LANES):
        cs = slice(c * LANES, (c + 1) * LANES)
        q_ref[:, cs] = rope(qz[:, cs]) * (ATT_HD ** -0.5)
        k_ref[:, cs] = rope(kz[:, cs])
    v_ref[...] = seg(o, o + ATT_W)
    o += ATT_W
    rw_ref[...] = seg(o, o + RWKV_PROJ)
    o += RWKV_PROJ
    gate_ref[...] = seg(o, w_ref.shape[1])


def _in_proj(x2d, g, w_bf, cos, sin, tm):
    m, d = x2d.shape
    ncol = w_bf.shape[1]
    ngate = ncol - (POOL_W + 3 * ATT_W + RWKV_PROJ)
    ntab = cos.shape[0] // tm
    row = lambda w: pl.BlockSpec((tm, w), lambda i: (i, 0))
    tab = pl.BlockSpec((tm, LANES), lambda i: (i % ntab, 0))
    widths = (POOL_W, ATT_W, ATT_W, ATT_W, RWKV_PROJ, ngate)
    return pl.pallas_call(
        _in_proj_kernel,
        grid=(m // tm,),
        in_specs=[row(d), _const_spec((1, d)), _const_spec((d, ncol)), tab, tab],
        out_specs=[row(w) for w in widths],
        out_shape=[jax.ShapeDtypeStruct((m, w), F32) for w in widths],
        compiler_params=_params("parallel"),
        name="in_proj",
    )(x2d, g, w_bf, cos, sin)


def _pool_kernel(prev_ref, cur_ref, w_ref, scale_ref, y_ref, ext_ref, *, pos0, zero_first_prev):
    bn, tm, w = cur_ref.shape
    j = pl.program_id(1)
    prev = prev_ref[...]
    if zero_first_prev:
        prev = jnp.where(j == 0, 0.0, prev)
    cur = cur_ref[...]
    ext_ref[:, :POOL_HALO, :] = prev
    ext_ref[:, POOL_HALO:, :] = cur

    lane = lax.broadcasted_iota(jnp.int32, (bn, tm, w), 2)
    group = jnp.right_shift(lane, POOL_GC.bit_length() - 1)
    acc = cur
    win = jnp.zeros_like(cur)
    for s in range(1, POOL_WINDOWS[-1] + 1):
        if s in POOL_WINDOWS:
            win = jnp.where(group == POOL_WINDOWS.index(s), acc, win)
        if s < POOL_WINDOWS[-1]:
            acc = acc + ext_ref[:, POOL_HALO - s:POOL_HALO - s + tm, :]
    width = jnp.left_shift(2, group)
    pos = pos0 + j * tm + lax.broadcasted_iota(jnp.int32, (bn, tm, w), 1)
    cnt = jnp.minimum(pos + 1, width).astype(F32)
    dlt = (win / cnt - cur).reshape(bn * tm, w)
    y = _bdot(dlt, w_ref[...]) * scale_ref[...]
    y_ref[...] = y.reshape(bn, tm, w)


def _pool(prev_arr, cur3, w_bd, scale, *, bn, tm, pos0, prev_is_self):
    n, t, w = cur3.shape
    if prev_is_self:
        per = tm // POOL_HALO
        prev_spec = pl.BlockSpec((bn, POOL_HALO, w), lambda b, j: (b, jnp.maximum(j * per - 1, 0), 0))
    else:
        prev_spec = pl.BlockSpec((bn, POOL_HALO, w), lambda b, j: (b, 0, 0))
    return pl.pallas_call(
        functools.partial(_pool_kernel, pos0=pos0, zero_first_prev=prev_is_self),
        grid=(n // bn, t // tm),
        in_specs=[prev_spec,
                  pl.BlockSpec((bn, tm, w), lambda b, j: (b, j, 0)),
                  _const_spec((w, w)), _const_spec((1, w))],
        out_specs=pl.BlockSpec((bn, tm, w), lambda b, j: (b, j, 0)),
        out_shape=jax.ShapeDtypeStruct((n, t, w), F32),
        scratch_shapes=[pltpu.VMEM((bn, tm + POOL_HALO, w), F32)],
        compiler_params=_params("parallel", "parallel"),
        name="pool_mix",
    )(prev_arr, cur3, w_bd, scale)


def _softmax_pair(q, k, v, valid):
    m_rows = q.shape[0]
    lane = lax.broadcasted_iota(jnp.int32, (m_rows, ATT_PAIR), 1)
    head0 = lane < ATT_HD
    kb = k.astype(BF16)
    vb = v.astype(BF16)
    outs, lses = [], []
    for hs in range(2):
        hm = head0 if hs == 0 else jnp.logical_not(head0)
        qm = jnp.where(hm, q, 0.0).astype(BF16)
        s = lax.dot_general(qm, kb, NT, preferred_element_type=F32)
        s = jnp.where(valid, s, NEG_BIG)
        mx = jnp.max(s, axis=-1, keepdims=True)
        e = jnp.exp(s - mx)
        den = jnp.sum(e, axis=-1, keepdims=True)
        o = jnp.dot(e.astype(BF16), vb, preferred_element_type=F32) / den
        outs.append(o)
        lses.append(jnp.broadcast_to(mx + jnp.log(den), o.shape))
    return jnp.where(head0, outs[0], outs[1]), jnp.where(head0, lses[0], lses[1])


def _band_attn_kernel(q_ref, kp_ref, kc_ref, vp_ref, vc_ref, o_ref, lse_ref, *, span):
    i = pl.program_id(2)
    tq = q_ref.shape[1]
    q = q_ref[0]
    kcat = jnp.concatenate([kp_ref[0], kc_ref[0]], axis=0)
    vcat = jnp.concatenate([vp_ref[0], vc_ref[0]], axis=0)
    qi = lax.broadcasted_iota(jnp.int32, (BAND_BLK, 2 * BAND_BLK), 0)
    kj = lax.broadcasted_iota(jnp.int32, (BAND_BLK, 2 * BAND_BLK), 1)
    dist = BAND_BLK + qi - kj
    band = (dist >= 0) & (dist <= span)
    for jb in range(tq // BAND_BLK):
        lo = jb * BAND_BLK
        valid = band
        if jb == 0:
            valid = band & (kj + jnp.where(i > 0, BAND_BLK, 0) >= BAND_BLK)
        o, lse = _softmax_pair(q[lo:lo + BAND_BLK], kcat[lo:lo + 2 * BAND_BLK],
                               vcat[lo:lo + 2 * BAND_BLK], valid)
        o_ref[0, lo:lo + BAND_BLK, :] = o
        lse_ref[0, lo:lo + BAND_BLK, :] = lse


def _band_attn(q3, k3, v3, gi, dil, span, tq):
    n, t, _ = q3.shape
    ls = t // dil
    tq = min(tq, ls)
    nchunk = ATT_W // LANES
    view = lambda a: a.reshape(n, ls, dil * ATT_W)
    per = tq // BAND_BLK
    cur = pl.BlockSpec((1, tq, LANES), lambda b, r, i: (b, i, r * nchunk + gi))
    prev = pl.BlockSpec((1, BAND_BLK, LANES),
                        lambda b, r, i: (b, jnp.maximum(i * per - 1, 0), r * nchunk + gi))
    out = pl.BlockSpec((1, tq, LANES), lambda b, r, i: (b, i, r))
    o, lse = pl.pallas_call(
        functools.partial(_band_attn_kernel, span=span),
        grid=(n, dil, ls // tq),
        in_specs=[cur, prev, cur, prev, cur],
        out_specs=[out, out],
        out_shape=[jax.ShapeDtypeStruct((n, ls, dil * LANES), F32)] * 2,
        compiler_params=_params("parallel", "parallel", "parallel"),
        name=f"band_attn_d{dil}",
    )(view(q3), view(k3), view(k3), view(v3), view(v3))
    return o.reshape(n, t, LANES), lse.reshape(n, t, LANES)


def _gather_attn_kernel(buf_ref, q_ref, kn_ref, vn_ref, o_ref, lse_ref, nbuf_ref, *, dil, span, keep):
    hist = buf_ref.shape[1]
    t = q_ref.shape[1]
    buf = buf_ref[0]
    kn = kn_ref[0]
    vn = vn_ref[0]
    k_ext = jnp.concatenate([buf[:, :ATT_PAIR], kn], axis=0)
    v_ext = jnp.concatenate([buf[:, ATT_PAIR:], vn], axis=0)
    mq = 2 * SUBLANES
    q = jnp.concatenate([q_ref[0], jnp.zeros((mq - t, ATT_PAIR), F32)], axis=0)
    ti = lax.broadcasted_iota(jnp.int32, (mq, hist + t), 0)
    pj = lax.broadcasted_iota(jnp.int32, (mq, hist + t), 1)
    delta = hist + ti - pj
    valid = (delta >= 0) & (delta <= dil * span) & ((delta & (dil - 1)) == 0)
    o, lse = _softmax_pair(q, k_ext, v_ext, valid)
    o_ref[0] = o[:t]
    lse_ref[0] = lse[:t]
    first = hist + t - keep
    if keep > t:
        nbuf_ref[0, :keep - t, :] = buf[first:hist, :]
    nbuf_ref[0, keep - t:, :] = jnp.concatenate([kn, vn], axis=1)


def _gather_attn(buf, q3, k3, v3, gi, win, dil):
    n, hist, w = buf.shape
    t = q3.shape[1]
    keep = min(win, hist + t)
    assert dil & (dil - 1) == 0 and t % SUBLANES == 0 and keep >= t
    new = pl.BlockSpec((1, t, LANES), lambda b: (b, 0, gi))
    out = pl.BlockSpec((1, t, LANES), lambda b: (b, 0, 0))
    return pl.pallas_call(
        functools.partial(_gather_attn_kernel, dil=dil, span=win // dil, keep=keep),
        grid=(n,),
        in_specs=[pl.BlockSpec((1, hist, w), lambda b: (b, 0, 0)), new, new, new],
        out_specs=[out, out, pl.BlockSpec((1, keep, w), lambda b: (b, 0, 0))],
        out_shape=[jax.ShapeDtypeStruct((n, t, LANES), F32)] * 2
        + [jax.ShapeDtypeStruct((n, keep, w), F32)],
        compiler_params=_params("parallel"),
        name=f"gather_attn_d{dil}",
    )(buf, q3, k3, v3)


def _rwkv_prep_kernel(prev8_ref, sp_ref, cur_ref, mu_ref, wwa_ref, w0_ref, a0_ref, g2_ref,
                      kk_ref, ka_ref, rk_ref, ones_ref, tril_ref,
                      at_ref, bt_ref, kt_ref, rt_ref, v_ref, pl_ref, g_ref, bonus_ref,
                      sh_ref, p_ref, *, chunk):
    bn, tm, w = cur_ref.shape
    j = pl.program_id(1)
    cur = cur_ref[...]
    prev_row = jnp.where(j == 0, sp_ref[...], prev8_ref[:, SUBLANES - 1:SUBLANES, :])
    sh_ref[:, SUBLANES - 1:SUBLANES, :] = prev_row
    sh_ref[:, SUBLANES:, :] = cur
    prev = sh_ref[:, SUBLANES - 1:SUBLANES - 1 + tm, :]
    m = bn * tm
    zs = (cur + (prev - cur) * mu_ref[...]).reshape(m, w)

    r = zs[:, 0:RWKV_W]
    k = zs[:, RWKV_W:2 * RWKV_W]
    v = zs[:, 2 * RWKV_W:3 * RWKV_W]
    lo = 3 * RWKV_W
    z_wa = zs[:, lo:lo + LANES]
    z_g = zs[:, lo + LANES:lo + 2 * LANES]
    lane = lax.broadcasted_iota(jnp.int32, z_wa.shape, 1)
    u = jnp.where(lane < LANES // 2, jnp.tanh(z_wa), z_wa)
    lora = _bdot(u, wwa_ref[...])
    xw = w0_ref[...] + lora[:, :RWKV_W]
    w_log = -(jnp.maximum(-xw, 0.0) + jnp.log(1.0 + jnp.exp(-jnp.abs(xw)))) - 0.5
    e = jnp.exp(w_log)
    a = _sigmoid(a0_ref[...] + lora[:, RWKV_W:])
    g_ref[...] = _bdot(_sigmoid(z_g), g2_ref[...])

    ones_bd = ones_ref[...]
    kk = k * kk_ref[...]
    kk = kk * lax.rsqrt(jnp.maximum(_mm2_exact_rhs(kk * kk, ones_bd), 1e-24))
    k = k * (1.0 + (a - 1.0) * ka_ref[...])
    bonus_ref[...] = _mm2_exact_rhs(r * k * rk_ref[...], ones_bd) * v

    tril = tril_ref[...]
    e1 = e.astype(BF16)
    rem = e - e1.astype(F32)
    e2 = rem.astype(BF16)
    e3 = (rem - e2.astype(F32)).astype(BF16)
    f = lambda x: jnp.dot(tril, x, preferred_element_type=F32)
    cum = f(e1) + (f(e2) + f(e3))
    p_inc = jnp.exp(-cum)
    p_inv = jnp.exp(cum)
    at_ref[...] = -kk * jnp.exp(e - cum)
    bt_ref[...] = kk * a * p_inv
    kt_ref[...] = k * p_inv
    rt_ref[...] = r * p_inc
    v_ref[...] = v
    p_ref[...] = p_inc
    for c in range(m // chunk):
        pl_ref[c] = p_ref[(c + 1) * chunk - 1:(c + 1) * chunk, :]


def _rwkv_prep(zr3, shift_prev, lw, *, bn, tm, chunk):
    n, t, w = zr3.shape
    m = bn * tm
    per = tm // SUBLANES
    rowblk = pl.BlockSpec((m, RWKV_W), lambda b, j: (b * (t // tm) + j, 0))
    tril = (jnp.arange(m)[:, None] >= jnp.arange(m)[None, :]) & \
           (jnp.arange(m)[:, None] // chunk == jnp.arange(m)[None, :] // chunk)
    nch = m // chunk
    consts = [lw['rwkv_mu'], lw['rwkv_wwa'], lw['rwkv_w0'], lw['rwkv_a0'], lw['rwkv_g2'],
              lw['rwkv_k_k'], lw['rwkv_k_a'], lw['rwkv_r_k'], lw['ones_bd'], tril.astype(BF16)]
    outs = pl.pallas_call(
        functools.partial(_rwkv_prep_kernel, chunk=chunk),
        grid=(n // bn, t // tm),
        in_specs=[pl.BlockSpec((bn, SUBLANES, w), lambda b, j: (b, jnp.maximum(j * per - 1, 0), 0)),
                  pl.BlockSpec((bn, 1, w), lambda b, j: (b, 0, 0)),
                  pl.BlockSpec((bn, tm, w), lambda b, j: (b, j, 0))]
        + [_const_spec(c.shape) for c in consts],
        out_specs=[rowblk] * 5
        + [pl.BlockSpec((nch, 1, RWKV_W), lambda b, j: (b * (t // tm) + j, 0, 0)), rowblk, rowblk],
        out_shape=[jax.ShapeDtypeStruct((n * t, RWKV_W), F32)] * 5
        + [jax.ShapeDtypeStruct((n * t // chunk, 1, RWKV_W), F32)]
        + [jax.ShapeDtypeStruct((n * t, RWKV_W), F32)] * 2,
        scratch_shapes=[pltpu.VMEM((bn, tm + SUBLANES, w), F32), pltpu.VMEM((m, RWKV_W), F32)],
        compiler_params=_params("parallel", "parallel"),
        name="rwkv_prep",
    )(zr3, shift_prev, zr3, *consts)
    return outs


def _rwkv_scan_kernel(at_ref, bt_ref, kt_ref, rt_ref, v_ref, pl_ref, s0_ref, y_ref, sT_ref, s_ref):
    c = pl.program_id(1)
    chunk = at_ref.shape[1]

    @pl.when(c == 0)
    def _():
        s_ref[...] = s0_ref[0, 0]

    at, bt, kt, rt, v = at_ref[0], bt_ref[0], kt_ref[0], rt_ref[0], v_ref[0]
    s = s_ref[...]
    lane = lax.broadcasted_iota(jnp.int32, (chunk, 2 * RWKV_HD), 1)
    head0 = lane < RWKV_HD
    row = lax.broadcasted_iota(jnp.int32, (chunk, chunk), 0)
    col = lax.broadcasted_iota(jnp.int32, (chunk, chunk), 1)
    strict = col < row
    incl = col <= row
    eye = jnp.where(col == row, 1.0, 0.0).astype(F32)

    x0 = _mm3(at, s, NT)
    cs, ys = [], []
    for hs in range(2):
        hm = head0 if hs == 0 else jnp.logical_not(head0)
        am = jnp.where(hm, at, 0.0)
        rm = jnp.where(hm, rt, 0.0)
        m_ab = jnp.where(strict, _mm3(am, bt, NT), 0.0)
        m_ak = jnp.where(strict, _mm3(am, kt, NT), 0.0)
        m_rb = jnp.where(incl, _mm3(rm, bt, NT), 0.0)
        m_rk = jnp.where(incl, _mm3(rm, kt, NT), 0.0)
        inv = eye + m_ab
        npow = m_ab
        for _ in range(int(math.log2(chunk)) - 1):
            npow = _mm3(npow, npow)
            inv = inv + _mm3(npow, inv)
        ch = _mm3(inv, x0 + _mm3(m_ak, v))
        cs.append(ch)
        ys.append((m_rb, m_rk))
    cmat = jnp.where(head0, cs[0], cs[1])
    y_heads = [_mm3(m_rb, cmat) + _mm3(m_rk, v) for m_rb, m_rk in ys]
    y_ref[0] = _mm3(rt, s, NT) + jnp.where(head0, y_heads[0], y_heads[1])

    ds = _mm3(cmat, bt, TN) + _mm3(v, kt, TN)
    r2 = lax.broadcasted_iota(jnp.int32, ds.shape, 0) < RWKV_HD
    c2 = lax.broadcasted_iota(jnp.int32, ds.shape, 1) < RWKV_HD
    s_new = (s + jnp.where(r2 == c2, ds, 0.0)) * pl_ref[0]
    s_ref[...] = s_new

    @pl.when(c == pl.num_programs(1) - 1)
    def _():
        sT_ref[0, 0] = s_new


def _rwkv_scan(at, bt, kt, rt, v, plast, s0, *, n, t, chunk):
    npair = RWKV_W // LANES
    nch = t // chunk
    tok = pl.BlockSpec((1, chunk, LANES), lambda b, c: (b // npair, c, b % npair))
    st = pl.BlockSpec((1, 1, LANES, LANES), lambda b, c: (b // npair, b % npair, 0, 0))
    v3 = lambda a: a.reshape(n, t, RWKV_W)
    y, s_t = pl.pallas_call(
        _rwkv_scan_kernel,
        grid=(n * npair, nch),
        in_specs=[tok] * 5
        + [pl.BlockSpec((1, 1, LANES), lambda b, c: ((b // npair) * nch + c, 0, b % npair)), st],
        out_specs=[tok, st],
        out_shape=[jax.ShapeDtypeStruct((n, t, RWKV_W), F32),
                   jax.ShapeDtypeStruct((n, npair, LANES, LANES), F32)],
        scratch_shapes=[pltpu.VMEM((LANES, LANES), F32)],
        compiler_params=_params("parallel", "arbitrary"),
        name="rwkv_scan",
    )(v3(at), v3(bt), v3(kt), v3(rt), v3(v), plast, s0)
    return y, s_t


def _merge_kernel(x_ref, yp_ref, o0_ref, o1_ref, o2_ref, l0_ref, l1_ref, l2_ref,
                  ys_ref, bonus_ref, g_ref, gate_ref,
                  pp_ref, pa_ref, pr_ref, wo_ref, lng_ref, lnb_ref, mean_ref, out_ref):
    d = x_ref.shape[1]
    l0, l1, l2 = l0_ref[...], l1_ref[...], l2_ref[...]
    mx = jnp.maximum(jnp.maximum(l0, l1), l2)
    e0, e1, e2 = jnp.exp(l0 - mx), jnp.exp(l1 - mx), jnp.exp(l2 - mx)
    y_attn = (e0 * o0_ref[...] + e1 * o1_ref[...] + e2 * o2_ref[...]) / (e0 + e1 + e2)

    ys = ys_ref[...]
    mean_bd = mean_ref[...]
    mu = _mm2_exact_rhs(ys, mean_bd)
    dev = ys - mu
    var = _mm2_exact_rhs(dev * dev, mean_bd)
    yn = dev * lax.rsqrt(var + RWKV_LN_EPS) * lng_ref[...] + lnb_ref[...]
    y_rwkv = (yn + bonus_ref[...]) * g_ref[...]

    merged = (_sigmoid(gate_ref[:, 0:d]) * _bdot(yp_ref[...], pp_ref[...])
              + _sigmoid(gate_ref[:, d:2 * d]) * _bdot(y_attn, pa_ref[...])
              + _sigmoid(gate_ref[:, 2 * d:3 * d]) * _bdot(y_rwkv, pr_ref[...]))
    out_ref[...] = x_ref[...] + _bdot(merged, wo_ref[...])


def _merge(x2d, yp, os_, ls_, ys, bonus, g, gates, lw, tm):
    m, d = x2d.shape
    row = lambda w: pl.BlockSpec((tm, w), lambda i: (i, 0))
    consts = [lw['proj_pool'], lw['proj_attn'], lw['proj_rwkv'], lw['w_out'],
              lw['rwkv_ln_g'], lw['rwkv_ln_b'], lw['mean_bd']]
    return pl.pallas_call(
        _merge_kernel,
        grid=(m // tm,),
        in_specs=[row(d), row(POOL_W)] + [row(LANES)] * 6 + [row(RWKV_W)] * 3 + [row(3 * d)]
        + [_const_spec(c.shape) for c in consts],
        out_specs=row(d),
        out_shape=jax.ShapeDtypeStruct((m, d), F32),
        compiler_params=_params("parallel"),
        name="merge_out_proj",
    )(x2d, yp, *os_, *ls_, ys, bonus, g, gates, *consts)


def _ffn_kernel(x_ref, g_ref, w1_ref, w3_ref, w2_ref, out_ref):
    x = x_ref[...]
    h = _rms(x, g_ref[...]).astype(BF16)
    h1 = jnp.dot(h, w1_ref[...], preferred_element_type=F32)
    h3 = jnp.dot(h, w3_ref[...], preferred_element_type=F32)
    act = h1 * _sigmoid(h1) * h3
    out_ref[...] = x + _bdot(act, w2_ref[...])


def _ffn(x2d, lw, tm):
    m, d = x2d.shape
    row = pl.BlockSpec((tm, d), lambda i: (i, 0))
    consts = [lw['norm_ffn_g'], lw['ffn_w1'], lw['ffn_w3'], lw['ffn_w2']]
    return pl.pallas_call(
        _ffn_kernel,
        grid=(m // tm,),
        in_specs=[row] + [_const_spec(c.shape) for c in consts],
        out_specs=row,
        out_shape=jax.ShapeDtypeStruct((m, d), F32),
        compiler_params=_params("parallel"),
        name="swiglu",
    )(x2d, *consts)


def _ple_kernel(x_ref, p_ref, g_ref, wg_ref, wp_ref, gf_ref, out_ref, *, final_norm):
    x = x_ref[...]
    gate = _sigmoid(_bdot(_rms(x, g_ref[...]), wg_ref[...]))
    y = x + _bdot(p_ref[...], wp_ref[...]) * gate
    if final_norm:
        y = _rms(y, gf_ref[...])
    out_ref[...] = y


def _ple(x2d, p2d, lw, g_final, tm, final_norm):
    m, d = x2d.shape
    row = lambda w: pl.BlockSpec((tm, w), lambda i: (i, 0))
    consts = [lw['norm_ple_g'], lw['ple_gate'], lw['ple_proj'], g_final]
    return pl.pallas_call(
        functools.partial(_ple_kernel, final_norm=final_norm),
        grid=(m // tm,),
        in_specs=[row(d), row(p2d.shape[1])] + [_const_spec(c.shape) for c in consts],
        out_specs=row(d),
        out_shape=jax.ShapeDtypeStruct((m, d), F32),
        compiler_params=_params("parallel"),
        name="ple_final" if final_norm else "ple",
    )(x2d, p2d, *consts)


def _rope_tables(pos, reps):
    half = ATT_HD // 2
    inv = ROPE_THETA ** (-2.0 * jnp.arange(half, dtype=F32) / ATT_HD)
    ang = pos.astype(F32)[:, None] * inv[None, :]
    cos = jnp.cos(ang)
    sin = jnp.sin(ang)
    cos = jnp.tile(jnp.concatenate([cos, cos], axis=1), (reps, LANES // ATT_HD))
    sin = jnp.tile(jnp.concatenate([-sin, sin], axis=1), (reps, LANES // ATT_HD))
    return cos, sin


def _block_diag(blocks):
    g, a, b = blocks.shape
    eye = jnp.eye(g, dtype=blocks.dtype)
    return (eye[:, None, :, None] * blocks[:, :, None, :]).reshape(g * a, g * b)


def _layer_weights(i, w):
    row = lambda a: a[i][None, :]
    bf = lambda a: a[i].astype(BF16)
    heads = RWKV_W // RWKV_HD
    zero = jnp.zeros_like(w['rwkv_w2'][i])
    wwa = jnp.concatenate([jnp.concatenate([w['rwkv_w2'][i], zero], axis=1),
                           jnp.concatenate([zero, w['rwkv_a2'][i]], axis=1)], axis=0)
    ones = jnp.ones((heads, RWKV_HD, RWKV_HD), F32)
    return {
        'norm_mix_g': row(w['norm_mix_g']), 'w_in': bf(w['w_in']),
        'pool_w_bd': _block_diag(w['pool_w_grp'][i]).astype(BF16), 'pool_scale': row(w['pool_scale']),
        'rwkv_mu': row(w['rwkv_mu']), 'rwkv_wwa': wwa.astype(BF16),
        'rwkv_w0': row(w['rwkv_w0']), 'rwkv_a0': row(w['rwkv_a0']), 'rwkv_g2': bf(w['rwkv_g2']),
        'rwkv_k_k': row(w['rwkv_k_k']), 'rwkv_k_a': row(w['rwkv_k_a']),
        'rwkv_r_k': w['rwkv_r_k'][i].reshape(1, RWKV_W),
        'rwkv_ln_g': row(w['rwkv_ln_g']), 'rwkv_ln_b': row(w['rwkv_ln_b']),
        'ones_bd': _block_diag(ones).astype(BF16),
        'mean_bd': _block_diag(ones / RWKV_HD).astype(BF16),
        'proj_pool': bf(w['proj_pool']), 'proj_attn': bf(w['proj_attn']),
        'proj_rwkv': bf(w['proj_rwkv']), 'w_out': bf(w['w_out']),
        'norm_ffn_g': row(w['norm_ffn_g']), 'ffn_w1': bf(w['ffn_w1']), 'ffn_w3': bf(w['ffn_w3']),
        'ffn_w2': bf(w['ffn_w2']), 'norm_ple_g': row(w['norm_ple_g']),
        'ple_proj': bf(w['ple_proj']), 'ple_gate': bf(w['ple_gate']),
    }


def _pair_states(s):
    n, h, a, b = s.shape
    s = s.reshape(n * (h // 2), 2, a, b)
    return jax.vmap(_block_diag)(s).reshape(n, h // 2, 2 * a, 2 * b)


def _unpair_states(s2):
    n, p, a, b = s2.shape
    s2 = s2.reshape(n, p, 2, a // 2, 2, b // 2)
    return jnp.stack([s2[:, :, 0, :, 0, :], s2[:, :, 1, :, 1, :]], axis=2).reshape(n, 2 * p, a // 2, b // 2)


def _decoder_layer(x2d, n, t, p2d, lw, g_final, final_norm, tabs, state, *, prompt, tm, chunk):
    d = x2d.shape[1]
    zp, q, k, v, zr, gates = _in_proj(x2d, lw['norm_mix_g'], lw['w_in'], tabs[0], tabs[1], tm)
    three = lambda a: a.reshape(n, t, a.shape[-1])
    zp3, q3, k3, v3, zr3 = three(zp), three(q), three(k), three(v), three(zr)

    if prompt:
        yp = _pool(zp3, zp3, lw['pool_w_bd'], lw['pool_scale'], bn=1, tm=tm, pos0=0, prev_is_self=True)
        new_pool = zp3[:, t - POOL_HIST:]
    else:
        hist = jnp.pad(state['pool'], ((0, 0), (POOL_HALO - POOL_HIST, 0), (0, 0)))
        yp = _pool(hist, zp3, lw['pool_w_bd'], lw['pool_scale'], bn=n, tm=t,
                   pos0=state['pos0'], prev_is_self=False)
        new_pool = jnp.concatenate([state['pool'], zp3], axis=1)[:, -POOL_HIST:]

    os_, ls_, new_kv = [], [], []
    for gi, (win, dil) in enumerate(DIL_GROUPS):
        cols = slice(gi * ATT_PAIR, (gi + 1) * ATT_PAIR)
        if prompt:
            o, lse = _band_attn(q3, k3, v3, gi, dil, win // dil, tq=256)
            keep = min(win, t)
            kv = jnp.concatenate([k3[:, t - keep:, cols], v3[:, t - keep:, cols]], axis=-1)
        else:
            buf = state['kv'][gi]
            o, lse, kv = _gather_attn(buf.reshape(buf.shape[0], buf.shape[1], -1), q3, k3, v3, gi, win, dil)
            keep = kv.shape[1]
        os_.append(o.reshape(n * t, LANES))
        ls_.append(lse.reshape(n * t, LANES))
        new_kv.append(kv.reshape(n, keep, 2, 2, ATT_HD))

    if prompt:
        at, bt, kt, rt, vv, plast, g, bonus = _rwkv_prep(zr3, state['shift'][:, None, :], lw,
                                                         bn=1, tm=tm, chunk=chunk)
        ys, s_t = _rwkv_scan(at, bt, kt, rt, vv, plast, state['wkv'], n=n, t=t, chunk=chunk)
    else:
        at, bt, kt, rt, vv, plast, g, bonus = _rwkv_prep(zr3, state['shift'][:, None, :], lw,
                                                         bn=n, tm=t, chunk=t)
        padt = lambda a: jnp.pad(a.reshape(n, t, RWKV_W), ((0, 0), (0, chunk - t), (0, 0))).reshape(n * chunk, RWKV_W)
        ys, s_t = _rwkv_scan(padt(at), padt(bt), padt(kt), padt(rt), padt(vv), plast, state['wkv'],
                             n=n, t=chunk, chunk=chunk)
        ys = ys[:, :t]
    new_shift = zr3[:, t - 1]

    x2d = _merge(x2d, yp.reshape(n * t, POOL_W), os_, ls_, ys.reshape(n * t, RWKV_W), bonus, g, gates, lw, tm)
    x2d = _ffn(x2d, lw, tm)
    x2d = _ple(x2d, p2d, lw, g_final, tm, final_norm)
    return x2d, (new_pool, new_shift, _unpair_states(s_t), new_kv)


def kernel(x_prompt, x_sample, state_pool, state_shift, state_wkv, cache_kv_w128, cache_kv_w512,
           cache_kv_w2048, p_prompt, p_sample, norm_mix_g, w_in, pool_w_grp, pool_scale, rwkv_mu,
           rwkv_w0, rwkv_w2, rwkv_a0, rwkv_a2, rwkv_g2, rwkv_k_k, rwkv_k_a, rwkv_r_k, rwkv_ln_g,
           rwkv_ln_b, proj_pool, proj_attn, proj_rwkv, w_out, norm_ffn_g, ffn_w1, ffn_w3, ffn_w2,
           norm_ple_g, ple_proj, ple_gate, norm_final_g):
    weights = dict(norm_mix_g=norm_mix_g, w_in=w_in, pool_w_grp=pool_w_grp, pool_scale=pool_scale,
                   rwkv_mu=rwkv_mu, rwkv_w0=rwkv_w0, rwkv_w2=rwkv_w2, rwkv_a0=rwkv_a0, rwkv_a2=rwkv_a2,
                   rwkv_g2=rwkv_g2, rwkv_k_k=rwkv_k_k, rwkv_k_a=rwkv_k_a, rwkv_r_k=rwkv_r_k,
                   rwkv_ln_g=rwkv_ln_g, rwkv_ln_b=rwkv_ln_b, proj_pool=proj_pool, proj_attn=proj_attn,
                   proj_rwkv=proj_rwkv, w_out=w_out, norm_ffn_g=norm_ffn_g, ffn_w1=ffn_w1, ffn_w3=ffn_w3,
                   ffn_w2=ffn_w2, norm_ple_g=norm_ple_g, ple_proj=ple_proj, ple_gate=ple_gate)
    depth = w_in.shape[0]
    np_, tp, d = x_prompt.shape
    ns, ts, _ = x_sample.shape
    caches = (cache_kv_w128, cache_kv_w512, cache_kv_w2048)
    past_len = PAST_LEN
    tm_p = min(256, np_ * tp)
    tm_s = ns * ts
    chunk = 64
    tabs_p = _rope_tables(jnp.arange(tp), 1)
    tabs_s = _rope_tables(past_len + jnp.arange(ts), ns)
    g_final = norm_final_g[None, :]

    xp = x_prompt.reshape(np_ * tp, d)
    xs = x_sample.reshape(ns * ts, d)
    heads = RWKV_W // RWKV_HD
    zero_state = {
        'shift': jnp.zeros((np_, RWKV_PROJ), F32),
        'wkv': jnp.zeros((np_, heads // 2, 2 * RWKV_HD, 2 * RWKV_HD), F32),
    }
    outs_p, outs_s = [], []
    for i in range(depth):
        lw = _layer_weights(i, weights)
        last = i == depth - 1
        xp, st_p = _decoder_layer(xp, np_, tp, p_prompt[i].reshape(np_ * tp, -1), lw, g_final, last,
                                  tabs_p, zero_state, prompt=True, tm=tm_p, chunk=chunk)
        state_s = {'pool': state_pool[i], 'shift': state_shift[i], 'wkv': _pair_states(state_wkv[i]),
                   'kv': [c[i] for c in caches], 'pos0': past_len}
        xs, st_s = _decoder_layer(xs, ns, ts, p_sample[i].reshape(ns * ts, -1), lw, g_final, last,
                                  tabs_s, state_s, prompt=False, tm=tm_s, chunk=chunk)
        outs_p.append(st_p)
        outs_s.append(st_s)

    stack = lambda outs, f: jnp.stack([f(o) for o in outs])
    res = [xp.reshape(np_, tp, d), xs.reshape(ns, ts, d)]
    for idx in range(3):
        res.append(stack(outs_p, lambda o: o[idx]))
        res.append(stack(outs_s, lambda o: o[idx]))
    for gi in range(len(DIL_GROUPS)):
        res.append(stack(outs_p, lambda o: o[3][gi]))
        res.append(stack(outs_s, lambda o: o[3][gi]))
    return tuple(res)
```

```python
import functools
import math

import jax
import jax.numpy as jnp
from jax import lax
from jax.experimental import pallas as pl
from jax.experimental.pallas import tpu as pltpu

F32 = jnp.float32
BF16 = jnp.bfloat16

POOL_GC = 64
POOL_W = 256
POOL_WINDOWS = (2, 4, 8, 16)
POOL_HIST = 15
POOL_HALO = 16
ATT_HD = 64
ATT_W = 384
ATT_PAIR = 128
DIL_GROUPS = ((128, 1), (512, 4), (2048, 16))
BAND_BLK = 128
ROPE_THETA = 10000.0
RWKV_HD = 64
RWKV_W = 384
RWKV_PROJ = 1408
RWKV_LN_EPS = 64e-5
RMS_EPS = 1e-6
PAST_LEN = 8192
NEG_BIG = -1e30

LANES = 128
SUBLANES = 8
VMEM_LIMIT = 56 * 1024 * 1024

NN = (((1,), (0,)), ((), ()))
NT = (((1,), (1,)), ((), ()))
TN = (((0,), (0,)), ((), ()))


def _params(*sem):
    return pltpu.CompilerParams(dimension_semantics=sem, vmem_limit_bytes=VMEM_LIMIT)


def _const_spec(shape):
    nd = len(shape)
    return pl.BlockSpec(shape, lambda *_: (0,) * nd, pipeline_mode=pl.Buffered(1))


def _bdot(a, b, dims=NN):
    return lax.dot_general(a.astype(BF16), b.astype(BF16), dims, preferred_element_type=F32)


def _split(a):
    hi = a.astype(BF16)
    lo = (a - hi.astype(F32)).astype(BF16)
    return hi, lo


def _mm3(a, b, dims=NN):
    ah, al = _split(a)
    bh, bl = _split(b)
    f = lambda x, y: lax.dot_general(x, y, dims, preferred_element_type=F32)
    return f(ah, bh) + (f(ah, bl) + f(al, bh))


def _mm2_exact_rhs(a, b_bf16, dims=NN):
    ah, al = _split(a)
    f = lambda x: lax.dot_general(x, b_bf16, dims, preferred_element_type=F32)
    return f(ah) + f(al)


def _rms(x, g):
    return x * lax.rsqrt(jnp.mean(x * x, axis=-1, keepdims=True) + RMS_EPS) * g


def _sigmoid(x):
    return 1.0 / (1.0 + jnp.exp(-x))


def _in_proj_kernel(x_ref, g_ref, w_ref, cos_ref, sin_ref,
                    pool_ref, q_ref, k_ref, v_ref, rw_ref, gate_ref):
    h = _rms(x_ref[...], g_ref[...]).astype(BF16)

    def seg(a, b):
        return jnp.dot(h, w_ref[:, a:b], preferred_element_type=F32)

    cos = cos_ref[...]
    sin = sin_ref[...]
    lane = lax.broadcasted_iota(jnp.int32, cos.shape, 1)
    low_half = (lane & 32) == 0

    def rope(t):
        partner = jnp.where(low_half, pltpu.roll(t, 96, 1), pltpu.roll(t, 32, 1))
        return t * cos + partner * sin

    o = 0
    pool_ref[...] = seg(o, o + POOL_W)
    o += POOL_W
    qz = seg(o, o + ATT_W)
    o += ATT_W
    kz = seg(o, o + ATT_W)
    o += ATT_W
    for c in range(ATT_W // LANES):
        cs = slice(c * LANES, (c + 1) * LANES)
        q_ref[:, cs] = rope(qz[:, cs]) * (ATT_HD ** -0.5)
        k_ref[:, cs] = rope(kz[:, cs])
    v_ref[...] = seg(o, o + ATT_W)
    o += ATT_W
    rw_ref[...] = seg(o, o + RWKV_PROJ)
    o += RWKV_PROJ
    gate_ref[...] = seg(o, w_ref.shape[1])


def _in_proj(x2d, g, w_bf, cos, sin, tm):
    m, d = x2d.shape
    ncol = w_bf.shape[1]
    ngate = ncol - (POOL_W + 3 * ATT_W + RWKV_PROJ)
    ntab = cos.shape[0] // tm
    row = lambda w: pl.BlockSpec((tm, w), lambda i: (i, 0))
    tab = pl.BlockSpec((tm, LANES), lambda i: (i % ntab, 0))
    widths = (POOL_W, ATT_W, ATT_W, ATT_W, RWKV_PROJ, ngate)
    return pl.pallas_call(
        _in_proj_kernel,
        grid=(m // tm,),
        in_specs=[row(d), _const_spec((1, d)), _const_spec((d, ncol)), tab, tab],
        out_specs=[row(w) for w in widths],
        out_shape=[jax.ShapeDtypeStruct((m, w), F32) for w in widths],
        compiler_params=_params("parallel"),
        name="in_proj",
    )(x2d, g, w_bf, cos, sin)


def _pool_kernel(prev_ref, cur_ref, w_ref, scale_ref, y_ref, ext_ref, *, pos0, zero_first_prev):
    bn, tm, w = cur_ref.shape
    j = pl.program_id(1)
    prev = prev_ref[...]
    if zero_first_prev:
        prev = jnp.where(j == 0, 0.0, prev)
    cur = cur_ref[...]
    ext_ref[:, :POOL_HALO, :] = prev
    ext_ref[:, POOL_HALO:, :] = cur

    lane = lax.broadcasted_iota(jnp.int32, (bn, tm, w), 2)
    group = jnp.right_shift(lane, POOL_GC.bit_length() - 1)
    acc = cur
    win = jnp.zeros_like(cur)
    for s in range(1, POOL_WINDOWS[-1] + 1):
        if s in POOL_WINDOWS:
            win = jnp.where(group == POOL_WINDOWS.index(s), acc, win)
        if s < POOL_WINDOWS[-1]:
            acc = acc + ext_ref[:, POOL_HALO - s:POOL_HALO - s + tm, :]
    width = jnp.left_shift(2, group)
    pos = pos0 + j * tm + lax.broadcasted_iota(jnp.int32, (bn, tm, w), 1)
    cnt = jnp.minimum(pos + 1, width).astype(F32)
    dlt = (win / cnt - cur).reshape(bn * tm, w)
    y = _bdot(dlt, w_ref[...]) * scale_ref[...]
    y_ref[...] = y.reshape(bn, tm, w)


def _pool(prev_arr, cur3, w_bd, scale, *, bn, tm, pos0, prev_is_self):
    n, t, w = cur3.shape
    if prev_is_self:
        per = tm // POOL_HALO
        prev_spec = pl.BlockSpec((bn, POOL_HALO, w), lambda b, j: (b, jnp.maximum(j * per - 1, 0), 0))
    else:
        prev_spec = pl.BlockSpec((bn, POOL_HALO, w), lambda b, j: (b, 0, 0))
    return pl.pallas_call(
        functools.partial(_pool_kernel, pos0=pos0, zero_first_prev=prev_is_self),
        grid=(n // bn, t // tm),
        in_specs=[prev_spec,
                  pl.BlockSpec((bn, tm, w), lambda b, j: (b, j, 0)),
                  _const_spec((w, w)), _const_spec((1, w))],
        out_specs=pl.BlockSpec((bn, tm, w), lambda b, j: (b, j, 0)),
        out_shape=jax.ShapeDtypeStruct((n, t, w), F32),
        scratch_shapes=[pltpu.VMEM((bn, tm + POOL_HALO, w), F32)],
        compiler_params=_params("parallel", "parallel"),
        name="pool_mix",
    )(prev_arr, cur3, w_bd, scale)


def _softmax_pair(q, k, v, valid):
    m_rows = q.shape[0]
    lane = lax.broadcasted_iota(jnp.int32, (m_rows, ATT_PAIR), 1)
    head0 = lane < ATT_HD
    kb = k.astype(BF16)
    vb = v.astype(BF16)
    outs, lses = [], []
    for hs in range(2):
        hm = head0 if hs == 0 else jnp.logical_not(head0)
        qm = jnp.where(hm, q, 0.0).astype(BF16)
        s = lax.dot_general(qm, kb, NT, preferred_element_type=F32)
        s = jnp.where(valid, s, NEG_BIG)
        mx = jnp.max(s, axis=-1, keepdims=True)
        e = jnp.exp(s - mx)
        den = jnp.sum(e, axis=-1, keepdims=True)
        o = jnp.dot(e.astype(BF16), vb, preferred_element_type=F32) / den
        outs.append(o)
        lses.append(jnp.broadcast_to(mx + jnp.log(den), o.shape))
    return jnp.where(head0, outs[0], outs[1]), jnp.where(head0, lses[0], lses[1])


def _merge_groups(outs, lses):
    mx = functools.reduce(jnp.maximum, lses)
    es = [jnp.exp(l - mx) for l in lses]
    num = functools.reduce(lambda a, b: a + b, [e * o for e, o in zip(es, outs)])
    return num / functools.reduce(lambda a, b: a + b, es)


def _rows(start, size, stride):
    return pl.ds(start, size, stride=stride) if stride > 1 else pl.ds(start, size)


def _dil_attn_kernel(*refs):
    ng = len(DIL_GROUPS)
    y_ref, o_scr, l_scr = refs[5 * ng:]
    j = pl.program_id(1)
    tq = y_ref.shape[1]
    qi = lax.broadcasted_iota(jnp.int32, (BAND_BLK, 2 * BAND_BLK), 0)
    kj = lax.broadcasted_iota(jnp.int32, (BAND_BLK, 2 * BAND_BLK), 1)
    dist = BAND_BLK + qi - kj
    has_prev = kj + jnp.where(j > 0, BAND_BLK, 0) >= BAND_BLK
    for gi, (win, dil) in enumerate(DIL_GROUPS):
        band = (dist >= 0) & (dist <= win // dil)
        step = BAND_BLK * dil
        q_ref, k_ref, v_ref, kp_ref, vp_ref = refs[5 * gi:5 * gi + 5]
        for r in range(dil):
            for sb in range(tq // step):
                rows = _rows(r + sb * step, BAND_BLK, dil)
                if sb == 0:
                    prow = _rows(r, BAND_BLK, dil)
                    k_prev, v_prev = kp_ref[0, prow, :], vp_ref[0, prow, :]
                    valid = band & has_prev
                else:
                    prow = _rows(r + (sb - 1) * step, BAND_BLK, dil)
                    k_prev, v_prev = k_ref[0, prow, :], v_ref[0, prow, :]
                    valid = band
                o, lse = _softmax_pair(q_ref[0, rows, :],
                                       jnp.concatenate([k_prev, k_ref[0, rows, :]], axis=0),
                                       jnp.concatenate([v_prev, v_ref[0, rows, :]], axis=0), valid)
                o_scr[gi, rows, :] = o
                l_scr[gi, rows, :] = lse
    y_ref[0] = _merge_groups([o_scr[g] for g in range(ng)], [l_scr[g] for g in range(ng)])


def _dil_attn(q3, k3, v3):
    n, t, w = q3.shape
    tq = BAND_BLK * max(d for _, d in DIL_GROUPS)
    assert t % tq == 0
    specs, args = [], []
    for gi, (_, dil) in enumerate(DIL_GROUPS):
        per = tq // (BAND_BLK * dil)
        cur = pl.BlockSpec((1, tq, LANES), lambda b, j, gi=gi: (b, j, gi))
        prev = pl.BlockSpec((1, BAND_BLK * dil, LANES),
                            lambda b, j, per=per, gi=gi: (b, jnp.maximum(j * per - 1, 0), gi))
        specs += [cur, cur, cur, prev, prev]
        args += [q3, k3, v3, k3, v3]
    ng = len(DIL_GROUPS)
    return pl.pallas_call(
        _dil_attn_kernel,
        grid=(n, t // tq),
        in_specs=specs,
        out_specs=pl.BlockSpec((1, tq, LANES), lambda b, j: (b, j, 0)),
        out_shape=jax.ShapeDtypeStruct((n, t, LANES), F32),
        scratch_shapes=[pltpu.VMEM((ng, tq, LANES), F32), pltpu.VMEM((ng, tq, LANES), F32)],
        compiler_params=_params("parallel", "parallel"),
        name="dil_attn",
    )(*args)


def _sample_attn_kernel(q_ref, k_ref, v_ref, *rest):
    ng = len(DIL_GROUPS)
    cache_refs, y_ref, new_refs = rest[:ng], rest[2 * ng], rest[2 * ng + 1:]
    t = q_ref.shape[1]
    mq = 2 * SUBLANES
    zpad = jnp.zeros((mq - t, ATT_PAIR), F32)
    lane = lax.broadcasted_iota(jnp.int32, (mq, ATT_PAIR), 1)
    head0 = lane < ATT_HD
    tn = lax.broadcasted_iota(jnp.int32, (mq, mq), 0)
    jn = lax.broadcasted_iota(jnp.int32, (mq, mq), 1)
    outs, lses = [], []
    for gi, (win, dil) in enumerate(DIL_GROUPS):
        reach = dil * (win // dil)
        cols = slice(gi * ATT_PAIR, (gi + 1) * ATT_PAIR)
        ct = cache_refs[gi][0, 0]
        hist = ct.shape[1]
        kt2 = ct[:ATT_PAIR].astype(BF16)
        vt2 = ct[ATT_PAIR:].astype(BF16)
        k_new, v_new = k_ref[0, :, cols], v_ref[0, :, cols]
        q = jnp.concatenate([q_ref[0, :, cols], zpad], axis=0)
        kn = jnp.concatenate([k_new, zpad], axis=0).astype(BF16)
        vn = jnp.concatenate([v_new, zpad], axis=0).astype(BF16)
        ti = lax.broadcasted_iota(jnp.int32, (mq, hist), 0)
        pj = lax.broadcasted_iota(jnp.int32, (mq, hist), 1)
        d_old = hist + ti - pj
        ok_old = (d_old <= reach) & ((d_old & (dil - 1)) == 0)
        d_new = tn - jn
        ok_new = (d_new >= 0) & (d_new <= reach) & ((d_new & (dil - 1)) == 0) & (jn < t)
        o_h, l_h = [], []
        for hs in range(2):
            hm = head0 if hs == 0 else jnp.logical_not(head0)
            qm = jnp.where(hm, q, 0.0).astype(BF16)
            s_old = jnp.where(ok_old, jnp.dot(qm, kt2, preferred_element_type=F32), NEG_BIG)
            s_new = jnp.where(ok_new, lax.dot_general(qm, kn, NT, preferred_element_type=F32), NEG_BIG)
            mx = jnp.maximum(jnp.max(s_old, axis=-1, keepdims=True), jnp.max(s_new, axis=-1, keepdims=True))
            e_old = jnp.exp(s_old - mx)
            e_new = jnp.exp(s_new - mx)
            den = jnp.sum(e_old, axis=-1, keepdims=True) + jnp.sum(e_new, axis=-1, keepdims=True)
            num = (lax.dot_general(e_old.astype(BF16), vt2, NT, preferred_element_type=F32)
                   + jnp.dot(e_new.astype(BF16), vn, preferred_element_type=F32))
            o_h.append(num / den)
            l_h.append(jnp.broadcast_to(mx + jnp.log(den), num.shape))
        outs.append(jnp.where(head0, o_h[0], o_h[1])[:t])
        lses.append(jnp.where(head0, l_h[0], l_h[1])[:t])

        shifted = pltpu.roll(ct, hist - t, 1)
        fresh = jnp.concatenate([jnp.zeros((LANES - t, 2 * ATT_PAIR), F32),
                                 jnp.concatenate([k_new, v_new], axis=1)], axis=0).T
        lane_c = lax.broadcasted_iota(jnp.int32, (2 * ATT_PAIR, LANES), 1)
        if hist > LANES:
            new_refs[gi][0, 0, :, :hist - LANES] = shifted[:, :hist - LANES]
        new_refs[gi][0, 0, :, hist - LANES:] = jnp.where(lane_c >= LANES - t, fresh, shifted[:, hist - LANES:])
    y_ref[0] = _merge_groups(outs, lses)


def _sample_attn(q3, k3, v3, caches_t, accs, layer):
    n, t, w = q3.shape
    ng = len(DIL_GROUPS)
    for (win, dil), c in zip(DIL_GROUPS, caches_t):
        assert c.shape[3] == win and win % LANES == 0 and dil & (dil - 1) == 0 and t <= SUBLANES
    new = pl.BlockSpec((1, t, w), lambda b: (b, 0, 0))
    blk = lambda c: pl.BlockSpec((1, 1) + c.shape[2:], lambda b: (layer, b, 0, 0))
    res = pl.pallas_call(
        _sample_attn_kernel,
        grid=(n,),
        in_specs=[new, new, new] + [blk(c) for c in caches_t]
        + [pl.BlockSpec(memory_space=pl.ANY)] * ng,
        out_specs=[pl.BlockSpec((1, t, LANES), lambda b: (b, 0, 0))] + [blk(c) for c in accs],
        out_shape=[jax.ShapeDtypeStruct((n, t, LANES), F32)]
        + [jax.ShapeDtypeStruct(a.shape, F32) for a in accs],
        input_output_aliases={3 + ng + g: 1 + g for g in range(ng)},
        compiler_params=_params("parallel"),
        name="sample_attn",
    )(q3, k3, v3, *caches_t, *accs)
    return res[0], list(res[1:])


def _rwkv_prep_kernel(prev8_ref, sp_ref, cur_ref, mu_ref, wwa_ref, w0_ref, a0_ref, g2_ref,
                      kk_ref, ka_ref, rk_ref, ones_ref, tril_ref,
                      at_ref, bt_ref, kt_ref, rt_ref, v_ref, pl_ref, g_ref, bonus_ref,
                      sh_ref, p_ref, *, chunk):
    bn, tm, w = cur_ref.shape
    j = pl.program_id(1)
    cur = cur_ref[...]
    prev_row = jnp.where(j == 0, sp_ref[...], prev8_ref[:, SUBLANES - 1:SUBLANES, :])
    sh_ref[:, SUBLANES - 1:SUBLANES, :] = prev_row
    sh_ref[:, SUBLANES:, :] = cur
    prev = sh_ref[:, SUBLANES - 1:SUBLANES - 1 + tm, :]
    m = bn * tm
    zs = (cur + (prev - cur) * mu_ref[...]).reshape(m, w)

    r = zs[:, 0:RWKV_W]
    k = zs[:, RWKV_W:2 * RWKV_W]
    v = zs[:, 2 * RWKV_W:3 * RWKV_W]
    lo = 3 * RWKV_W
    z_wa = zs[:, lo:lo + LANES]
    z_g = zs[:, lo + LANES:lo + 2 * LANES]
    lane = lax.broadcasted_iota(jnp.int32, z_wa.shape, 1)
    u = jnp.where(lane < LANES // 2, jnp.tanh(z_wa), z_wa)
    lora = _bdot(u, wwa_ref[...])
    xw = w0_ref[...] + lora[:, :RWKV_W]
    w_log = -(jnp.maximum(-xw, 0.0) + jnp.log(1.0 + jnp.exp(-jnp.abs(xw)))) - 0.5
    e = jnp.exp(w_log)
    a = _sigmoid(a0_ref[...] + lora[:, RWKV_W:])
    g_ref[...] = _bdot(_sigmoid(z_g), g2_ref[...])

    ones_bd = ones_ref[...]
    kk = k * kk_ref[...]
    kk = kk * lax.rsqrt(jnp.maximum(_mm2_exact_rhs(kk * kk, ones_bd), 1e-24))
    k = k * (1.0 + (a - 1.0) * ka_ref[...])
    bonus_ref[...] = _mm2_exact_rhs(r * k * rk_ref[...], ones_bd) * v

    tril = tril_ref[...]
    e1 = e.astype(BF16)
    rem = e - e1.astype(F32)
    e2 = rem.astype(BF16)
    e3 = (rem - e2.astype(F32)).astype(BF16)
    f = lambda x: jnp.dot(tril, x, preferred_element_type=F32)
    cum = f(e1) + (f(e2) + f(e3))
    p_inc = jnp.exp(-cum)
    p_inv = jnp.exp(cum)
    at_ref[...] = -kk * jnp.exp(e - cum)
    bt_ref[...] = kk * a * p_inv
    kt_ref[...] = k * p_inv
    rt_ref[...] = r * p_inc
    v_ref[...] = v
    p_ref[...] = p_inc
    for c in range(m // chunk):
        pl_ref[c] = p_ref[(c + 1) * chunk - 1:(c + 1) * chunk, :]


def _rwkv_prep(zr3, shift_prev, lw, *, bn, tm, chunk):
    n, t, w = zr3.shape
    m = bn * tm
    per = tm // SUBLANES
    rowblk = pl.BlockSpec((m, RWKV_W), lambda b, j: (b * (t // tm) + j, 0))
    tril = (jnp.arange(m)[:, None] >= jnp.arange(m)[None, :]) & \
           (jnp.arange(m)[:, None] // chunk == jnp.arange(m)[None, :] // chunk)
    nch = m // chunk
    consts = [lw['rwkv_mu'], lw['rwkv_wwa'], lw['rwkv_w0'], lw['rwkv_a0'], lw['rwkv_g2'],
              lw['rwkv_k_k'], lw['rwkv_k_a'], lw['rwkv_r_k'], lw['ones_bd'], tril.astype(BF16)]
    outs = pl.pallas_call(
        functools.partial(_rwkv_prep_kernel, chunk=chunk),
        grid=(n // bn, t // tm),
        in_specs=[pl.BlockSpec((bn, SUBLANES, w), lambda b, j: (b, jnp.maximum(j * per - 1, 0), 0)),
                  pl.BlockSpec((bn, 1, w), lambda b, j: (b, 0, 0)),
                  pl.BlockSpec((bn, tm, w), lambda b, j: (b, j, 0))]
        + [_const_spec(c.shape) for c in consts],
        out_specs=[rowblk] * 5
        + [pl.BlockSpec((nch, 1, RWKV_W), lambda b, j: (b * (t // tm) + j, 0, 0)), rowblk, rowblk],
        out_shape=[jax.ShapeDtypeStruct((n * t, RWKV_W), F32)] * 5
        + [jax.ShapeDtypeStruct((n * t // chunk, 1, RWKV_W), F32)]
        + [jax.ShapeDtypeStruct((n * t, RWKV_W), F32)] * 2,
        scratch_shapes=[pltpu.VMEM((bn, tm + SUBLANES, w), F32), pltpu.VMEM((m, RWKV_W), F32)],
        compiler_params=_params("parallel", "parallel"),
        name="rwkv_prep",
    )(zr3, shift_prev, zr3, *consts)
    return outs


def _rwkv_scan_kernel(at_ref, bt_ref, kt_ref, rt_ref, v_ref, pl_ref, s0_ref, y_ref, sT_ref, s_ref):
    c = pl.program_id(1)
    nb, chunk, _ = at_ref.shape
    npair = RWKV_W // LANES
    rows = 2 * chunk

    @pl.when(c == 0)
    def _():
        s_ref[...] = s0_ref[...]

    lane = lax.broadcasted_iota(jnp.int32, (chunk, LANES), 1)
    head0 = lane < RWKV_HD
    ri = lax.broadcasted_iota(jnp.int32, (rows, rows), 0)
    ci = lax.broadcasted_iota(jnp.int32, (rows, rows), 1)
    same = (ri < chunk) == (ci < chunk)
    strict = same & (ci < ri)
    incl = same & (ci <= ri)
    eye = jnp.where(ri == ci, 1.0, 0.0).astype(F32)

    def stack(x):
        return jnp.concatenate([jnp.where(head0, x, 0.0), jnp.where(head0, 0.0, x)], axis=0)

    def mm(a, b, dims=NN):
        (ah, al), (bh, bl) = a, b
        lhs = jnp.concatenate([ah, ah, al], axis=0 if dims == TN else 1)
        rhs = jnp.concatenate([bh, bl, bh], axis=1 if dims == NT else 0)
        return lax.dot_general(lhs, rhs, dims, preferred_element_type=F32)

    jobs = [(sq, p, slice(p * LANES, (p + 1) * LANES)) for sq in range(nb) for p in range(npair)]
    ast = [stack(at_ref[sq, :, cs]) for sq, p, cs in jobs]
    rst = [stack(rt_ref[sq, :, cs]) for sq, p, cs in jobs]
    vst = [stack(v_ref[sq, :, cs]) for sq, p, cs in jobs]
    gram = []
    for i, (sq, p, cs) in enumerate(jobs):
        bt, kt = bt_ref[sq, :, cs], kt_ref[sq, :, cs]
        gram.append(mm(_split(jnp.concatenate([ast[i], rst[i]], axis=0)),
                       _split(jnp.concatenate([bt, bt, kt, kt], axis=0)), NT))
    m_ab = [jnp.where(strict, g[:rows, :rows], 0.0) for g in gram]
    m_ak = [jnp.where(strict, g[:rows, rows:], 0.0) for g in gram]
    m_r = [jnp.concatenate([jnp.where(incl, g[rows:, :rows], 0.0),
                            jnp.where(incl, g[rows:, rows:], 0.0)], axis=1) for g in gram]
    assert len(jobs) % 2 == 0
    zero_bf = jnp.zeros((rows, rows), BF16)

    def pair_dot(n_i, n_j, x_i, x_j):
        rhs = jnp.concatenate([jnp.concatenate([x_i, zero_bf], axis=1),
                               jnp.concatenate([zero_bf, x_j], axis=1)], axis=0)
        out = jnp.dot(jnp.concatenate([n_i, n_j], axis=1), rhs, preferred_element_type=F32)
        return out[:, :rows], out[:, rows:]

    npow = [m.astype(BF16) for m in m_ab]
    inv = [eye + m for m in m_ab]
    for _ in range(int(math.log2(chunk)) - 1):
        for i in range(0, len(jobs), 2):
            sq_i, sq_j = pair_dot(npow[i], npow[i + 1], npow[i], npow[i + 1])
            npow[i], npow[i + 1] = sq_i.astype(BF16), sq_j.astype(BF16)
            up_i, up_j = pair_dot(npow[i], npow[i + 1], inv[i].astype(BF16), inv[i + 1].astype(BF16))
            inv[i], inv[i + 1] = inv[i] + up_i, inv[i + 1] + up_j
    mv = [mm(_split(m), _split(x)) for m, x in zip(m_ak, vst)]
    wu = [mm(_split(t), _split(jnp.concatenate([x, a], axis=1))) for t, x, a in zip(inv, mv, ast)]
    z = [_split(jnp.concatenate([jnp.concatenate([w[:, LANES:], w[:, :LANES]], axis=1),
                                 jnp.concatenate([jnp.zeros_like(x), x], axis=1)], axis=0))
         for w, x in zip(wu, vst)]
    qy = [mm(_split(m), zz) for m, zz in zip(m_r, z)]
    ad = []
    for i, (sq, p, cs) in enumerate(jobs):
        decay = pl_ref[sq, 0, :, cs]
        bk = jnp.concatenate([stack(bt_ref[sq, :, cs]), stack(kt_ref[sq, :, cs])], axis=0) * decay
        ad.append(mm(_split(bk), z[i], TN))
    for i, (sq, p, cs) in enumerate(jobs):
        a_t = eye * pl_ref[sq, 0, :, cs] + ad[i][:, :LANES]
        st = _split(s_ref[sq, p])
        yst = mm(_split(rst[i] + qy[i][:, :LANES]), st) + qy[i][:, LANES:]
        y_ref[sq, :, cs] = yst[:chunk] + yst[chunk:]
        s_ref[sq, p] = mm(_split(a_t), st) + ad[i][:, LANES:]

    @pl.when(c == pl.num_programs(1) - 1)
    def _():
        sT_ref[...] = s_ref[...]


def _rwkv_scan(at, bt, kt, rt, v, plast, s0, *, n, t, chunk, nb):
    assert 2 * chunk == LANES
    npair = RWKV_W // LANES
    nch = t // chunk
    tok = pl.BlockSpec((nb, chunk, RWKV_W), lambda b, c: (b, c, 0))
    st = pl.BlockSpec((nb, npair, LANES, LANES), lambda b, c: (b, 0, 0, 0))
    v3 = lambda a: a.reshape(n, t, RWKV_W)
    y, s_t = pl.pallas_call(
        _rwkv_scan_kernel,
        grid=(n // nb, nch),
        in_specs=[tok] * 5 + [pl.BlockSpec((nb, 1, 1, RWKV_W), lambda b, c: (b, c, 0, 0)), st],
        out_specs=[tok, st],
        out_shape=[jax.ShapeDtypeStruct((n, t, RWKV_W), F32),
                   jax.ShapeDtypeStruct((n, npair, LANES, LANES), F32)],
        scratch_shapes=[pltpu.VMEM((nb, npair, LANES, LANES), F32)],
        compiler_params=_params("parallel", "arbitrary"),
        name="rwkv_scan",
    )(v3(at), v3(bt), v3(kt), v3(rt), v3(v), plast.reshape(n, nch, 1, RWKV_W), s0)
    return y, s_t


def _merge_kernel(x_ref, yp_ref, ya_ref, ys_ref, bonus_ref, g_ref, gate_ref,
                  pp_ref, pa_ref, pr_ref, wo_ref, lng_ref, lnb_ref, mean_ref, out_ref):
    d = x_ref.shape[1]
    y_attn = ya_ref[...]

    ys = ys_ref[...]
    mean_bd = mean_ref[...]
    mu = _mm2_exact_rhs(ys, mean_bd)
    dev = ys - mu
    var = _mm2_exact_rhs(dev * dev, mean_bd)
    yn = dev * lax.rsqrt(var + RWKV_LN_EPS) * lng_ref[...] + lnb_ref[...]
    y_rwkv = (yn + bonus_ref[...]) * g_ref[...]

    merged = (_sigmoid(gate_ref[:, 0:d]) * _bdot(yp_ref[...], pp_ref[...])
              + _sigmoid(gate_ref[:, d:2 * d]) * _bdot(y_attn, pa_ref[...])
              + _sigmoid(gate_ref[:, 2 * d:3 * d]) * _bdot(y_rwkv, pr_ref[...]))
    out_ref[...] = x_ref[...] + _bdot(merged, wo_ref[...])


def _merge(x2d, yp, ya, ys, bonus, g, gates, lw, tm):
    m, d = x2d.shape
    row = lambda w: pl.BlockSpec((tm, w), lambda i: (i, 0))
    consts = [lw['proj_pool'], lw['proj_attn'], lw['proj_rwkv'], lw['w_out'],
              lw['rwkv_ln_g'], lw['rwkv_ln_b'], lw['mean_bd']]
    return pl.pallas_call(
        _merge_kernel,
        grid=(m // tm,),
        in_specs=[row(d), row(POOL_W), row(LANES)] + [row(RWKV_W)] * 3 + [row(3 * d)]
        + [_const_spec(c.shape) for c in consts],
        out_specs=row(d),
        out_shape=jax.ShapeDtypeStruct((m, d), F32),
        compiler_params=_params("parallel"),
        name="merge_out_proj",
    )(x2d, yp, ya, ys, bonus, g, gates, *consts)


def _ffn_kernel(x_ref, g_ref, w1_ref, w3_ref, w2_ref, out_ref):
    x = x_ref[...]
    h = _rms(x, g_ref[...]).astype(BF16)
    h1 = jnp.dot(h, w1_ref[...], preferred_element_type=F32)
    h3 = jnp.dot(h, w3_ref[...], preferred_element_type=F32)
    act = h1 * _sigmoid(h1) * h3
    out_ref[...] = x + _bdot(act, w2_ref[...])


def _ffn(x2d, lw, tm):
    m, d = x2d.shape
    row = pl.BlockSpec((tm, d), lambda i: (i, 0))
    consts = [lw['norm_ffn_g'], lw['ffn_w1'], lw['ffn_w3'], lw['ffn_w2']]
    return pl.pallas_call(
        _ffn_kernel,
        grid=(m // tm,),
        in_specs=[row] + [_const_spec(c.shape) for c in consts],
        out_specs=row,
        out_shape=jax.ShapeDtypeStruct((m, d), F32),
        compiler_params=_params("parallel"),
        name="swiglu",
    )(x2d, *consts)


def _ple_kernel(x_ref, p_ref, g_ref, wg_ref, wp_ref, gf_ref, out_ref, *, final_norm):
    x = x_ref[...]
    gate = _sigmoid(_bdot(_rms(x, g_ref[...]), wg_ref[...]))
    y = x + _bdot(p_ref[...], wp_ref[...]) * gate
    if final_norm:
        y = _rms(y, gf_ref[...])
    out_ref[...] = y


def _ple(x2d, p2d, lw, g_final, tm, final_norm):
    m, d = x2d.shape
    row = lambda w: pl.BlockSpec((tm, w), lambda i: (i, 0))
    consts = [lw['norm_ple_g'], lw['ple_gate'], lw['ple_proj'], g_final]
    return pl.pallas_call(
        functools.partial(_ple_kernel, final_norm=final_norm),
        grid=(m // tm,),
        in_specs=[row(d), row(p2d.shape[1])] + [_const_spec(c.shape) for c in consts],
        out_specs=row(d),
        out_shape=jax.ShapeDtypeStruct((m, d), F32),
        compiler_params=_params("parallel"),
        name="ple_final" if final_norm else "ple",
    )(x2d, p2d, *consts)


def _rope_tables(pos, reps):
    half = ATT_HD // 2
    inv = ROPE_THETA ** (-2.0 * jnp.arange(half, dtype=F32) / ATT_HD)
    ang = pos.astype(F32)[:, None] * inv[None, :]
    cos = jnp.cos(ang)
    sin = jnp.sin(ang)
    cos = jnp.tile(jnp.concatenate([cos, cos], axis=1), (reps, LANES // ATT_HD))
    sin = jnp.tile(jnp.concatenate([-sin, sin], axis=1), (reps, LANES // ATT_HD))
    return cos, sin


def _block_diag(blocks):
    g, a, b = blocks.shape
    eye = jnp.eye(g, dtype=blocks.dtype)
    return (eye[:, None, :, None] * blocks[:, :, None, :]).reshape(g * a, g * b)


def _layer_weights(i, w):
    row = lambda a: a[i][None, :]
    bf = lambda a: a[i].astype(BF16)
    heads = RWKV_W // RWKV_HD
    zero = jnp.zeros_like(w['rwkv_w2'][i])
    wwa = jnp.concatenate([jnp.concatenate([w['rwkv_w2'][i], zero], axis=1),
                           jnp.concatenate([zero, w['rwkv_a2'][i]], axis=1)], axis=0)
    ones = jnp.ones((heads, RWKV_HD, RWKV_HD), F32)
    return {
        'norm_mix_g': row(w['norm_mix_g']), 'w_in': bf(w['w_in']),
        'pool_w_bd': _block_diag(w['pool_w_grp'][i]).astype(BF16), 'pool_scale': row(w['pool_scale']),
        'rwkv_mu': row(w['rwkv_mu']), 'rwkv_wwa': wwa.astype(BF16),
        'rwkv_w0': row(w['rwkv_w0']), 'rwkv_a0': row(w['rwkv_a0']), 'rwkv_g2': bf(w['rwkv_g2']),
        'rwkv_k_k': row(w['rwkv_k_k']), 'rwkv_k_a': row(w['rwkv_k_a']),
        'rwkv_r_k': w['rwkv_r_k'][i].reshape(1, RWKV_W),
        'rwkv_ln_g': row(w['rwkv_ln_g']), 'rwkv_ln_b': row(w['rwkv_ln_b']),
        'ones_bd': _block_diag(ones).astype(BF16),
        'mean_bd': _block_diag(ones / RWKV_HD).astype(BF16),
        'proj_pool': bf(w['proj_pool']), 'proj_attn': bf(w['proj_attn']),
        'proj_rwkv': bf(w['proj_rwkv']), 'w_out': bf(w['w_out']),
        'norm_ffn_g': row(w['norm_ffn_g']), 'ffn_w1': bf(w['ffn_w1']), 'ffn_w3': bf(w['ffn_w3']),
        'ffn_w2': bf(w['ffn_w2']), 'norm_ple_g': row(w['norm_ple_g']),
        'ple_proj': bf(w['ple_proj']), 'ple_gate': bf(w['ple_gate']),
    }


def _pair_states(s):
    n, h, a, b = s.shape
    s = jnp.swapaxes(s, 2, 3).reshape(n * (h // 2), 2, b, a)
    return jax.vmap(_block_diag)(s).reshape(n, h // 2, 2 * b, 2 * a)


def _unpair_states(s2):
    n, p, a, b = s2.shape
    s2 = s2.reshape(n, p, 2, a // 2, 2, b // 2)
    s = jnp.stack([s2[:, :, 0, :, 0, :], s2[:, :, 1, :, 1, :]], axis=2).reshape(n, 2 * p, a // 2, b // 2)
    return jnp.swapaxes(s, 2, 3)


def _decoder_layer(x2d, n, t, p2d, lw, g_final, final_norm, tabs, state, *, prompt, tm, chunk):
    d = x2d.shape[1]
    zp, q, k, v, zr, gates = _in_proj(x2d, lw['norm_mix_g'], lw['w_in'], tabs[0], tabs[1], tm)
    three = lambda a: a.reshape(n, t, a.shape[-1])
    zp3, q3, k3, v3, zr3 = three(zp), three(q), three(k), three(v), three(zr)

    if prompt:
        yp = _pool(zp3, zp3, lw['pool_w_bd'], lw['pool_scale'], bn=1, tm=tm, pos0=0, prev_is_self=True)
        new_pool = zp3[:, t - POOL_HIST:]
    else:
        hist = jnp.pad(state['pool'], ((0, 0), (POOL_HALO - POOL_HIST, 0), (0, 0)))
        yp = _pool(hist, zp3, lw['pool_w_bd'], lw['pool_scale'], bn=n, tm=t,
                   pos0=state['pos0'], prev_is_self=False)
        new_pool = jnp.concatenate([state['pool'], zp3], axis=1)[:, -POOL_HIST:]

    if prompt:
        ya = _dil_attn(q3, k3, v3)
        new_kv = []
        for gi, (win, dil) in enumerate(DIL_GROUPS):
            cols = slice(gi * ATT_PAIR, (gi + 1) * ATT_PAIR)
            keep = min(win, t)
            kv = jnp.concatenate([k3[:, t - keep:, cols], v3[:, t - keep:, cols]], axis=-1)
            new_kv.append(kv.reshape(n, keep, 2, 2, ATT_HD))
    else:
        ya, new_kv = _sample_attn(q3, k3, v3, state['kv_t'], state['kv_acc'], state['layer'])

    if prompt:
        at, bt, kt, rt, vv, plast, g, bonus = _rwkv_prep(zr3, state['shift'][:, None, :], lw,
                                                         bn=1, tm=tm, chunk=chunk)
        ys, s_t = _rwkv_scan(at, bt, kt, rt, vv, plast, state['wkv'], n=n, t=t, chunk=chunk, nb=n)
    else:
        at, bt, kt, rt, vv, plast, g, bonus = _rwkv_prep(zr3, state['shift'][:, None, :], lw,
                                                         bn=n, tm=t, chunk=t)
        padt = lambda a: jnp.pad(a.reshape(n, t, RWKV_W), ((0, 0), (0, chunk - t), (0, 0))).reshape(n * chunk, RWKV_W)
        ys, s_t = _rwkv_scan(padt(at), padt(bt), padt(kt), padt(rt), padt(vv), plast, state['wkv'],
                             n=n, t=chunk, chunk=chunk, nb=2)
        ys = ys[:, :t]
    new_shift = zr3[:, t - 1]

    x2d = _merge(x2d, yp.reshape(n * t, POOL_W), ya.reshape(n * t, LANES), ys.reshape(n * t, RWKV_W),
                 bonus, g, gates, lw, tm)
    x2d = _ffn(x2d, lw, tm)
    x2d = _ple(x2d, p2d, lw, g_final, tm, final_norm)
    return x2d, (new_pool, new_shift, _unpair_states(s_t), new_kv)


def kernel(x_prompt, x_sample, state_pool, state_shift, state_wkv, cache_kv_w128, cache_kv_w512,
           cache_kv_w2048, p_prompt, p_sample, norm_mix_g, w_in, pool_w_grp, pool_scale, rwkv_mu,
           rwkv_w0, rwkv_w2, rwkv_a0, rwkv_a2, rwkv_g2, rwkv_k_k, rwkv_k_a, rwkv_r_k, rwkv_ln_g,
           rwkv_ln_b, proj_pool, proj_attn, proj_rwkv, w_out, norm_ffn_g, ffn_w1, ffn_w3, ffn_w2,
           norm_ple_g, ple_proj, ple_gate, norm_final_g):
    weights = dict(norm_mix_g=norm_mix_g, w_in=w_in, pool_w_grp=pool_w_grp, pool_scale=pool_scale,
                   rwkv_mu=rwkv_mu, rwkv_w0=rwkv_w0, rwkv_w2=rwkv_w2, rwkv_a0=rwkv_a0, rwkv_a2=rwkv_a2,
                   rwkv_g2=rwkv_g2, rwkv_k_k=rwkv_k_k, rwkv_k_a=rwkv_k_a, rwkv_r_k=rwkv_r_k,
                   rwkv_ln_g=rwkv_ln_g, rwkv_ln_b=rwkv_ln_b, proj_pool=proj_pool, proj_attn=proj_attn,
                   proj_rwkv=proj_rwkv, w_out=w_out, norm_ffn_g=norm_ffn_g, ffn_w1=ffn_w1, ffn_w3=ffn_w3,
                   ffn_w2=ffn_w2, norm_ple_g=norm_ple_g, ple_proj=ple_proj, ple_gate=ple_gate)
    depth = w_in.shape[0]
    np_, tp, d = x_prompt.shape
    ns, ts, _ = x_sample.shape
    caches = (cache_kv_w128, cache_kv_w512, cache_kv_w2048)
    past_len = PAST_LEN
    tm_p = min(256, np_ * tp)
    tm_s = ns * ts
    chunk = 64
    tabs_p = _rope_tables(jnp.arange(tp), 1)
    tabs_s = _rope_tables(past_len + jnp.arange(ts), ns)
    g_final = norm_final_g[None, :]

    xp = x_prompt.reshape(np_ * tp, d)
    xs = x_sample.reshape(ns * ts, d)
    heads = RWKV_W // RWKV_HD
    zero_state = {
        'shift': jnp.zeros((np_, RWKV_PROJ), F32),
        'wkv': jnp.zeros((np_, heads // 2, 2 * RWKV_HD, 2 * RWKV_HD), F32),
    }
    caches_t = [jnp.transpose(c, (0, 1, 3, 4, 5, 2)).reshape(c.shape[0], c.shape[1], -1, c.shape[2])
                for c in caches]
    kv_acc = [jnp.zeros(c.shape, F32) for c in caches_t]
    outs_p, outs_s = [], []
    for i in range(depth):
        lw = _layer_weights(i, weights)
        last = i == depth - 1
        xp, st_p = _decoder_layer(xp, np_, tp, p_prompt[i].reshape(np_ * tp, -1), lw, g_final, last,
                                  tabs_p, zero_state, prompt=True, tm=tm_p, chunk=chunk)
        state_s = {'pool': state_pool[i], 'shift': state_shift[i], 'wkv': _pair_states(state_wkv[i]),
                   'kv_t': caches_t, 'kv_acc': kv_acc, 'layer': i, 'pos0': past_len}
        xs, st_s = _decoder_layer(xs, ns, ts, p_sample[i].reshape(ns * ts, -1), lw, g_final, last,
                                  tabs_s, state_s, prompt=False, tm=tm_s, chunk=chunk)
        kv_acc = st_s[3]
        outs_p.append(st_p)
        outs_s.append(st_s)

    stack = lambda outs, f: jnp.stack([f(o) for o in outs])
    res = [xp.reshape(np_, tp, d), xs.reshape(ns, ts, d)]
    for idx in range(3):
        res.append(stack(outs_p, lambda o: o[idx]))
        res.append(stack(outs_s, lambda o: o[idx]))
    for gi in range(len(DIL_GROUPS)):
        res.append(stack(outs_p, lambda o: o[3][gi]))
        a = kv_acc[gi]
        a = a.reshape(a.shape[0], a.shape[1], 2, 2, ATT_HD, a.shape[3])
        res.append(jnp.transpose(a, (0, 1, 5, 2, 3, 4)))
    return tuple(res)
```

```python
import functools
import math

import jax
import jax.numpy as jnp
from jax import lax
from jax.experimental import pallas as pl
from jax.experimental.pallas import tpu as pltpu

F32 = jnp.float32
BF16 = jnp.bfloat16

POOL_GC = 64
POOL_W = 256
POOL_WINDOWS = (2, 4, 8, 16)
POOL_HIST = 15
POOL_HALO = 16
ATT_HD = 64
ATT_W = 384
ATT_PAIR = 128
DIL_GROUPS = ((128, 1), (512, 4), (2048, 16))
BAND_BLK = 128
ROPE_THETA = 10000.0
RWKV_HD = 64
RWKV_W = 384
RWKV_PROJ = 1408
RWKV_LN_EPS = 64e-5
RMS_EPS = 1e-6
PAST_LEN = 8192
NEG_BIG = -1e30

LANES = 128
SUBLANES = 8
VMEM_LIMIT = 56 * 1024 * 1024

NN = (((1,), (0,)), ((), ()))
NT = (((1,), (1,)), ((), ()))
TN = (((0,), (0,)), ((), ()))


def _params(*sem):
    return pltpu.CompilerParams(dimension_semantics=sem, vmem_limit_bytes=VMEM_LIMIT)


def _wspec(arr, layer=None):
    if layer is None:
        nd = arr.ndim
        return pl.BlockSpec(arr.shape, lambda *_: (0,) * nd, pipeline_mode=pl.Buffered(1))
    nd = arr.ndim - 1
    return pl.BlockSpec((None,) + arr.shape[1:], lambda *_: (layer,) + (0,) * nd,
                        pipeline_mode=pl.Buffered(1))


def _bdot(a, b, dims=NN):
    return lax.dot_general(a.astype(BF16), b.astype(BF16), dims, preferred_element_type=F32)


def _split(a):
    hi = a.astype(BF16)
    lo = (a - hi.astype(F32)).astype(BF16)
    return hi, lo


def _mm3(a, b, dims=NN):
    ah, al = _split(a)
    bh, bl = _split(b)
    f = lambda x, y: lax.dot_general(x, y, dims, preferred_element_type=F32)
    return f(ah, bh) + (f(ah, bl) + f(al, bh))


def _mm2_exact_rhs(a, b_bf16, dims=NN):
    ah, al = _split(a)
    f = lambda x: lax.dot_general(x, b_bf16, dims, preferred_element_type=F32)
    return f(ah) + f(al)


def _rms(x, g):
    return x * lax.rsqrt(jnp.mean(x * x, axis=-1, keepdims=True) + RMS_EPS) * g


def _sigmoid(x):
    return 1.0 / (1.0 + jnp.exp(-x))


def _in_proj_kernel(x_ref, g_ref, w_ref, cos_ref, sin_ref,
                    pool_ref, q_ref, k_ref, v_ref, rw_ref, gate_ref):
    h = _rms(x_ref[...], g_ref[...]).astype(BF16)

    def seg(a, b):
        return jnp.dot(h, w_ref[:, a:b], preferred_element_type=F32)

    cos = cos_ref[...]
    sin = sin_ref[...]
    lane = lax.broadcasted_iota(jnp.int32, cos.shape, 1)
    low_half = (lane & 32) == 0

    def rope(t):
        partner = jnp.where(low_half, pltpu.roll(t, 96, 1), pltpu.roll(t, 32, 1))
        return t * cos + partner * sin

    o = 0
    pool_ref[...] = seg(o, o + POOL_W)
    o += POOL_W
    qz = seg(o, o + ATT_W)
    o += ATT_W
    kz = seg(o, o + ATT_W)
    o += ATT_W
    for c in range(ATT_W // LANES):
        cs = slice(c * LANES, (c + 1) * LANES)
        q_ref[:, cs] = rope(qz[:, cs]) * (ATT_HD ** -0.5)
        k_ref[:, cs] = rope(kz[:, cs])
    v_ref[...] = seg(o, o + ATT_W)
    o += ATT_W
    rw_ref[...] = seg(o, o + RWKV_PROJ)
    o += RWKV_PROJ
    gate_ref[...] = _sigmoid(seg(o, w_ref.shape[1])).astype(gate_ref.dtype)


def _in_proj(x2d, wts, layer, cos, sin, tm):
    m, d = x2d.shape
    g, w_bf = wts['norm_mix_g'], wts['w_in']
    ncol = w_bf.shape[-1]
    ngate = ncol - (POOL_W + 3 * ATT_W + RWKV_PROJ)
    ntab = cos.shape[0] // tm
    row = lambda w: pl.BlockSpec((tm, w), lambda i: (i, 0))
    tab = pl.BlockSpec((tm, LANES), lambda i: (i % ntab, 0))
    widths = (POOL_W, ATT_W, ATT_W, ATT_W, RWKV_PROJ, ngate)
    dtypes = (F32,) * 5 + (BF16,)
    return pl.pallas_call(
        _in_proj_kernel,
        grid=(m // tm,),
        in_specs=[row(d), _wspec(g, layer), _wspec(w_bf, layer), tab, tab],
        out_specs=[row(w) for w in widths],
        out_shape=[jax.ShapeDtypeStruct((m, w), dt) for w, dt in zip(widths, dtypes)],
        compiler_params=_params("parallel"),
        name="in_proj",
    )(x2d, g, w_bf, cos, sin)


def _pool_kernel(prev_ref, cur_ref, w_ref, scale_ref, y_ref, ext_ref, *, pos0, zero_first_prev):
    bn, tm, w = cur_ref.shape
    j = pl.program_id(1)
    prev = prev_ref[...]
    if zero_first_prev:
        prev = jnp.where(j == 0, 0.0, prev)
    cur = cur_ref[...]
    ext_ref[:, :POOL_HALO, :] = prev
    ext_ref[:, POOL_HALO:, :] = cur

    lane = lax.broadcasted_iota(jnp.int32, (bn, tm, w), 2)
    group = jnp.right_shift(lane, POOL_GC.bit_length() - 1)
    acc = cur
    win = jnp.zeros_like(cur)
    for s in range(1, POOL_WINDOWS[-1] + 1):
        if s in POOL_WINDOWS:
            win = jnp.where(group == POOL_WINDOWS.index(s), acc, win)
        if s < POOL_WINDOWS[-1]:
            acc = acc + ext_ref[:, POOL_HALO - s:POOL_HALO - s + tm, :]
    width = jnp.left_shift(2, group)
    pos = pos0 + j * tm + lax.broadcasted_iota(jnp.int32, (bn, tm, w), 1)
    cnt = jnp.minimum(pos + 1, width).astype(F32)
    dlt = (win / cnt - cur).reshape(bn * tm, w)
    y = _bdot(dlt, w_ref[...]) * scale_ref[...]
    y_ref[...] = y.reshape(bn, tm, w)


def _pool(hist, cur3, wts, layer, *, bn, tm, pos0):
    n, t, w = cur3.shape
    if hist is None:
        per = tm // POOL_HALO
        prev_arr = cur3
        prev_spec = pl.BlockSpec((bn, POOL_HALO, w), lambda b, j: (b, jnp.maximum(j * per - 1, 0), 0))
    else:
        prev_arr = hist
        prev_spec = pl.BlockSpec((None, bn, POOL_HALO, w), lambda b, j: (layer, b, 0, 0))
    w_bd, scale = wts['pool_w_bd'], wts['pool_scale']
    return pl.pallas_call(
        functools.partial(_pool_kernel, pos0=pos0, zero_first_prev=hist is None),
        grid=(n // bn, t // tm),
        in_specs=[prev_spec,
                  pl.BlockSpec((bn, tm, w), lambda b, j: (b, j, 0)),
                  _wspec(w_bd, layer), _wspec(scale, layer)],
        out_specs=pl.BlockSpec((bn, tm, w), lambda b, j: (b, j, 0)),
        out_shape=jax.ShapeDtypeStruct((n, t, w), F32),
        scratch_shapes=[pltpu.VMEM((bn, tm + POOL_HALO, w), F32)],
        compiler_params=_params("parallel", "parallel"),
        name="pool_mix",
    )(prev_arr, cur3, w_bd, scale)


def _softmax_pair(q, k, v, valid):
    m_rows = q.shape[0]
    lane = lax.broadcasted_iota(jnp.int32, (m_rows, ATT_PAIR), 1)
    head0 = lane < ATT_HD
    kb = k.astype(BF16)
    vb = v.astype(BF16)
    outs, lses = [], []
    for hs in range(2):
        hm = head0 if hs == 0 else jnp.logical_not(head0)
        qm = jnp.where(hm, q, 0.0).astype(BF16)
        s = lax.dot_general(qm, kb, NT, preferred_element_type=F32)
        s = jnp.where(valid, s, NEG_BIG)
        mx = jnp.max(s, axis=-1, keepdims=True)
        e = jnp.exp(s - mx)
        den = jnp.sum(e, axis=-1, keepdims=True)
        o = jnp.dot(e.astype(BF16), vb, preferred_element_type=F32) / den
        outs.append(o)
        lses.append(jnp.broadcast_to(mx + jnp.log(den), o.shape))
    return jnp.where(head0, outs[0], outs[1]), jnp.where(head0, lses[0], lses[1])


def _merge_groups(outs, lses):
    mx = functools.reduce(jnp.maximum, lses)
    es = [jnp.exp(l - mx) for l in lses]
    num = functools.reduce(lambda a, b: a + b, [e * o for e, o in zip(es, outs)])
    return num / functools.reduce(lambda a, b: a + b, es)


def _rows(start, size, stride):
    return pl.ds(start, size, stride=stride) if stride > 1 else pl.ds(start, size)


def _dil_attn_kernel(*refs):
    ng = len(DIL_GROUPS)
    y_ref, o_scr, l_scr = refs[5 * ng:]
    j = pl.program_id(1)
    tq = y_ref.shape[1]
    qi = lax.broadcasted_iota(jnp.int32, (BAND_BLK, 2 * BAND_BLK), 0)
    kj = lax.broadcasted_iota(jnp.int32, (BAND_BLK, 2 * BAND_BLK), 1)
    dist = BAND_BLK + qi - kj
    has_prev = kj + jnp.where(j > 0, BAND_BLK, 0) >= BAND_BLK
    for gi, (win, dil) in enumerate(DIL_GROUPS):
        band = (dist >= 0) & (dist <= win // dil)
        step = BAND_BLK * dil
        q_ref, k_ref, v_ref, kp_ref, vp_ref = refs[5 * gi:5 * gi + 5]
        for r in range(dil):
            for sb in range(tq // step):
                rows = _rows(r + sb * step, BAND_BLK, dil)
                if sb == 0:
                    prow = _rows(r, BAND_BLK, dil)
                    k_prev, v_prev = kp_ref[0, prow, :], vp_ref[0, prow, :]
                    valid = band & has_prev
                else:
                    prow = _rows(r + (sb - 1) * step, BAND_BLK, dil)
                    k_prev, v_prev = k_ref[0, prow, :], v_ref[0, prow, :]
                    valid = band
                o, lse = _softmax_pair(q_ref[0, rows, :],
                                       jnp.concatenate([k_prev, k_ref[0, rows, :]], axis=0),
                                       jnp.concatenate([v_prev, v_ref[0, rows, :]], axis=0), valid)
                o_scr[gi, rows, :] = o
                l_scr[gi, rows, :] = lse
    y_ref[0] = _merge_groups([o_scr[g] for g in range(ng)], [l_scr[g] for g in range(ng)])


def _dil_attn(q3, k3, v3):
    n, t, w = q3.shape
    tq = BAND_BLK * max(d for _, d in DIL_GROUPS)
    assert t % tq == 0
    specs, args = [], []
    for gi, (_, dil) in enumerate(DIL_GROUPS):
        per = tq // (BAND_BLK * dil)
        cur = pl.BlockSpec((1, tq, LANES), lambda b, j, gi=gi: (b, j, gi))
        prev = pl.BlockSpec((1, BAND_BLK * dil, LANES),
                            lambda b, j, per=per, gi=gi: (b, jnp.maximum(j * per - 1, 0), gi))
        specs += [cur, cur, cur, prev, prev]
        args += [q3, k3, v3, k3, v3]
    ng = len(DIL_GROUPS)
    return pl.pallas_call(
        _dil_attn_kernel,
        grid=(n, t // tq),
        in_specs=specs,
        out_specs=pl.BlockSpec((1, tq, LANES), lambda b, j: (b, j, 0)),
        out_shape=jax.ShapeDtypeStruct((n, t, LANES), F32),
        scratch_shapes=[pltpu.VMEM((ng, tq, LANES), F32), pltpu.VMEM((ng, tq, LANES), F32)],
        compiler_params=_params("parallel", "parallel"),
        name="dil_attn",
    )(*args)


def _sample_attn_kernel(q_ref, k_ref, v_ref, *rest):
    ng = len(DIL_GROUPS)
    cache_refs, y_ref, new_refs = rest[:ng], rest[2 * ng], rest[2 * ng + 1:]
    t = q_ref.shape[1]
    mq = 2 * SUBLANES
    zpad = jnp.zeros((mq - t, ATT_PAIR), F32)
    lane = lax.broadcasted_iota(jnp.int32, (mq, ATT_PAIR), 1)
    head0 = lane < ATT_HD
    tn = lax.broadcasted_iota(jnp.int32, (mq, mq), 0)
    jn = lax.broadcasted_iota(jnp.int32, (mq, mq), 1)
    outs, lses = [], []
    for gi, (win, dil) in enumerate(DIL_GROUPS):
        reach = dil * (win // dil)
        cols = slice(gi * ATT_PAIR, (gi + 1) * ATT_PAIR)
        ct = cache_refs[gi][0, 0]
        hist = ct.shape[1]
        kt2 = ct[:ATT_PAIR].astype(BF16)
        vt2 = ct[ATT_PAIR:].astype(BF16)
        k_new, v_new = k_ref[0, :, cols], v_ref[0, :, cols]
        q = jnp.concatenate([q_ref[0, :, cols], zpad], axis=0)
        kn = jnp.concatenate([k_new, zpad], axis=0).astype(BF16)
        vn = jnp.concatenate([v_new, zpad], axis=0).astype(BF16)
        ti = lax.broadcasted_iota(jnp.int32, (mq, hist), 0)
        pj = lax.broadcasted_iota(jnp.int32, (mq, hist), 1)
        d_old = hist + ti - pj
        ok_old = (d_old <= reach) & ((d_old & (dil - 1)) == 0)
        d_new = tn - jn
        ok_new = (d_new >= 0) & (d_new <= reach) & ((d_new & (dil - 1)) == 0) & (jn < t)
        o_h, l_h = [], []
        for hs in range(2):
            hm = head0 if hs == 0 else jnp.logical_not(head0)
            qm = jnp.where(hm, q, 0.0).astype(BF16)
            s_old = jnp.where(ok_old, jnp.dot(qm, kt2, preferred_element_type=F32), NEG_BIG)
            s_new = jnp.where(ok_new, lax.dot_general(qm, kn, NT, preferred_element_type=F32), NEG_BIG)
            mx = jnp.maximum(jnp.max(s_old, axis=-1, keepdims=True), jnp.max(s_new, axis=-1, keepdims=True))
            e_old = jnp.exp(s_old - mx)
            e_new = jnp.exp(s_new - mx)
            den = jnp.sum(e_old, axis=-1, keepdims=True) + jnp.sum(e_new, axis=-1, keepdims=True)
            num = (lax.dot_general(e_old.astype(BF16), vt2, NT, preferred_element_type=F32)
                   + jnp.dot(e_new.astype(BF16), vn, preferred_element_type=F32))
            o_h.append(num / den)
            l_h.append(jnp.broadcast_to(mx + jnp.log(den), num.shape))
        outs.append(jnp.where(head0, o_h[0], o_h[1])[:t])
        lses.append(jnp.where(head0, l_h[0], l_h[1])[:t])

        shifted = pltpu.roll(ct, hist - t, 1)
        fresh = jnp.concatenate([jnp.zeros((LANES - t, 2 * ATT_PAIR), F32),
                                 jnp.concatenate([k_new, v_new], axis=1)], axis=0).T
        lane_c = lax.broadcasted_iota(jnp.int32, (2 * ATT_PAIR, LANES), 1)
        if hist > LANES:
            new_refs[gi][0, 0, :, :hist - LANES] = shifted[:, :hist - LANES]
        new_refs[gi][0, 0, :, hist - LANES:] = jnp.where(lane_c >= LANES - t, fresh, shifted[:, hist - LANES:])
    y_ref[0] = _merge_groups(outs, lses)


def _sample_attn(q3, k3, v3, caches_t, accs, layer):
    n, t, w = q3.shape
    ng = len(DIL_GROUPS)
    for (win, dil), c in zip(DIL_GROUPS, caches_t):
        assert c.shape[3] == win and win % LANES == 0 and dil & (dil - 1) == 0 and t <= SUBLANES
    new = pl.BlockSpec((1, t, w), lambda b: (b, 0, 0))
    blk = lambda c: pl.BlockSpec((1, 1) + c.shape[2:], lambda b: (layer, b, 0, 0))
    res = pl.pallas_call(
        _sample_attn_kernel,
        grid=(n,),
        in_specs=[new, new, new] + [blk(c) for c in caches_t]
        + [pl.BlockSpec(memory_space=pl.ANY)] * ng,
        out_specs=[pl.BlockSpec((1, t, LANES), lambda b: (b, 0, 0))] + [blk(c) for c in accs],
        out_shape=[jax.ShapeDtypeStruct((n, t, LANES), F32)]
        + [jax.ShapeDtypeStruct(a.shape, F32) for a in accs],
        input_output_aliases={3 + ng + g: 1 + g for g in range(ng)},
        compiler_params=_params("parallel"),
        name="sample_attn",
    )(q3, k3, v3, *caches_t, *accs)
    return res[0], list(res[1:])


def _rwkv_prep_kernel(prev8_ref, sp_ref, cur_ref, mu_ref, wwa_ref, w0_ref, a0_ref, g2_ref,
                      kk_ref, ka_ref, rk_ref, ones_ref, tril_ref,
                      at_ref, bt_ref, kt_ref, rt_ref, v_ref, pl_ref, g_ref, bonus_ref,
                      sh_ref, p_ref, *, chunk):
    bn, tm, w = cur_ref.shape
    j = pl.program_id(1)
    cur = cur_ref[...]
    prev_row = jnp.where(j == 0, sp_ref[...], prev8_ref[:, SUBLANES - 1:SUBLANES, :])
    sh_ref[:, SUBLANES - 1:SUBLANES, :] = prev_row
    sh_ref[:, SUBLANES:, :] = cur
    prev = sh_ref[:, SUBLANES - 1:SUBLANES - 1 + tm, :]
    m = bn * tm
    zs = (cur + (prev - cur) * mu_ref[...]).reshape(m, w)

    r = zs[:, 0:RWKV_W]
    k = zs[:, RWKV_W:2 * RWKV_W]
    v = zs[:, 2 * RWKV_W:3 * RWKV_W]
    lo = 3 * RWKV_W
    z_wa = zs[:, lo:lo + LANES]
    z_g = zs[:, lo + LANES:lo + 2 * LANES]
    lane = lax.broadcasted_iota(jnp.int32, z_wa.shape, 1)
    u = jnp.where(lane < LANES // 2, jnp.tanh(z_wa), z_wa)
    lora = _bdot(u, wwa_ref[...])
    xw = w0_ref[...] + lora[:, :RWKV_W]
    w_log = -(jnp.maximum(-xw, 0.0) + jnp.log(1.0 + jnp.exp(-jnp.abs(xw)))) - 0.5
    e = jnp.exp(w_log)
    a = _sigmoid(a0_ref[...] + lora[:, RWKV_W:])
    g_ref[...] = _bdot(_sigmoid(z_g), g2_ref[...])

    ones_bd = ones_ref[...]
    kk = k * kk_ref[...]
    kk = kk * lax.rsqrt(jnp.maximum(_mm2_exact_rhs(kk * kk, ones_bd), 1e-24))
    k = k * (1.0 + (a - 1.0) * ka_ref[...])
    bonus_ref[...] = _mm2_exact_rhs(r * k * rk_ref[...], ones_bd) * v

    tril = tril_ref[...]
    e1 = e.astype(BF16)
    rem = e - e1.astype(F32)
    e2 = rem.astype(BF16)
    e3 = (rem - e2.astype(F32)).astype(BF16)
    f = lambda x: jnp.dot(tril, x, preferred_element_type=F32)
    cum = f(e1) + (f(e2) + f(e3))
    p_inc = jnp.exp(-cum)
    p_inv = jnp.exp(cum)
    at_ref[...] = -kk * jnp.exp(e - cum)
    bt_ref[...] = kk * a * p_inv
    kt_ref[...] = k * p_inv
    rt_ref[...] = r * p_inc
    v_ref[...] = v
    p_ref[...] = p_inc
    for c in range(m // chunk):
        pl_ref[c] = p_ref[(c + 1) * chunk - 1:(c + 1) * chunk, :]


def _rwkv_prep(zr3, shift_prev, state_layer, wts, layer, *, bn, tm, chunk):
    n, t, w = zr3.shape
    m = bn * tm
    per = tm // SUBLANES
    rowblk = pl.BlockSpec((m, RWKV_W), lambda b, j: (b * (t // tm) + j, 0))
    tril = (jnp.arange(m)[:, None] >= jnp.arange(m)[None, :]) & \
           (jnp.arange(m)[:, None] // chunk == jnp.arange(m)[None, :] // chunk)
    nch = m // chunk
    names = ('rwkv_mu', 'rwkv_wwa', 'rwkv_w0', 'rwkv_a0', 'rwkv_g2', 'rwkv_k_k', 'rwkv_k_a', 'rwkv_r_k')
    consts = [wts[k] for k in names] + [wts['ones_bd'], tril.astype(BF16)]
    const_specs = [_wspec(wts[k], layer) for k in names] + [_wspec(wts['ones_bd']), _wspec(consts[-1])]
    outs = pl.pallas_call(
        functools.partial(_rwkv_prep_kernel, chunk=chunk),
        grid=(n // bn, t // tm),
        in_specs=[pl.BlockSpec((bn, SUBLANES, w), lambda b, j: (b, jnp.maximum(j * per - 1, 0), 0)),
                  pl.BlockSpec((None, bn, 1, w), lambda b, j: (state_layer, b, 0, 0)),
                  pl.BlockSpec((bn, tm, w), lambda b, j: (b, j, 0))]
        + const_specs,
        out_specs=[rowblk] * 5
        + [pl.BlockSpec((nch, 1, RWKV_W), lambda b, j: (b * (t // tm) + j, 0, 0)), rowblk, rowblk],
        out_shape=[jax.ShapeDtypeStruct((n * t, RWKV_W), F32)] * 5
        + [jax.ShapeDtypeStruct((n * t // chunk, 1, RWKV_W), F32)]
        + [jax.ShapeDtypeStruct((n * t, RWKV_W), F32)] * 2,
        scratch_shapes=[pltpu.VMEM((bn, tm + SUBLANES, w), F32), pltpu.VMEM((m, RWKV_W), F32)],
        compiler_params=_params("parallel", "parallel"),
        name="rwkv_prep",
    )(zr3, shift_prev, zr3, *consts)
    return outs


def _rwkv_scan_kernel(at_ref, bt_ref, kt_ref, rt_ref, v_ref, pl_ref, s0_ref, y_ref, sT_ref, s_ref):
    c = pl.program_id(1)
    nb, chunk, _ = at_ref.shape
    npair = RWKV_W // LANES
    rows = 2 * chunk

    @pl.when(c == 0)
    def _():
        s_ref[...] = s0_ref[...]

    lane = lax.broadcasted_iota(jnp.int32, (chunk, LANES), 1)
    head0 = lane < RWKV_HD
    ri = lax.broadcasted_iota(jnp.int32, (rows, rows), 0)
    ci = lax.broadcasted_iota(jnp.int32, (rows, rows), 1)
    same = (ri < chunk) == (ci < chunk)
    strict = same & (ci < ri)
    incl = same & (ci <= ri)
    eye = jnp.where(ri == ci, 1.0, 0.0).astype(F32)

    def stack(x):
        return jnp.concatenate([jnp.where(head0, x, 0.0), jnp.where(head0, 0.0, x)], axis=0)

    def mm(a, b, dims=NN):
        (ah, al), (bh, bl) = a, b
        lhs = jnp.concatenate([ah, ah, al], axis=0 if dims == TN else 1)
        rhs = jnp.concatenate([bh, bl, bh], axis=1 if dims == NT else 0)
        return lax.dot_general(lhs, rhs, dims, preferred_element_type=F32)

    jobs = [(sq, p, slice(p * LANES, (p + 1) * LANES)) for sq in range(nb) for p in range(npair)]
    ast = [stack(at_ref[sq, :, cs]) for sq, p, cs in jobs]
    rst = [stack(rt_ref[sq, :, cs]) for sq, p, cs in jobs]
    vst = [stack(v_ref[sq, :, cs]) for sq, p, cs in jobs]
    gram_a, gram_r = [], []
    for i, (sq, p, cs) in enumerate(jobs):
        bt, kt = bt_ref[sq, :, cs], kt_ref[sq, :, cs]
        bk = _split(jnp.concatenate([bt, bt, kt, kt], axis=0))
        gram_a.append(mm(_split(ast[i]), bk, NT))
        gram_r.append(lax.dot_general(rst[i].astype(BF16), bk[0], NT, preferred_element_type=F32))
    m_ab = [jnp.where(strict, g[:, :rows], 0.0) for g in gram_a]
    m_ak = [jnp.where(strict, g[:, rows:], 0.0) for g in gram_a]
    m_r = [jnp.concatenate([jnp.where(incl, g[:, :rows], 0.0),
                            jnp.where(incl, g[:, rows:], 0.0)], axis=1).astype(BF16) for g in gram_r]
    assert len(jobs) % 2 == 0
    zero_bf = jnp.zeros((rows, rows), BF16)

    def pair_dot(n_i, n_j, x_i, x_j):
        rhs = jnp.concatenate([jnp.concatenate([x_i, zero_bf], axis=1),
                               jnp.concatenate([zero_bf, x_j], axis=1)], axis=0)
        out = jnp.dot(jnp.concatenate([n_i, n_j], axis=1), rhs, preferred_element_type=F32)
        return out[:, :rows], out[:, rows:]

    npow = [m.astype(BF16) for m in m_ab]
    inv = [eye + m for m in m_ab]
    for _ in range(int(math.log2(chunk)) - 1):
        for i in range(0, len(jobs), 2):
            sq_i, sq_j = pair_dot(npow[i], npow[i + 1], npow[i], npow[i + 1])
            npow[i], npow[i + 1] = sq_i.astype(BF16), sq_j.astype(BF16)
            up_i, up_j = pair_dot(npow[i], npow[i + 1], inv[i].astype(BF16), inv[i + 1].astype(BF16))
            inv[i], inv[i + 1] = inv[i] + up_i, inv[i + 1] + up_j
    mv = [mm(_split(m), _split(x)) for m, x in zip(m_ak, vst)]
    wu = [mm(_split(t), _split(jnp.concatenate([x, a], axis=1))) for t, x, a in zip(inv, mv, ast)]
    z = [_split(jnp.concatenate([jnp.concatenate([w[:, LANES:], w[:, :LANES]], axis=1),
                                 jnp.concatenate([jnp.zeros_like(x), x], axis=1)], axis=0))
         for w, x in zip(wu, vst)]
    qy = [jnp.dot(m, zz[0], preferred_element_type=F32) for m, zz in zip(m_r, z)]
    ad = []
    for i, (sq, p, cs) in enumerate(jobs):
        decay = pl_ref[sq, 0, :, cs]
        bk = jnp.concatenate([stack(bt_ref[sq, :, cs]), stack(kt_ref[sq, :, cs])], axis=0) * decay
        ad.append(mm(_split(bk), z[i], TN))
    for i, (sq, p, cs) in enumerate(jobs):
        a_t = eye * pl_ref[sq, 0, :, cs] + ad[i][:, :LANES]
        st = _split(s_ref[sq, p])
        yst = jnp.dot((rst[i] + qy[i][:, :LANES]).astype(BF16), st[0],
                      preferred_element_type=F32) + qy[i][:, LANES:]
        y_ref[sq, :, cs] = yst[:chunk] + yst[chunk:]
        s_ref[sq, p] = mm(_split(a_t), st) + ad[i][:, LANES:]

    @pl.when(c == pl.num_programs(1) - 1)
    def _():
        sT_ref[...] = s_ref[...]


def _rwkv_scan(at, bt, kt, rt, v, plast, s0, state_layer, *, n, t, chunk, nb):
    assert 2 * chunk == LANES
    npair = RWKV_W // LANES
    nch = t // chunk
    tok = pl.BlockSpec((nb, chunk, RWKV_W), lambda b, c: (b, c, 0))
    st = pl.BlockSpec((nb, npair, LANES, LANES), lambda b, c: (b, 0, 0, 0))
    st_in = pl.BlockSpec((None, nb, npair, LANES, LANES), lambda b, c: (state_layer, b, 0, 0, 0))
    v3 = lambda a: a.reshape(n, t, RWKV_W)
    y, s_t = pl.pallas_call(
        _rwkv_scan_kernel,
        grid=(n // nb, nch),
        in_specs=[tok] * 5 + [pl.BlockSpec((nb, 1, 1, RWKV_W), lambda b, c: (b, c, 0, 0)), st_in],
        out_specs=[tok, st],
        out_shape=[jax.ShapeDtypeStruct((n, t, RWKV_W), F32),
                   jax.ShapeDtypeStruct((n, npair, LANES, LANES), F32)],
        scratch_shapes=[pltpu.VMEM((nb, npair, LANES, LANES), F32)],
        compiler_params=_params("parallel", "arbitrary"),
        name="rwkv_scan",
    )(v3(at), v3(bt), v3(kt), v3(rt), v3(v), plast.reshape(n, nch, 1, RWKV_W), s0)
    return y, s_t


def _merge_kernel(x_ref, yp_ref, ya_ref, ys_ref, bonus_ref, g_ref, gate_ref,
                  pp_ref, pa_ref, pr_ref, wo_ref, lng_ref, lnb_ref, mean_ref, out_ref):
    d = x_ref.shape[1]
    y_attn = ya_ref[...]

    ys = ys_ref[...]
    mean_bd = mean_ref[...]
    mu = _mm2_exact_rhs(ys, mean_bd)
    dev = ys - mu
    var = _mm2_exact_rhs(dev * dev, mean_bd)
    yn = dev * lax.rsqrt(var + RWKV_LN_EPS) * lng_ref[...] + lnb_ref[...]
    y_rwkv = (yn + bonus_ref[...]) * g_ref[...]

    gate = lambda c: gate_ref[:, c * d:(c + 1) * d].astype(F32)
    merged = (gate(0) * _bdot(yp_ref[...], pp_ref[...])
              + gate(1) * _bdot(y_attn, pa_ref[...])
              + gate(2) * _bdot(y_rwkv, pr_ref[...]))
    out_ref[...] = x_ref[...] + _bdot(merged, wo_ref[...])


def _merge(x2d, yp, ya, ys, bonus, g, gates, wts, layer, tm):
    m, d = x2d.shape
    row = lambda w: pl.BlockSpec((tm, w), lambda i: (i, 0))
    names = ('proj_pool', 'proj_attn', 'proj_rwkv', 'w_out', 'rwkv_ln_g', 'rwkv_ln_b')
    consts = [wts[k] for k in names] + [wts['mean_bd']]
    return pl.pallas_call(
        _merge_kernel,
        grid=(m // tm,),
        in_specs=[row(d), row(POOL_W), row(LANES)] + [row(RWKV_W)] * 3 + [row(3 * d)]
        + [_wspec(wts[k], layer) for k in names] + [_wspec(wts['mean_bd'])],
        out_specs=row(d),
        out_shape=jax.ShapeDtypeStruct((m, d), F32),
        compiler_params=_params("parallel"),
        name="merge_out_proj",
    )(x2d, yp, ya, ys, bonus, g, gates, *consts)


FFN_SPLIT = 2


def _ffn_ple_kernel(x_ref, p_ref, gf_ref, g1_ref, w1_ref, w3_ref, w2_ref, g2_ref, wg_ref, wp_ref,
                    out_ref, *, final_norm):
    x = x_ref[...]
    h = _rms(x, g1_ref[...]).astype(BF16)
    part = w1_ref.shape[1] // FFN_SPLIT
    y = x
    for c in range(FFN_SPLIT):
        cs = slice(c * part, (c + 1) * part)
        h1 = jnp.dot(h, w1_ref[:, cs], preferred_element_type=F32)
        h3 = jnp.dot(h, w3_ref[:, cs], preferred_element_type=F32)
        y = y + _bdot(h1 * _sigmoid(h1) * h3, w2_ref[cs, :])
    gate = _sigmoid(_bdot(_rms(y, g2_ref[...]), wg_ref[...]))
    y = y + _bdot(p_ref[...], wp_ref[...]) * gate
    if final_norm:
        y = _rms(y, gf_ref[...])
    out_ref[...] = y


def _ffn_ple(x2d, p_all, wts, layer, g_final, tm, final_norm):
    m, d = x2d.shape
    assert wts['ffn_w1'].shape[-1] % (FFN_SPLIT * LANES) == 0
    row = pl.BlockSpec((tm, d), lambda i: (i, 0))
    names = ('norm_ffn_g', 'ffn_w1', 'ffn_w3', 'ffn_w2', 'norm_ple_g', 'ple_gate', 'ple_proj')
    return pl.pallas_call(
        functools.partial(_ffn_ple_kernel, final_norm=final_norm),
        grid=(m // tm,),
        in_specs=[row, pl.BlockSpec((None, tm, p_all.shape[2]), lambda i: (layer, i, 0)), _wspec(g_final)]
        + [_wspec(wts[k], layer) for k in names],
        out_specs=row,
        out_shape=jax.ShapeDtypeStruct((m, d), F32),
        compiler_params=_params("parallel"),
        name="swiglu_ple_final" if final_norm else "swiglu_ple",
    )(x2d, p_all, g_final, *[wts[k] for k in names])


def _rope_tables(pos, reps):
    half = ATT_HD // 2
    inv = ROPE_THETA ** (-2.0 * jnp.arange(half, dtype=F32) / ATT_HD)
    ang = pos.astype(F32)[:, None] * inv[None, :]
    cos = jnp.cos(ang)
    sin = jnp.sin(ang)
    cos = jnp.tile(jnp.concatenate([cos, cos], axis=1), (reps, LANES // ATT_HD))
    sin = jnp.tile(jnp.concatenate([-sin, sin], axis=1), (reps, LANES // ATT_HD))
    return cos, sin


def _block_diag(blocks):
    g, a, b = blocks.shape[-3:]
    rows = []
    for i in range(g):
        parts = [blocks[..., i, :, :] if j == i else jnp.zeros_like(blocks[..., i, :, :]) for j in range(g)]
        rows.append(jnp.concatenate(parts, axis=-1))
    return jnp.concatenate(rows, axis=-2)


def _stacked_weights(w):
    row = lambda a: a.reshape(a.shape[0], 1, -1)
    bf = lambda a: a.astype(BF16)
    heads = RWKV_W // RWKV_HD
    zero = jnp.zeros_like(w['rwkv_w2'])
    wwa = jnp.concatenate([jnp.concatenate([w['rwkv_w2'], zero], axis=2),
                           jnp.concatenate([zero, w['rwkv_a2']], axis=2)], axis=1)
    ones = jnp.ones((heads, RWKV_HD, RWKV_HD), F32)
    return {
        'norm_mix_g': row(w['norm_mix_g']), 'w_in': bf(w['w_in']),
        'pool_w_bd': bf(_block_diag(w['pool_w_grp'])), 'pool_scale': row(w['pool_scale']),
        'rwkv_mu': row(w['rwkv_mu']), 'rwkv_wwa': bf(wwa),
        'rwkv_w0': row(w['rwkv_w0']), 'rwkv_a0': row(w['rwkv_a0']), 'rwkv_g2': bf(w['rwkv_g2']),
        'rwkv_k_k': row(w['rwkv_k_k']), 'rwkv_k_a': row(w['rwkv_k_a']), 'rwkv_r_k': row(w['rwkv_r_k']),
        'rwkv_ln_g': row(w['rwkv_ln_g']), 'rwkv_ln_b': row(w['rwkv_ln_b']),
        'ones_bd': bf(_block_diag(ones)), 'mean_bd': bf(_block_diag(ones / RWKV_HD)),
        'proj_pool': bf(w['proj_pool']), 'proj_attn': bf(w['proj_attn']),
        'proj_rwkv': bf(w['proj_rwkv']), 'w_out': bf(w['w_out']),
        'norm_ffn_g': row(w['norm_ffn_g']), 'ffn_w1': bf(w['ffn_w1']), 'ffn_w3': bf(w['ffn_w3']),
        'ffn_w2': bf(w['ffn_w2']), 'norm_ple_g': row(w['norm_ple_g']),
        'ple_proj': bf(w['ple_proj']), 'ple_gate': bf(w['ple_gate']),
    }


def _token_tiles(m):
    tm = next(c for c in (512, 256, 128, m) if m % c == 0)
    return tm, min(tm, 256)


def _pair_states(s):
    st = jnp.swapaxes(s, -1, -2)
    st = st.reshape(st.shape[:-3] + (st.shape[-3] // 2, 2) + st.shape[-2:])
    return _block_diag(st)


def _unpair_states(s2):
    half = s2.shape[-1] // 2
    s = jnp.stack([s2[..., :half, :half], s2[..., half:, half:]], axis=-3)
    s = s.reshape(s.shape[:-4] + (2 * s.shape[-4],) + s.shape[-2:])
    return jnp.swapaxes(s, -1, -2)


def _decoder_layer(x2d, n, t, p_all, wts, layer, g_final, final_norm, tabs, state, *, prompt, tm, tm_in, chunk):
    sl = state['layer']
    zp, q, k, v, zr, gates = _in_proj(x2d, wts, layer, tabs[0], tabs[1], tm_in)
    three = lambda a: a.reshape(n, t, a.shape[-1])
    zp3, q3, k3, v3, zr3 = three(zp), three(q), three(k), three(v), three(zr)

    if prompt:
        yp = _pool(None, zp3, wts, layer, bn=1, tm=tm, pos0=0)
        new_pool = zp3[:, t - POOL_HIST:]
    else:
        yp = _pool(state['pool16'], zp3, wts, layer, bn=n, tm=t, pos0=state['pos0'])
        new_pool = jnp.concatenate([state['pool16'][sl], zp3], axis=1)[:, -POOL_HIST:]

    if prompt:
        ya = _dil_attn(q3, k3, v3)
        new_kv = []
        for gi, (win, dil) in enumerate(DIL_GROUPS):
            cols = slice(gi * ATT_PAIR, (gi + 1) * ATT_PAIR)
            keep = min(win, t)
            kv = jnp.concatenate([k3[:, t - keep:, cols], v3[:, t - keep:, cols]], axis=-1)
            new_kv.append(kv.reshape(n, keep, 2, 2, ATT_HD))
    else:
        ya, new_kv = _sample_attn(q3, k3, v3, state['kv_t'], state['kv_acc'], sl)

    if prompt:
        at, bt, kt, rt, vv, plast, g, bonus = _rwkv_prep(zr3, state['shift'], sl, wts, layer,
                                                         bn=1, tm=tm, chunk=chunk)
        ys, s_t = _rwkv_scan(at, bt, kt, rt, vv, plast, state['wkv'], sl, n=n, t=t, chunk=chunk, nb=n)
    else:
        at, bt, kt, rt, vv, plast, g, bonus = _rwkv_prep(zr3, state['shift'], sl, wts, layer,
                                                         bn=n, tm=t, chunk=t)
        padt = lambda a: jnp.pad(a.reshape(n, t, RWKV_W), ((0, 0), (0, chunk - t), (0, 0))).reshape(n * chunk, RWKV_W)
        ys, s_t = _rwkv_scan(padt(at), padt(bt), padt(kt), padt(rt), padt(vv), plast, state['wkv'], sl,
                             n=n, t=chunk, chunk=chunk, nb=2)
        ys = ys[:, :t]
    new_shift = zr3[:, t - 1]

    x2d = _merge(x2d, yp.reshape(n * t, POOL_W), ya.reshape(n * t, LANES), ys.reshape(n * t, RWKV_W),
                 bonus, g, gates, wts, layer, tm)
    x2d = _ffn_ple(x2d, p_all, wts, layer, g_final, tm, final_norm)
    return x2d, (new_pool, new_shift, s_t, new_kv)


def kernel(x_prompt, x_sample, state_pool, state_shift, state_wkv, cache_kv_w128, cache_kv_w512,
           cache_kv_w2048, p_prompt, p_sample, norm_mix_g, w_in, pool_w_grp, pool_scale, rwkv_mu,
           rwkv_w0, rwkv_w2, rwkv_a0, rwkv_a2, rwkv_g2, rwkv_k_k, rwkv_k_a, rwkv_r_k, rwkv_ln_g,
           rwkv_ln_b, proj_pool, proj_attn, proj_rwkv, w_out, norm_ffn_g, ffn_w1, ffn_w3, ffn_w2,
           norm_ple_g, ple_proj, ple_gate, norm_final_g):
    weights = dict(norm_mix_g=norm_mix_g, w_in=w_in, pool_w_grp=pool_w_grp, pool_scale=pool_scale,
                   rwkv_mu=rwkv_mu, rwkv_w0=rwkv_w0, rwkv_w2=rwkv_w2, rwkv_a0=rwkv_a0, rwkv_a2=rwkv_a2,
                   rwkv_g2=rwkv_g2, rwkv_k_k=rwkv_k_k, rwkv_k_a=rwkv_k_a, rwkv_r_k=rwkv_r_k,
                   rwkv_ln_g=rwkv_ln_g, rwkv_ln_b=rwkv_ln_b, proj_pool=proj_pool, proj_attn=proj_attn,
                   proj_rwkv=proj_rwkv, w_out=w_out, norm_ffn_g=norm_ffn_g, ffn_w1=ffn_w1, ffn_w3=ffn_w3,
                   ffn_w2=ffn_w2, norm_ple_g=norm_ple_g, ple_proj=ple_proj, ple_gate=ple_gate)
    depth = w_in.shape[0]
    np_, tp, d = x_prompt.shape
    ns, ts, _ = x_sample.shape
    caches = (cache_kv_w128, cache_kv_w512, cache_kv_w2048)
    past_len = PAST_LEN
    tm_p, tm_in_p = _token_tiles(np_ * tp)
    tm_s = ns * ts
    chunk = 64
    tabs_p = _rope_tables(jnp.arange(tp), 1)
    tabs_s = _rope_tables(past_len + jnp.arange(ts), ns)
    g_final = norm_final_g[None, :]
    wts = _stacked_weights(weights)

    xp = x_prompt.reshape(np_ * tp, d)
    xs = x_sample.reshape(ns * ts, d)
    pp_all = p_prompt.reshape(depth, np_ * tp, -1)
    ps_all = p_sample.reshape(depth, ns * ts, -1)
    heads = RWKV_W // RWKV_HD
    state_p = {
        'layer': 0,
        'shift': jnp.zeros((1, np_, 1, RWKV_PROJ), F32),
        'wkv': jnp.zeros((1, np_, heads // 2, 2 * RWKV_HD, 2 * RWKV_HD), F32),
    }
    caches_t = [jnp.transpose(c, (0, 1, 3, 4, 5, 2)).reshape(c.shape[0], c.shape[1], -1, c.shape[2])
                for c in caches]
    kv_acc = [jnp.zeros(c.shape, F32) for c in caches_t]
    state_s = {
        'pool16': jnp.pad(state_pool, ((0, 0), (0, 0), (POOL_HALO - POOL_HIST, 0), (0, 0))),
        'shift': state_shift[:, :, None, :], 'wkv': _pair_states(state_wkv),
        'kv_t': caches_t, 'pos0': past_len,
    }
    outs_p, outs_s = [], []
    for i in range(depth):
        last = i == depth - 1
        xp, st_p = _decoder_layer(xp, np_, tp, pp_all, wts, i, g_final, last, tabs_p, state_p,
                                  prompt=True, tm=tm_p, tm_in=tm_in_p, chunk=chunk)
        xs, st_s = _decoder_layer(xs, ns, ts, ps_all, wts, i, g_final, last, tabs_s,
                                  dict(state_s, layer=i, kv_acc=kv_acc),
                                  prompt=False, tm=tm_s, tm_in=tm_s, chunk=chunk)
        kv_acc = st_s[3]
        outs_p.append(st_p)
        outs_s.append(st_s)

    stack = lambda outs, f: jnp.stack([f(o) for o in outs])
    res = [xp.reshape(np_, tp, d), xs.reshape(ns, ts, d)]
    for idx in range(2):
        res.append(stack(outs_p, lambda o: o[idx]))
        res.append(stack(outs_s, lambda o: o[idx]))
    res.append(_unpair_states(stack(outs_p, lambda o: o[2])))
    res.append(_unpair_states(stack(outs_s, lambda o: o[2])))
    for gi in range(len(DIL_GROUPS)):
        res.append(stack(outs_p, lambda o: o[3][gi]))
        a = kv_acc[gi]
        a = a.reshape(a.shape[0], a.shape[1], 2, 2, ATT_HD, a.shape[3])
        res.append(jnp.transpose(a, (0, 1, 5, 2, 3, 4)))
    return tuple(res)
```

```python
import functools
import math

import jax
import jax.numpy as jnp
from jax import lax
from jax.experimental import pallas as pl
from jax.experimental.pallas import tpu as pltpu

F32 = jnp.float32
BF16 = jnp.bfloat16

POOL_GC = 64
POOL_W = 256
POOL_WINDOWS = (2, 4, 8, 16)
POOL_HIST = 15
POOL_HALO = 16
ATT_HD = 64
ATT_W = 384
ATT_PAIR = 128
DIL_GROUPS = ((128, 1), (512, 4), (2048, 16))
BAND_BLK = 128
ROPE_THETA = 10000.0
RWKV_HD = 64
RWKV_W = 384
RWKV_PROJ = 1408
RWKV_LN_EPS = 64e-5
RMS_EPS = 1e-6
PAST_LEN = 8192
NEG_BIG = -1e30

LANES = 128
SUBLANES = 8
VMEM_LIMIT = 56 * 1024 * 1024

NN = (((1,), (0,)), ((), ()))
NT = (((1,), (1,)), ((), ()))
TN = (((0,), (0,)), ((), ()))


def _params(*sem):
    return pltpu.CompilerParams(dimension_semantics=sem, vmem_limit_bytes=VMEM_LIMIT)


def _wspec(arr, layer=None):
    if layer is None:
        nd = arr.ndim
        return pl.BlockSpec(arr.shape, lambda *_: (0,) * nd, pipeline_mode=pl.Buffered(1))
    nd = arr.ndim - 1
    return pl.BlockSpec((None,) + arr.shape[1:], lambda *_: (layer,) + (0,) * nd,
                        pipeline_mode=pl.Buffered(1))


def _bdot(a, b, dims=NN):
    return lax.dot_general(a.astype(BF16), b.astype(BF16), dims, preferred_element_type=F32)


def _split(a):
    hi = a.astype(BF16)
    lo = (a - hi.astype(F32)).astype(BF16)
    return hi, lo


def _mm3(a, b, dims=NN):
    ah, al = _split(a)
    bh, bl = _split(b)
    f = lambda x, y: lax.dot_general(x, y, dims, preferred_element_type=F32)
    return f(ah, bh) + (f(ah, bl) + f(al, bh))


def _mm2_exact_rhs(a, b_bf16, dims=NN):
    ah, al = _split(a)
    f = lambda x: lax.dot_general(x, b_bf16, dims, preferred_element_type=F32)
    return f(ah) + f(al)


def _rms(x, g):
    return x * lax.rsqrt(jnp.mean(x * x, axis=-1, keepdims=True) + RMS_EPS) * g


def _sigmoid(x):
    return 1.0 / (1.0 + jnp.exp(-x))


def _in_proj_kernel(x_ref, g_ref, w_ref, cos_ref, sin_ref,
                    pool_ref, q_ref, k_ref, v_ref, rw_ref, gate_ref):
    h = _rms(x_ref[...], g_ref[...]).astype(BF16)

    def seg(a, b):
        return jnp.dot(h, w_ref[:, a:b], preferred_element_type=F32)

    cos = cos_ref[...]
    sin = sin_ref[...]
    lane = lax.broadcasted_iota(jnp.int32, cos.shape, 1)
    low_half = (lane & 32) == 0

    def rope(t):
        partner = jnp.where(low_half, pltpu.roll(t, 96, 1), pltpu.roll(t, 32, 1))
        return t * cos + partner * sin

    n_mix = POOL_W + 3 * ATT_W + RWKV_PROJ
    mix = seg(0, n_mix)
    o = 0
    pool_ref[...] = mix[:, o:o + POOL_W]
    o += POOL_W
    for c in range(ATT_W // LANES):
        cs = slice(c * LANES, (c + 1) * LANES)
        q_ref[:, cs] = rope(mix[:, o + c * LANES:o + (c + 1) * LANES]) * (ATT_HD ** -0.5)
        k_ref[:, cs] = rope(mix[:, o + ATT_W + c * LANES:o + ATT_W + (c + 1) * LANES])
    o += 2 * ATT_W
    v_ref[...] = mix[:, o:o + ATT_W]
    o += ATT_W
    rw_ref[...] = mix[:, o:o + RWKV_PROJ]
    gate_ref[...] = _sigmoid(seg(n_mix, w_ref.shape[1])).astype(gate_ref.dtype)


def _in_proj(x2d, wts, layer, cos, sin, tm):
    m, d = x2d.shape
    g, w_bf = wts['norm_mix_g'], wts['w_in']
    ncol = w_bf.shape[-1]
    ngate = ncol - (POOL_W + 3 * ATT_W + RWKV_PROJ)
    ntab = cos.shape[0] // tm
    row = lambda w: pl.BlockSpec((tm, w), lambda i: (i, 0))
    tab = pl.BlockSpec((tm, LANES), lambda i: (i % ntab, 0))
    widths = (POOL_W, ATT_W, ATT_W, ATT_W, RWKV_PROJ, ngate)
    dtypes = (F32,) * 5 + (BF16,)
    return pl.pallas_call(
        _in_proj_kernel,
        grid=(m // tm,),
        in_specs=[row(d), _wspec(g, layer), _wspec(w_bf, layer), tab, tab],
        out_specs=[row(w) for w in widths],
        out_shape=[jax.ShapeDtypeStruct((m, w), dt) for w, dt in zip(widths, dtypes)],
        compiler_params=_params("parallel"),
        name="in_proj",
    )(x2d, g, w_bf, cos, sin)


def _pool_kernel(prev_ref, cur_ref, w_ref, scale_ref, y_ref, ext_ref, *, pos0, zero_first_prev):
    bn, tm, w = cur_ref.shape
    j = pl.program_id(1)
    prev = prev_ref[...]
    if zero_first_prev:
        prev = jnp.where(j == 0, 0.0, prev)
    cur = cur_ref[...]
    ext_ref[:, :POOL_HALO, :] = prev
    ext_ref[:, POOL_HALO:, :] = cur

    lane = lax.broadcasted_iota(jnp.int32, (bn, tm, w), 2)
    group = jnp.right_shift(lane, POOL_GC.bit_length() - 1)
    acc = cur
    win = jnp.zeros_like(cur)
    for s in range(1, POOL_WINDOWS[-1] + 1):
        if s in POOL_WINDOWS:
            win = jnp.where(group == POOL_WINDOWS.index(s), acc, win)
        if s < POOL_WINDOWS[-1]:
            acc = acc + ext_ref[:, POOL_HALO - s:POOL_HALO - s + tm, :]
    width = jnp.left_shift(2, group)
    pos = pos0 + j * tm + lax.broadcasted_iota(jnp.int32, (bn, tm, w), 1)
    cnt = jnp.minimum(pos + 1, width).astype(F32)
    dlt = (win / cnt - cur).reshape(bn * tm, w)
    y = _bdot(dlt, w_ref[...]) * scale_ref[...]
    y_ref[...] = y.reshape(bn, tm, w)


def _pool(hist, cur3, wts, layer, *, bn, tm, pos0):
    n, t, w = cur3.shape
    if hist is None:
        per = tm // POOL_HALO
        prev_arr = cur3
        prev_spec = pl.BlockSpec((bn, POOL_HALO, w), lambda b, j: (b, jnp.maximum(j * per - 1, 0), 0))
    else:
        prev_arr = hist
        prev_spec = pl.BlockSpec((None, bn, POOL_HALO, w), lambda b, j: (layer, b, 0, 0))
    w_bd, scale = wts['pool_w_bd'], wts['pool_scale']
    return pl.pallas_call(
        functools.partial(_pool_kernel, pos0=pos0, zero_first_prev=hist is None),
        grid=(n // bn, t // tm),
        in_specs=[prev_spec,
                  pl.BlockSpec((bn, tm, w), lambda b, j: (b, j, 0)),
                  _wspec(w_bd, layer), _wspec(scale, layer)],
        out_specs=pl.BlockSpec((bn, tm, w), lambda b, j: (b, j, 0)),
        out_shape=jax.ShapeDtypeStruct((n, t, w), F32),
        scratch_shapes=[pltpu.VMEM((bn, tm + POOL_HALO, w), F32)],
        compiler_params=_params("parallel", "parallel"),
        name="pool_mix",
    )(prev_arr, cur3, w_bd, scale)


def _softmax_pair(q, k, v, valid):
    m_rows = q.shape[0]
    lane = lax.broadcasted_iota(jnp.int32, (m_rows, ATT_PAIR), 1)
    head0 = lane < ATT_HD
    kb = k.astype(BF16)
    vb = v.astype(BF16)
    outs, lses = [], []
    for hs in range(2):
        hm = head0 if hs == 0 else jnp.logical_not(head0)
        qm = jnp.where(hm, q, 0.0).astype(BF16)
        s = lax.dot_general(qm, kb, NT, preferred_element_type=F32)
        s = jnp.where(valid, s, NEG_BIG)
        mx = jnp.max(s, axis=-1, keepdims=True)
        e = jnp.exp(s - mx)
        den = jnp.sum(e, axis=-1, keepdims=True)
        o = jnp.dot(e.astype(BF16), vb, preferred_element_type=F32) / den
        outs.append(o)
        lses.append(jnp.broadcast_to(mx + jnp.log(den), o.shape))
    return jnp.where(head0, outs[0], outs[1]), jnp.where(head0, lses[0], lses[1])


def _merge_groups(outs, lses):
    mx = functools.reduce(jnp.maximum, lses)
    es = [jnp.exp(l - mx) for l in lses]
    num = functools.reduce(lambda a, b: a + b, [e * o for e, o in zip(es, outs)])
    return num / functools.reduce(lambda a, b: a + b, es)


def _rows(start, size, stride):
    return pl.ds(start, size, stride=stride) if stride > 1 else pl.ds(start, size)


def _dil_attn_kernel(*refs):
    ng = len(DIL_GROUPS)
    y_ref, o_scr, l_scr = refs[5 * ng:]
    j = pl.program_id(1)
    tq = y_ref.shape[1]
    qi = lax.broadcasted_iota(jnp.int32, (BAND_BLK, 2 * BAND_BLK), 0)
    kj = lax.broadcasted_iota(jnp.int32, (BAND_BLK, 2 * BAND_BLK), 1)
    dist = BAND_BLK + qi - kj
    has_prev = kj + jnp.where(j > 0, BAND_BLK, 0) >= BAND_BLK
    for gi, (win, dil) in enumerate(DIL_GROUPS):
        band = (dist >= 0) & (dist <= win // dil)
        step = BAND_BLK * dil
        q_ref, k_ref, v_ref, kp_ref, vp_ref = refs[5 * gi:5 * gi + 5]
        for r in range(dil):
            for sb in range(tq // step):
                rows = _rows(r + sb * step, BAND_BLK, dil)
                if sb == 0:
                    prow = _rows(r, BAND_BLK, dil)
                    k_prev, v_prev = kp_ref[0, prow, :], vp_ref[0, prow, :]
                    valid = band & has_prev
                else:
                    prow = _rows(r + (sb - 1) * step, BAND_BLK, dil)
                    k_prev, v_prev = k_ref[0, prow, :], v_ref[0, prow, :]
                    valid = band
                o, lse = _softmax_pair(q_ref[0, rows, :],
                                       jnp.concatenate([k_prev, k_ref[0, rows, :]], axis=0),
                                       jnp.concatenate([v_prev, v_ref[0, rows, :]], axis=0), valid)
                o_scr[gi, rows, :] = o
                l_scr[gi, rows, :] = lse
    y_ref[0] = _merge_groups([o_scr[g] for g in range(ng)], [l_scr[g] for g in range(ng)])


def _dil_attn(q3, k3, v3):
    n, t, w = q3.shape
    tq = BAND_BLK * max(d for _, d in DIL_GROUPS)
    assert t % tq == 0
    specs, args = [], []
    for gi, (_, dil) in enumerate(DIL_GROUPS):
        per = tq // (BAND_BLK * dil)
        cur = pl.BlockSpec((1, tq, LANES), lambda b, j, gi=gi: (b, j, gi))
        prev = pl.BlockSpec((1, BAND_BLK * dil, LANES),
                            lambda b, j, per=per, gi=gi: (b, jnp.maximum(j * per - 1, 0), gi))
        specs += [cur, cur, cur, prev, prev]
        args += [q3, k3, v3, k3, v3]
    ng = len(DIL_GROUPS)
    return pl.pallas_call(
        _dil_attn_kernel,
        grid=(n, t // tq),
        in_specs=specs,
        out_specs=pl.BlockSpec((1, tq, LANES), lambda b, j: (b, j, 0)),
        out_shape=jax.ShapeDtypeStruct((n, t, LANES), F32),
        scratch_shapes=[pltpu.VMEM((ng, tq, LANES), F32), pltpu.VMEM((ng, tq, LANES), F32)],
        compiler_params=_params("parallel", "parallel"),
        name="dil_attn",
    )(*args)


def _sample_attn_kernel(q_ref, k_ref, v_ref, *rest):
    ng = len(DIL_GROUPS)
    cache_refs, y_ref, new_refs = rest[:ng], rest[2 * ng], rest[2 * ng + 1:]
    t = q_ref.shape[1]
    mq = 2 * SUBLANES
    zpad = jnp.zeros((mq - t, ATT_PAIR), F32)
    lane = lax.broadcasted_iota(jnp.int32, (mq, ATT_PAIR), 1)
    head0 = lane < ATT_HD
    tn = lax.broadcasted_iota(jnp.int32, (mq, mq), 0)
    jn = lax.broadcasted_iota(jnp.int32, (mq, mq), 1)
    outs, lses = [], []
    for gi, (win, dil) in enumerate(DIL_GROUPS):
        reach = dil * (win // dil)
        cols = slice(gi * ATT_PAIR, (gi + 1) * ATT_PAIR)
        ct = cache_refs[gi][0, 0]
        hist = ct.shape[1]
        kt2 = ct[:ATT_PAIR].astype(BF16)
        vt2 = ct[ATT_PAIR:].astype(BF16)
        k_new, v_new = k_ref[0, :, cols], v_ref[0, :, cols]
        q = jnp.concatenate([q_ref[0, :, cols], zpad], axis=0)
        kn = jnp.concatenate([k_new, zpad], axis=0).astype(BF16)
        vn = jnp.concatenate([v_new, zpad], axis=0).astype(BF16)
        ti = lax.broadcasted_iota(jnp.int32, (mq, hist), 0)
        pj = lax.broadcasted_iota(jnp.int32, (mq, hist), 1)
        d_old = hist + ti - pj
        ok_old = (d_old <= reach) & ((d_old & (dil - 1)) == 0)
        d_new = tn - jn
        ok_new = (d_new >= 0) & (d_new <= reach) & ((d_new & (dil - 1)) == 0) & (jn < t)
        o_h, l_h = [], []
        for hs in range(2):
            hm = head0 if hs == 0 else jnp.logical_not(head0)
            qm = jnp.where(hm, q, 0.0).astype(BF16)
            s_old = jnp.where(ok_old, jnp.dot(qm, kt2, preferred_element_type=F32), NEG_BIG)
            s_new = jnp.where(ok_new, lax.dot_general(qm, kn, NT, preferred_element_type=F32), NEG_BIG)
            mx = jnp.maximum(jnp.max(s_old, axis=-1, keepdims=True), jnp.max(s_new, axis=-1, keepdims=True))
            e_old = jnp.exp(s_old - mx)
            e_new = jnp.exp(s_new - mx)
            den = jnp.sum(e_old, axis=-1, keepdims=True) + jnp.sum(e_new, axis=-1, keepdims=True)
            num = (lax.dot_general(e_old.astype(BF16), vt2, NT, preferred_element_type=F32)
                   + jnp.dot(e_new.astype(BF16), vn, preferred_element_type=F32))
            o_h.append(num / den)
            l_h.append(jnp.broadcast_to(mx + jnp.log(den), num.shape))
        outs.append(jnp.where(head0, o_h[0], o_h[1])[:t])
        lses.append(jnp.where(head0, l_h[0], l_h[1])[:t])

        shifted = pltpu.roll(ct, hist - t, 1)
        fresh = jnp.concatenate([jnp.zeros((LANES - t, 2 * ATT_PAIR), F32),
                                 jnp.concatenate([k_new, v_new], axis=1)], axis=0).T
        lane_c = lax.broadcasted_iota(jnp.int32, (2 * ATT_PAIR, LANES), 1)
        if hist > LANES:
            new_refs[gi][0, 0, :, :hist - LANES] = shifted[:, :hist - LANES]
        new_refs[gi][0, 0, :, hist - LANES:] = jnp.where(lane_c >= LANES - t, fresh, shifted[:, hist - LANES:])
    y_ref[0] = _merge_groups(outs, lses)


def _sample_attn(q3, k3, v3, caches_t, accs, layer):
    n, t, w = q3.shape
    ng = len(DIL_GROUPS)
    for (win, dil), c in zip(DIL_GROUPS, caches_t):
        assert c.shape[3] == win and win % LANES == 0 and dil & (dil - 1) == 0 and t <= SUBLANES
    new = pl.BlockSpec((1, t, w), lambda b: (b, 0, 0))
    blk = lambda c: pl.BlockSpec((1, 1) + c.shape[2:], lambda b: (layer, b, 0, 0))
    res = pl.pallas_call(
        _sample_attn_kernel,
        grid=(n,),
        in_specs=[new, new, new] + [blk(c) for c in caches_t]
        + [pl.BlockSpec(memory_space=pl.ANY)] * ng,
        out_specs=[pl.BlockSpec((1, t, LANES), lambda b: (b, 0, 0))] + [blk(c) for c in accs],
        out_shape=[jax.ShapeDtypeStruct((n, t, LANES), F32)]
        + [jax.ShapeDtypeStruct(a.shape, F32) for a in accs],
        input_output_aliases={3 + ng + g: 1 + g for g in range(ng)},
        compiler_params=_params("parallel"),
        name="sample_attn",
    )(q3, k3, v3, *caches_t, *accs)
    return res[0], list(res[1:])


def _rwkv_prep_kernel(prev8_ref, sp_ref, cur_ref, mu_ref, wwa_ref, w0_ref, a0_ref, g2_ref,
                      kk_ref, ka_ref, rk_ref, ones_ref, tril_ref,
                      at_ref, bt_ref, kt_ref, rt_ref, v_ref, pl_ref, g_ref, bonus_ref,
                      sh_ref, p_ref, *, chunk):
    bn, tm, w = cur_ref.shape
    j = pl.program_id(1)
    cur = cur_ref[...]
    prev_row = jnp.where(j == 0, sp_ref[...], prev8_ref[:, SUBLANES - 1:SUBLANES, :])
    sh_ref[:, SUBLANES - 1:SUBLANES, :] = prev_row
    sh_ref[:, SUBLANES:, :] = cur
    prev = sh_ref[:, SUBLANES - 1:SUBLANES - 1 + tm, :]
    m = bn * tm
    zs = (cur + (prev - cur) * mu_ref[...]).reshape(m, w)

    r = zs[:, 0:RWKV_W]
    k = zs[:, RWKV_W:2 * RWKV_W]
    v = zs[:, 2 * RWKV_W:3 * RWKV_W]
    lo = 3 * RWKV_W
    z_wa = zs[:, lo:lo + LANES]
    z_g = zs[:, lo + LANES:lo + 2 * LANES]
    lane = lax.broadcasted_iota(jnp.int32, z_wa.shape, 1)
    u = jnp.where(lane < LANES // 2, jnp.tanh(z_wa), z_wa)
    lora = _bdot(u, wwa_ref[...])
    xw = w0_ref[...] + lora[:, :RWKV_W]
    w_log = -(jnp.maximum(-xw, 0.0) + jnp.log(1.0 + jnp.exp(-jnp.abs(xw)))) - 0.5
    e = jnp.exp(w_log)
    a = _sigmoid(a0_ref[...] + lora[:, RWKV_W:])
    g_ref[...] = _bdot(_sigmoid(z_g), g2_ref[...])

    ones_bd = ones_ref[...]
    kk = k * kk_ref[...]
    kk = kk * lax.rsqrt(jnp.maximum(_mm2_exact_rhs(kk * kk, ones_bd), 1e-24))
    k = k * (1.0 + (a - 1.0) * ka_ref[...])
    bonus_ref[...] = _mm2_exact_rhs(r * k * rk_ref[...], ones_bd) * v

    tril = tril_ref[...]
    e1 = e.astype(BF16)
    rem = e - e1.astype(F32)
    e2 = rem.astype(BF16)
    e3 = (rem - e2.astype(F32)).astype(BF16)
    f = lambda x: jnp.dot(tril, x, preferred_element_type=F32)
    cum = f(e1) + (f(e2) + f(e3))
    p_inc = jnp.exp(-cum)
    p_inv = jnp.exp(cum)

    def put(ref, val):
        ref[:, :tm, :] = val.reshape(bn, tm, RWKV_W)
        if ref.shape[1] > tm:
            ref[:, tm:, :] = jnp.zeros((bn, ref.shape[1] - tm, RWKV_W), F32)

    put(at_ref, -kk * jnp.exp(e - cum))
    put(bt_ref, kk * a * p_inv)
    put(kt_ref, k * p_inv)
    put(rt_ref, r * p_inc)
    put(v_ref, v)
    p_ref[...] = p_inc
    for c in range(m // chunk):
        pl_ref[c] = p_ref[(c + 1) * chunk - 1:(c + 1) * chunk, :]


def _rwkv_prep(zr3, shift_prev, state_layer, wts, layer, *, bn, tm, chunk, t_out):
    n, t, w = zr3.shape
    m = bn * tm
    per = tm // SUBLANES
    assert t_out == t or t == tm
    rowblk = pl.BlockSpec((m, RWKV_W), lambda b, j: (b * (t // tm) + j, 0))
    scanblk = pl.BlockSpec((bn, tm if t_out == t else t_out, RWKV_W), lambda b, j: (b, j, 0))
    tril = (jnp.arange(m)[:, None] >= jnp.arange(m)[None, :]) & \
           (jnp.arange(m)[:, None] // chunk == jnp.arange(m)[None, :] // chunk)
    nch = m // chunk
    names = ('rwkv_mu', 'rwkv_wwa', 'rwkv_w0', 'rwkv_a0', 'rwkv_g2', 'rwkv_k_k', 'rwkv_k_a', 'rwkv_r_k')
    consts = [wts[k] for k in names] + [wts['ones_bd'], tril.astype(BF16)]
    const_specs = [_wspec(wts[k], layer) for k in names] + [_wspec(wts['ones_bd']), _wspec(consts[-1])]
    outs = pl.pallas_call(
        functools.partial(_rwkv_prep_kernel, chunk=chunk),
        grid=(n // bn, t // tm),
        in_specs=[pl.BlockSpec((bn, SUBLANES, w), lambda b, j: (b, jnp.maximum(j * per - 1, 0), 0)),
                  pl.BlockSpec((None, bn, 1, w), lambda b, j: (state_layer, b, 0, 0)),
                  pl.BlockSpec((bn, tm, w), lambda b, j: (b, j, 0))]
        + const_specs,
        out_specs=[scanblk] * 5
        + [pl.BlockSpec((nch, 1, RWKV_W), lambda b, j: (b * (t // tm) + j, 0, 0)), rowblk, rowblk],
        out_shape=[jax.ShapeDtypeStruct((n, t_out, RWKV_W), F32)] * 5
        + [jax.ShapeDtypeStruct((n * t // chunk, 1, RWKV_W), F32)]
        + [jax.ShapeDtypeStruct((n * t, RWKV_W), F32)] * 2,
        scratch_shapes=[pltpu.VMEM((bn, tm + SUBLANES, w), F32), pltpu.VMEM((m, RWKV_W), F32)],
        compiler_params=_params("parallel", "parallel"),
        name="rwkv_prep",
    )(zr3, shift_prev, zr3, *consts)
    return outs


def _rwkv_scan_kernel(at_ref, bt_ref, kt_ref, rt_ref, v_ref, pl_ref, s0_ref, y_ref, sT_ref, s_ref):
    c = pl.program_id(1)
    chunk = RWKV_HD
    nb, nsub = at_ref.shape[0], at_ref.shape[1] // chunk
    npair = RWKV_W // LANES
    assert 2 * chunk == LANES

    @pl.when(c == 0)
    def _():
        s_ref[...] = s0_ref[...]

    lane = lax.broadcasted_iota(jnp.int32, (chunk, LANES), 1)
    head0 = lane < RWKV_HD
    row2 = lax.broadcasted_iota(jnp.int32, (chunk, 2 * LANES), 0)
    col2 = lax.broadcasted_iota(jnp.int32, (chunk, 2 * LANES), 1) & (chunk - 1)
    strict2 = col2 < row2
    incl2 = col2 <= row2
    row1 = lax.broadcasted_iota(jnp.int32, (chunk, LANES), 0)
    eye_ls = jnp.where((lane & (chunk - 1)) == row1, 1.0, 0.0).astype(F32)
    ri = lax.broadcasted_iota(jnp.int32, (LANES, LANES), 0)
    ci = lax.broadcasted_iota(jnp.int32, (LANES, LANES), 1)
    same = (ri < chunk) == (ci < chunk)
    eye = jnp.where(ri == ci, 1.0, 0.0).astype(F32)

    def stack(x):
        return jnp.concatenate([jnp.where(head0, x, 0.0), jnp.where(head0, 0.0, x)], axis=0)

    def bdiag(x):
        return jnp.where(same, jnp.concatenate([x, x], axis=0), 0.0)

    def bdiag_bf(x):
        return bdiag(x).astype(BF16)

    def mm(a, b, dims=NN):
        (ah, al), (bh, bl) = a, b
        lhs = jnp.concatenate([ah, ah, al], axis=0 if dims == TN else 1)
        rhs = jnp.concatenate([bh, bl, bh], axis=1 if dims == NT else 0)
        return lax.dot_general(lhs, rhs, dims, preferred_element_type=F32)

    jobs = [(sq, p, slice(p * LANES, (p + 1) * LANES), slice(ch * chunk, (ch + 1) * chunk), ch)
            for sq in range(nb) for p in range(npair) for ch in range(nsub)]
    at = [at_ref[sq, rs, cs] for sq, p, cs, rs, ch in jobs]
    rt = [rt_ref[sq, rs, cs] for sq, p, cs, rs, ch in jobs]
    vv = [v_ref[sq, rs, cs] for sq, p, cs, rs, ch in jobs]
    gram_a, gram_r = [], []
    for i, (sq, p, cs, rs, ch) in enumerate(jobs):
        bk = _split(jnp.concatenate([stack(bt_ref[sq, rs, cs]), stack(kt_ref[sq, rs, cs])], axis=0))
        gram_a.append(mm(_split(at[i]), bk, NT))
        gram_r.append(lax.dot_general(rt[i].astype(BF16), bk[0], NT, preferred_element_type=F32))
    m_ab = [jnp.where(strict2, g, 0.0)[:, :LANES] for g in gram_a]
    m_ak = [jnp.where(strict2, g, 0.0)[:, LANES:] for g in gram_a]
    m_r = [jnp.where(incl2, g, 0.0).astype(BF16) for g in gram_r]
    zero_bf = jnp.zeros((LANES, LANES), BF16)
    nlev = int(math.log2(chunk))
    npow = [jnp.dot(m.astype(BF16), bdiag_bf(m), preferred_element_type=F32) for m in m_ab]
    inv = [eye_ls + m for m in m_ab]
    for lev in range(1, nlev):
        for i in range(len(jobs)):
            if lev < nlev - 1:
                rhs = jnp.concatenate([bdiag_bf(npow[i]), bdiag_bf(inv[i])], axis=1)
                out = jnp.dot(npow[i].astype(BF16), rhs, preferred_element_type=F32)
                npow[i], inv[i] = out[:, :LANES], inv[i] + out[:, LANES:]
            else:
                inv[i] = inv[i] + jnp.dot(npow[i].astype(BF16), bdiag_bf(inv[i]), preferred_element_type=F32)
    mv = [mm(_split(m), _split(bdiag(x))) for m, x in zip(m_ak, vv)]
    wu = [mm(_split(t), _split(jnp.concatenate([bdiag(x), bdiag(a)], axis=1))) for t, x, a in zip(inv, mv, at)]
    qy = []
    for i in range(len(jobs)):
        w_m, u_m = wu[i][:, :LANES], wu[i][:, LANES:]
        zq = jnp.concatenate([jnp.concatenate([bdiag_bf(u_m), bdiag_bf(w_m)], axis=1),
                              jnp.concatenate([zero_bf, bdiag_bf(vv[i])], axis=1)], axis=0)
        qy.append(jnp.dot(m_r[i], zq, preferred_element_type=F32))
    ad = []
    for i, (sq, p, cs, rs, ch) in enumerate(jobs):
        decay = pl_ref[sq, ch, :, cs]
        bk = jnp.concatenate([bt_ref[sq, rs, cs], kt_ref[sq, rs, cs]], axis=0) * decay
        z = jnp.concatenate([jnp.concatenate([wu[i][:, LANES:], wu[i][:, :LANES]], axis=1),
                             jnp.concatenate([jnp.zeros_like(vv[i]), vv[i]], axis=1)], axis=0)
        ad.append(mm(_split(bk), _split(z), TN))
    for i, (sq, p, cs, rs, ch) in enumerate(jobs):
        if ch == 0:
            state = s_ref[sq, p]
        a_t = eye * pl_ref[sq, ch, :, cs] + jnp.where(same, ad[i][:, :LANES], 0.0)
        d_t = jnp.where(same, ad[i][:, LANES:], 0.0)
        st = _split(state)
        y_ref[sq, rs, cs] = jnp.dot((rt[i] + qy[i][:, :LANES]).astype(BF16), st[0],
                                    preferred_element_type=F32) + qy[i][:, LANES:]
        state = mm(_split(a_t), st) + d_t
        if ch == nsub - 1:
            s_ref[sq, p] = state

    @pl.when(c == pl.num_programs(1) - 1)
    def _():
        sT_ref[...] = s_ref[...]


def _rwkv_scan(at, bt, kt, rt, v, plast, s0, state_layer, *, n, t, chunk, nb):
    assert chunk == RWKV_HD and at.shape == (n, t, RWKV_W)
    npair = RWKV_W // LANES
    nch = t // chunk
    nsub = 2 if nch % 2 == 0 else 1
    tok = pl.BlockSpec((nb, nsub * chunk, RWKV_W), lambda b, c: (b, c, 0))
    st = pl.BlockSpec((nb, npair, LANES, LANES), lambda b, c: (b, 0, 0, 0))
    st_in = pl.BlockSpec((None, nb, npair, LANES, LANES), lambda b, c: (state_layer, b, 0, 0, 0))
    y, s_t = pl.pallas_call(
        _rwkv_scan_kernel,
        grid=(n // nb, nch // nsub),
        in_specs=[tok] * 5 + [pl.BlockSpec((nb, nsub, 1, RWKV_W), lambda b, c: (b, c, 0, 0)), st_in],
        out_specs=[tok, st],
        out_shape=[jax.ShapeDtypeStruct((n, t, RWKV_W), F32),
                   jax.ShapeDtypeStruct((n, npair, LANES, LANES), F32)],
        scratch_shapes=[pltpu.VMEM((nb, npair, LANES, LANES), F32)],
        compiler_params=_params("parallel", "arbitrary"),
        name="rwkv_scan",
    )(at, bt, kt, rt, v, plast.reshape(n, nch, 1, RWKV_W), s0)
    return y, s_t


def _merge_kernel(x_ref, yp_ref, ya_ref, ys_ref, bonus_ref, g_ref, gate_ref,
                  pp_ref, pa_ref, pr_ref, wo_ref, lng_ref, lnb_ref, mean_ref, out_ref):
    d = x_ref.shape[1]
    y_attn = ya_ref[...]

    ys = ys_ref[...]
    mean_bd = mean_ref[...]
    mu = _mm2_exact_rhs(ys, mean_bd)
    dev = ys - mu
    var = _mm2_exact_rhs(dev * dev, mean_bd)
    yn = dev * lax.rsqrt(var + RWKV_LN_EPS) * lng_ref[...] + lnb_ref[...]
    y_rwkv = (yn + bonus_ref[...]) * g_ref[...]

    gate = lambda c: gate_ref[:, c * d:(c + 1) * d].astype(F32)
    merged = (gate(0) * _bdot(yp_ref[...], pp_ref[...])
              + gate(1) * _bdot(y_attn, pa_ref[...])
              + gate(2) * _bdot(y_rwkv, pr_ref[...]))
    out_ref[...] = x_ref[...] + _bdot(merged, wo_ref[...])


def _merge(x2d, yp, ya, ys, bonus, g, gates, wts, layer, tm):
    m, d = x2d.shape
    row = lambda w: pl.BlockSpec((tm, w), lambda i: (i, 0))
    names = ('proj_pool', 'proj_attn', 'proj_rwkv', 'w_out', 'rwkv_ln_g', 'rwkv_ln_b')
    consts = [wts[k] for k in names] + [wts['mean_bd']]
    return pl.pallas_call(
        _merge_kernel,
        grid=(m // tm,),
        in_specs=[row(d), row(POOL_W), row(LANES)] + [row(RWKV_W)] * 3 + [row(3 * d)]
        + [_wspec(wts[k], layer) for k in names] + [_wspec(wts['mean_bd'])],
        out_specs=row(d),
        out_shape=jax.ShapeDtypeStruct((m, d), F32),
        compiler_params=_params("parallel"),
        name="merge_out_proj",
    )(x2d, yp, ya, ys, bonus, g, gates, *consts)


MXU_TILE = 256


def _ffn_ple_kernel(x_ref, p_ref, gf_ref, g1_ref, w1_ref, w3_ref, w2_ref, g2_ref, wg_ref, wp_ref,
                    out_ref, *, final_norm):
    x = x_ref[...]
    h = _rms(x, g1_ref[...]).astype(BF16)
    hidden = w1_ref.shape[1]
    cut = pl.cdiv(hidden // MXU_TILE, 2) * MXU_TILE
    y = x
    for cs in (slice(0, cut), slice(cut, hidden)):
        h1 = jnp.dot(h, w1_ref[:, cs], preferred_element_type=F32)
        h3 = jnp.dot(h, w3_ref[:, cs], preferred_element_type=F32)
        y = y + _bdot(h1 * _sigmoid(h1) * h3, w2_ref[cs, :])
    gate = _sigmoid(_bdot(_rms(y, g2_ref[...]), wg_ref[...]))
    y = y + _bdot(p_ref[...], wp_ref[...]) * gate
    if final_norm:
        y = _rms(y, gf_ref[...])
    out_ref[...] = y


def _ffn_ple(x2d, p_all, wts, layer, g_final, tm, final_norm):
    m, d = x2d.shape
    row = pl.BlockSpec((tm, d), lambda i: (i, 0))
    names = ('norm_ffn_g', 'ffn_w1', 'ffn_w3', 'ffn_w2', 'norm_ple_g', 'ple_gate', 'ple_proj')
    return pl.pallas_call(
        functools.partial(_ffn_ple_kernel, final_norm=final_norm),
        grid=(m // tm,),
        in_specs=[row, pl.BlockSpec((None, tm, p_all.shape[2]), lambda i: (layer, i, 0)), _wspec(g_final)]
        + [_wspec(wts[k], layer) for k in names],
        out_specs=row,
        out_shape=jax.ShapeDtypeStruct((m, d), F32),
        compiler_params=_params("parallel"),
        name="swiglu_ple_final" if final_norm else "swiglu_ple",
    )(x2d, p_all, g_final, *[wts[k] for k in names])


def _rope_tables(pos, reps):
    half = ATT_HD // 2
    inv = ROPE_THETA ** (-2.0 * jnp.arange(half, dtype=F32) / ATT_HD)
    ang = pos.astype(F32)[:, None] * inv[None, :]
    cos = jnp.cos(ang)
    sin = jnp.sin(ang)
    cos = jnp.tile(jnp.concatenate([cos, cos], axis=1), (reps, LANES // ATT_HD))
    sin = jnp.tile(jnp.concatenate([-sin, sin], axis=1), (reps, LANES // ATT_HD))
    return cos, sin


def _block_diag(blocks):
    g, a, b = blocks.shape[-3:]
    rows = []
    for i in range(g):
        parts = [blocks[..., i, :, :] if j == i else jnp.zeros_like(blocks[..., i, :, :]) for j in range(g)]
        rows.append(jnp.concatenate(parts, axis=-1))
    return jnp.concatenate(rows, axis=-2)


def _stacked_weights(w):
    row = lambda a: a.reshape(a.shape[0], 1, -1)
    bf = lambda a: a.astype(BF16)
    heads = RWKV_W // RWKV_HD
    zero = jnp.zeros_like(w['rwkv_w2'])
    wwa = jnp.concatenate([jnp.concatenate([w['rwkv_w2'], zero], axis=2),
                           jnp.concatenate([zero, w['rwkv_a2']], axis=2)], axis=1)
    ones = jnp.ones((heads, RWKV_HD, RWKV_HD), F32)
    return {
        'norm_mix_g': row(w['norm_mix_g']), 'w_in': bf(w['w_in']),
        'pool_w_bd': bf(_block_diag(w['pool_w_grp'])), 'pool_scale': row(w['pool_scale']),
        'rwkv_mu': row(w['rwkv_mu']), 'rwkv_wwa': bf(wwa),
        'rwkv_w0': row(w['rwkv_w0']), 'rwkv_a0': row(w['rwkv_a0']), 'rwkv_g2': bf(w['rwkv_g2']),
        'rwkv_k_k': row(w['rwkv_k_k']), 'rwkv_k_a': row(w['rwkv_k_a']), 'rwkv_r_k': row(w['rwkv_r_k']),
        'rwkv_ln_g': row(w['rwkv_ln_g']), 'rwkv_ln_b': row(w['rwkv_ln_b']),
        'ones_bd': bf(_block_diag(ones)), 'mean_bd': bf(_block_diag(ones / RWKV_HD)),
        'proj_pool': bf(w['proj_pool']), 'proj_attn': bf(w['proj_attn']),
        'proj_rwkv': bf(w['proj_rwkv']), 'w_out': bf(w['w_out']),
        'norm_ffn_g': row(w['norm_ffn_g']), 'ffn_w1': bf(w['ffn_w1']), 'ffn_w3': bf(w['ffn_w3']),
        'ffn_w2': bf(w['ffn_w2']), 'norm_ple_g': row(w['norm_ple_g']),
        'ple_proj': bf(w['ple_proj']), 'ple_gate': bf(w['ple_gate']),
    }


def _token_tiles(m):
    tm = next(c for c in (512, 256, 128, m) if m % c == 0)
    return tm, tm


def _pair_states(s):
    st = jnp.swapaxes(s, -1, -2)
    st = st.reshape(st.shape[:-3] + (st.shape[-3] // 2, 2) + st.shape[-2:])
    return _block_diag(st)


def _unpair_states(s2):
    half = s2.shape[-1] // 2
    s = jnp.stack([s2[..., :half, :half], s2[..., half:, half:]], axis=-3)
    s = s.reshape(s.shape[:-4] + (2 * s.shape[-4],) + s.shape[-2:])
    return jnp.swapaxes(s, -1, -2)


def _decoder_layer(x2d, n, t, p_all, wts, layer, g_final, final_norm, tabs, state, *, prompt, tm, tm_in, chunk):
    sl = state['layer']
    zp, q, k, v, zr, gates = _in_proj(x2d, wts, layer, tabs[0], tabs[1], tm_in)
    three = lambda a: a.reshape(n, t, a.shape[-1])
    zp3, q3, k3, v3, zr3 = three(zp), three(q), three(k), three(v), three(zr)

    if prompt:
        yp = _pool(None, zp3, wts, layer, bn=1, tm=tm, pos0=0)
        new_pool = zp3[:, t - POOL_HIST:]
    else:
        yp = _pool(state['pool16'], zp3, wts, layer, bn=n, tm=t, pos0=state['pos0'])
        new_pool = jnp.concatenate([state['pool16'][sl], zp3], axis=1)[:, -POOL_HIST:]

    if prompt:
        ya = _dil_attn(q3, k3, v3)
        new_kv = []
        for gi, (win, dil) in enumerate(DIL_GROUPS):
            cols = slice(gi * ATT_PAIR, (gi + 1) * ATT_PAIR)
            keep = min(win, t)
            kv = jnp.concatenate([k3[:, t - keep:, cols], v3[:, t - keep:, cols]], axis=-1)
            new_kv.append(kv.reshape(n, keep, 2, 2, ATT_HD))
    else:
        ya, new_kv = _sample_attn(q3, k3, v3, state['kv_t'], state['kv_acc'], sl)

    if prompt:
        at, bt, kt, rt, vv, plast, g, bonus = _rwkv_prep(zr3, state['shift'], sl, wts, layer,
                                                         bn=1, tm=tm, chunk=chunk, t_out=t)
        ys, s_t = _rwkv_scan(at, bt, kt, rt, vv, plast, state['wkv'], sl, n=n, t=t, chunk=chunk, nb=n)
    else:
        at, bt, kt, rt, vv, plast, g, bonus = _rwkv_prep(zr3, state['shift'], sl, wts, layer,
                                                         bn=n, tm=t, chunk=t, t_out=chunk)
        ys, s_t = _rwkv_scan(at, bt, kt, rt, vv, plast, state['wkv'], sl,
                             n=n, t=chunk, chunk=chunk, nb=4 if n % 4 == 0 else 1)
        ys = ys[:, :t]
    new_shift = zr3[:, t - 1]

    x2d = _merge(x2d, yp.reshape(n * t, POOL_W), ya.reshape(n * t, LANES), ys.reshape(n * t, RWKV_W),
                 bonus, g, gates, wts, layer, tm)
    x2d = _ffn_ple(x2d, p_all, wts, layer, g_final, tm, final_norm)
    return x2d, (new_pool, new_shift, s_t, new_kv)


def kernel(x_prompt, x_sample, state_pool, state_shift, state_wkv, cache_kv_w128, cache_kv_w512,
           cache_kv_w2048, p_prompt, p_sample, norm_mix_g, w_in, pool_w_grp, pool_scale, rwkv_mu,
           rwkv_w0, rwkv_w2, rwkv_a0, rwkv_a2, rwkv_g2, rwkv_k_k, rwkv_k_a, rwkv_r_k, rwkv_ln_g,
           rwkv_ln_b, proj_pool, proj_attn, proj_rwkv, w_out, norm_ffn_g, ffn_w1, ffn_w3, ffn_w2,
           norm_ple_g, ple_proj, ple_gate, norm_final_g):
    weights = dict(norm_mix_g=norm_mix_g, w_in=w_in, pool_w_grp=pool_w_grp, pool_scale=pool_scale,
                   rwkv_mu=rwkv_mu, rwkv_w0=rwkv_w0, rwkv_w2=rwkv_w2, rwkv_a0=rwkv_a0, rwkv_a2=rwkv_a2,
                   rwkv_g2=rwkv_g2, rwkv_k_k=rwkv_k_k, rwkv_k_a=rwkv_k_a, rwkv_r_k=rwkv_r_k,
                   rwkv_ln_g=rwkv_ln_g, rwkv_ln_b=rwkv_ln_b, proj_pool=proj_pool, proj_attn=proj_attn,
                   proj_rwkv=proj_rwkv, w_out=w_out, norm_ffn_g=norm_ffn_g, ffn_w1=ffn_w1, ffn_w3=ffn_w3,
                   ffn_w2=ffn_w2, norm_ple_g=norm_ple_g, ple_proj=ple_proj, ple_gate=ple_gate)
    depth = w_in.shape[0]
    np_, tp, d = x_prompt.shape
    ns, ts, _ = x_sample.shape
    caches = (cache_kv_w128, cache_kv_w512, cache_kv_w2048)
    past_len = PAST_LEN
    tm_p, tm_in_p = _token_tiles(np_ * tp)
    tm_s = ns * ts
    chunk = 64
    tabs_p = _rope_tables(jnp.arange(tp), 1)
    tabs_s = _rope_tables(past_len + jnp.arange(ts), ns)
    g_final = norm_final_g[None, :]
    wts = _stacked_weights(weights)

    xp = x_prompt.reshape(np_ * tp, d)
    xs = x_sample.reshape(ns * ts, d)
    pp_all = p_prompt.reshape(depth, np_ * tp, -1)
    ps_all = p_sample.reshape(depth, ns * ts, -1)
    heads = RWKV_W // RWKV_HD
    state_p = {
        'layer': 0,
        'shift': jnp.zeros((1, np_, 1, RWKV_PROJ), F32),
        'wkv': jnp.zeros((1, np_, heads // 2, 2 * RWKV_HD, 2 * RWKV_HD), F32),
    }
    caches_t = [jnp.transpose(c, (0, 1, 3, 4, 5, 2)).reshape(c.shape[0], c.shape[1], -1, c.shape[2])
                for c in caches]
    kv_acc = [jnp.zeros(c.shape, F32) for c in caches_t]
    state_s = {
        'pool16': jnp.pad(state_pool, ((0, 0), (0, 0), (POOL_HALO - POOL_HIST, 0), (0, 0))),
        'shift': state_shift[:, :, None, :], 'wkv': _pair_states(state_wkv),
        'kv_t': caches_t, 'pos0': past_len,
    }
    outs_p, outs_s = [], []
    for i in range(depth):
        last = i == depth - 1
        xp, st_p = _decoder_layer(xp, np_, tp, pp_all, wts, i, g_final, last, tabs_p, state_p,
                                  prompt=True, tm=tm_p, tm_in=tm_in_p, chunk=chunk)
        xs, st_s = _decoder_layer(xs, ns, ts, ps_all, wts, i, g_final, last, tabs_s,
                                  dict(state_s, layer=i, kv_acc=kv_acc),
                                  prompt=False, tm=tm_s, tm_in=tm_s, chunk=chunk)
        kv_acc = st_s[3]
        outs_p.append(st_p)
        outs_s.append(st_s)

    stack = lambda outs, f: jnp.stack([f(o) for o in outs])
    res = [xp.reshape(np_, tp, d), xs.reshape(ns, ts, d)]
    for idx in range(2):
        res.append(stack(outs_p, lambda o: o[idx]))
        res.append(stack(outs_s, lambda o: o[idx]))
    res.append(_unpair_states(stack(outs_p, lambda o: o[2])))
    res.append(_unpair_states(stack(outs_s, lambda o: o[2])))
    for gi in range(len(DIL_GROUPS)):
        res.append(stack(outs_p, lambda o: o[3][gi]))
        a = kv_acc[gi]
        a = a.reshape(a.shape[0], a.shape[1], 2, 2, ATT_HD, a.shape[3])
        res.append(jnp.transpose(a, (0, 1, 5, 2, 3, 4)))
    return tuple(res)
```

```python
import functools
import math

import jax
import jax.numpy as jnp
from jax import lax
from jax.experimental import pallas as pl
from jax.experimental.pallas import tpu as pltpu

F32 = jnp.float32
BF16 = jnp.bfloat16

POOL_GC = 64
POOL_W = 256
POOL_WINDOWS = (2, 4, 8, 16)
POOL_HIST = 15
POOL_HALO = 16
ATT_HD = 64
ATT_W = 384
ATT_PAIR = 128
DIL_GROUPS = ((128, 1), (512, 4), (2048, 16))
BAND_BLK = 128
ROPE_THETA = 10000.0
RWKV_HD = 64
RWKV_W = 384
RWKV_PROJ = 1408
RWKV_LN_EPS = 64e-5
RMS_EPS = 1e-6
PAST_LEN = 8192
NEG_BIG = -1e30

LANES = 128
SUBLANES = 8
MXU_TILE = 256
VMEM_LIMIT = 56 * 1024 * 1024

NN = (((1,), (0,)), ((), ()))
NT = (((1,), (1,)), ((), ()))
TN = (((0,), (0,)), ((), ()))


def _params(*sem):
    return pltpu.CompilerParams(dimension_semantics=sem, vmem_limit_bytes=VMEM_LIMIT)


def _wspec(arr, layer=None):
    if layer is None:
        nd = arr.ndim
        return pl.BlockSpec(arr.shape, lambda *_: (0,) * nd, pipeline_mode=pl.Buffered(1))
    nd = arr.ndim - 1
    return pl.BlockSpec((None,) + arr.shape[1:], lambda *_: (layer,) + (0,) * nd,
                        pipeline_mode=pl.Buffered(1))


def _bdot(a, b, dims=NN):
    return lax.dot_general(a.astype(BF16), b.astype(BF16), dims, preferred_element_type=F32)


def _split(a):
    hi = a.astype(BF16)
    lo = (a - hi.astype(F32)).astype(BF16)
    return hi, lo


def _mm3(a, b, dims=NN):
    ah, al = _split(a)
    bh, bl = _split(b)
    f = lambda x, y: lax.dot_general(x, y, dims, preferred_element_type=F32)
    return f(ah, bh) + (f(ah, bl) + f(al, bh))


def _mm2_exact_rhs(a, b_bf16, dims=NN):
    ah, al = _split(a)
    f = lambda x: lax.dot_general(x, b_bf16, dims, preferred_element_type=F32)
    return f(ah) + f(al)


def _head_sums(x, bd_ref):
    w = x.shape[1]
    cuts = [c for c in range(0, w, MXU_TILE)] + [w]
    parts = [_mm2_exact_rhs(x[:, a:b], bd_ref[a:b, a:b]) for a, b in zip(cuts[:-1], cuts[1:])]
    return jnp.concatenate(parts, axis=1)


def _rms(x, g):
    return x * lax.rsqrt(jnp.mean(x * x, axis=-1, keepdims=True) + RMS_EPS) * g


def _sigmoid(x):
    return 1.0 / (1.0 + jnp.exp(-x))


def _in_proj_kernel(x_ref, g_ref, w_ref, cos_ref, sin_ref,
                    pool_ref, q_ref, k_ref, v_ref, rw_ref, gate_ref):
    h = _rms(x_ref[...], g_ref[...]).astype(BF16)

    def seg(a, b):
        return jnp.dot(h, w_ref[:, a:b], preferred_element_type=F32)

    cos = cos_ref[...]
    sin = sin_ref[...]
    lane = lax.broadcasted_iota(jnp.int32, cos.shape, 1)
    low_half = (lane & 32) == 0

    def rope(t):
        partner = jnp.where(low_half, pltpu.roll(t, 96, 1), pltpu.roll(t, 32, 1))
        return t * cos + partner * sin

    n_mix = POOL_W + 3 * ATT_W + RWKV_PROJ
    mix = seg(0, n_mix)
    o = 0
    pool_ref[...] = mix[:, o:o + POOL_W]
    o += POOL_W
    for c in range(ATT_W // LANES):
        cs = slice(c * LANES, (c + 1) * LANES)
        q_ref[:, cs] = rope(mix[:, o + c * LANES:o + (c + 1) * LANES]) * (ATT_HD ** -0.5)
        k_ref[:, cs] = rope(mix[:, o + ATT_W + c * LANES:o + ATT_W + (c + 1) * LANES])
    o += 2 * ATT_W
    v_ref[...] = mix[:, o:o + ATT_W]
    o += ATT_W
    rw_ref[...] = mix[:, o:o + RWKV_PROJ]
    gate_ref[...] = _sigmoid(seg(n_mix, w_ref.shape[1])).astype(gate_ref.dtype)


def _in_proj(x2d, wts, layer, cos, sin, tm):
    m, d = x2d.shape
    g, w_bf = wts['norm_mix_g'], wts['w_in']
    ncol = w_bf.shape[-1]
    ngate = ncol - (POOL_W + 3 * ATT_W + RWKV_PROJ)
    ntab = cos.shape[0] // tm
    row = lambda w: pl.BlockSpec((tm, w), lambda i: (i, 0))
    tab = pl.BlockSpec((tm, LANES), lambda i: (i % ntab, 0))
    widths = (POOL_W, ATT_W, ATT_W, ATT_W, RWKV_PROJ, ngate)
    dtypes = (F32,) * 5 + (BF16,)
    return pl.pallas_call(
        _in_proj_kernel,
        grid=(m // tm,),
        in_specs=[row(d), _wspec(g, layer), _wspec(w_bf, layer), tab, tab],
        out_specs=[row(w) for w in widths],
        out_shape=[jax.ShapeDtypeStruct((m, w), dt) for w, dt in zip(widths, dtypes)],
        compiler_params=_params("parallel"),
        name="in_proj",
    )(x2d, g, w_bf, cos, sin)


def _pool_kernel(prev_ref, cur_ref, w_ref, scale_ref, y_ref, ext_ref, *, pos0, zero_first_prev):
    bn, tm, w = cur_ref.shape
    j = pl.program_id(1)
    prev = prev_ref[...]
    if zero_first_prev:
        prev = jnp.where(j == 0, 0.0, prev)
    cur = cur_ref[...]
    ext_ref[:, :POOL_HALO, :] = prev
    ext_ref[:, POOL_HALO:, :] = cur

    lane = lax.broadcasted_iota(jnp.int32, (bn, tm, w), 2)
    group = jnp.right_shift(lane, POOL_GC.bit_length() - 1)
    acc = cur
    win = jnp.zeros_like(cur)
    for s in range(1, POOL_WINDOWS[-1] + 1):
        if s in POOL_WINDOWS:
            win = jnp.where(group == POOL_WINDOWS.index(s), acc, win)
        if s < POOL_WINDOWS[-1]:
            acc = acc + ext_ref[:, POOL_HALO - s:POOL_HALO - s + tm, :]
    width = jnp.left_shift(2, group)
    pos = pos0 + j * tm + lax.broadcasted_iota(jnp.int32, (bn, tm, w), 1)
    cnt = jnp.minimum(pos + 1, width).astype(F32)
    dlt = (win / cnt - cur).reshape(bn * tm, w)
    y = _bdot(dlt, w_ref[...]) * scale_ref[...]
    y_ref[...] = y.reshape(bn, tm, w)


def _pool(hist, cur3, wts, layer, *, bn, tm, pos0):
    n, t, w = cur3.shape
    if hist is None:
        per = tm // POOL_HALO
        prev_arr = cur3
        prev_spec = pl.BlockSpec((bn, POOL_HALO, w), lambda b, j: (b, jnp.maximum(j * per - 1, 0), 0))
    else:
        prev_arr = hist
        prev_spec = pl.BlockSpec((None, bn, POOL_HALO, w), lambda b, j: (layer, b, 0, 0))
    w_bd, scale = wts['pool_w_bd'], wts['pool_scale']
    return pl.pallas_call(
        functools.partial(_pool_kernel, pos0=pos0, zero_first_prev=hist is None),
        grid=(n // bn, t // tm),
        in_specs=[prev_spec,
                  pl.BlockSpec((bn, tm, w), lambda b, j: (b, j, 0)),
                  _wspec(w_bd, layer), _wspec(scale, layer)],
        out_specs=pl.BlockSpec((bn, tm, w), lambda b, j: (b, j, 0)),
        out_shape=jax.ShapeDtypeStruct((n, t, w), F32),
        scratch_shapes=[pltpu.VMEM((bn, tm + POOL_HALO, w), F32)],
        compiler_params=_params("parallel", "parallel"),
        name="pool_mix",
    )(prev_arr, cur3, w_bd, scale)


def _softmax_pair(q, k, v, valid):
    m_rows = q.shape[0]
    lane = lax.broadcasted_iota(jnp.int32, (m_rows, ATT_PAIR), 1)
    head0 = lane < ATT_HD
    kb = k.astype(BF16)
    vb = v.astype(BF16)
    outs, lses = [], []
    for hs in range(2):
        hm = head0 if hs == 0 else jnp.logical_not(head0)
        qm = jnp.where(hm, q, 0.0).astype(BF16)
        s = lax.dot_general(qm, kb, NT, preferred_element_type=F32)
        s = jnp.where(valid, s, NEG_BIG)
        mx = jnp.max(s, axis=-1, keepdims=True)
        e = jnp.exp(s - mx)
        den = jnp.sum(e, axis=-1, keepdims=True)
        o = jnp.dot(e.astype(BF16), vb, preferred_element_type=F32) / den
        outs.append(o)
        lses.append(jnp.broadcast_to(mx + jnp.log(den), o.shape))
    return jnp.where(head0, outs[0], outs[1]), jnp.where(head0, lses[0], lses[1])


def _merge_groups(outs, lses):
    mx = functools.reduce(jnp.maximum, lses)
    es = [jnp.exp(l - mx) for l in lses]
    num = functools.reduce(lambda a, b: a + b, [e * o for e, o in zip(es, outs)])
    return num / functools.reduce(lambda a, b: a + b, es)


def _rows(start, size, stride):
    return pl.ds(start, size, stride=stride) if stride > 1 else pl.ds(start, size)


def _dil_attn_kernel(*refs):
    ng = len(DIL_GROUPS)
    y_ref, o_scr, l_scr = refs[5 * ng:]
    j = pl.program_id(1)
    tq = y_ref.shape[1]
    qi = lax.broadcasted_iota(jnp.int32, (BAND_BLK, 2 * BAND_BLK), 0)
    kj = lax.broadcasted_iota(jnp.int32, (BAND_BLK, 2 * BAND_BLK), 1)
    dist = BAND_BLK + qi - kj
    has_prev = kj + jnp.where(j > 0, BAND_BLK, 0) >= BAND_BLK
    for gi, (win, dil) in enumerate(DIL_GROUPS):
        band = (dist >= 0) & (dist <= win // dil)
        step = BAND_BLK * dil
        q_ref, k_ref, v_ref, kp_ref, vp_ref = refs[5 * gi:5 * gi + 5]
        for r in range(dil):
            for sb in range(tq // step):
                rows = _rows(r + sb * step, BAND_BLK, dil)
                if sb == 0:
                    prow = _rows(r, BAND_BLK, dil)
                    k_prev, v_prev = kp_ref[0, prow, :], vp_ref[0, prow, :]
                    valid = band & has_prev
                else:
                    prow = _rows(r + (sb - 1) * step, BAND_BLK, dil)
                    k_prev, v_prev = k_ref[0, prow, :], v_ref[0, prow, :]
                    valid = band
                o, lse = _softmax_pair(q_ref[0, rows, :],
                                       jnp.concatenate([k_prev, k_ref[0, rows, :]], axis=0),
                                       jnp.concatenate([v_prev, v_ref[0, rows, :]], axis=0), valid)
                o_scr[gi, rows, :] = o
                l_scr[gi, rows, :] = lse
    y_ref[0] = _merge_groups([o_scr[g] for g in range(ng)], [l_scr[g] for g in range(ng)])


def _dil_attn(q3, k3, v3):
    n, t, w = q3.shape
    tq = BAND_BLK * max(d for _, d in DIL_GROUPS)
    assert t % tq == 0
    specs, args = [], []
    for gi, (_, dil) in enumerate(DIL_GROUPS):
        per = tq // (BAND_BLK * dil)
        cur = pl.BlockSpec((1, tq, LANES), lambda b, j, gi=gi: (b, j, gi))
        prev = pl.BlockSpec((1, BAND_BLK * dil, LANES),
                            lambda b, j, per=per, gi=gi: (b, jnp.maximum(j * per - 1, 0), gi))
        specs += [cur, cur, cur, prev, prev]
        args += [q3, k3, v3, k3, v3]
    ng = len(DIL_GROUPS)
    return pl.pallas_call(
        _dil_attn_kernel,
        grid=(n, t // tq),
        in_specs=specs,
        out_specs=pl.BlockSpec((1, tq, LANES), lambda b, j: (b, j, 0)),
        out_shape=jax.ShapeDtypeStruct((n, t, LANES), F32),
        scratch_shapes=[pltpu.VMEM((ng, tq, LANES), F32), pltpu.VMEM((ng, tq, LANES), F32)],
        compiler_params=_params("parallel", "parallel"),
        name="dil_attn",
    )(*args)


def _sample_attn_kernel(q_ref, k_ref, v_ref, *rest):
    ng = len(DIL_GROUPS)
    cache_refs, y_ref, new_refs = rest[:ng], rest[2 * ng], rest[2 * ng + 1:]
    t = q_ref.shape[1]
    mq = 2 * SUBLANES
    zpad = jnp.zeros((mq - t, ATT_PAIR), F32)
    lane = lax.broadcasted_iota(jnp.int32, (mq, ATT_PAIR), 1)
    head0 = lane < ATT_HD
    tn = lax.broadcasted_iota(jnp.int32, (mq, mq), 0)
    jn = lax.broadcasted_iota(jnp.int32, (mq, mq), 1)
    outs, lses = [], []
    for gi, (win, dil) in enumerate(DIL_GROUPS):
        reach = dil * (win // dil)
        cols = slice(gi * ATT_PAIR, (gi + 1) * ATT_PAIR)
        ct = cache_refs[gi][0, 0]
        hist = ct.shape[1]
        kt2 = ct[:ATT_PAIR].astype(BF16)
        vt2 = ct[ATT_PAIR:].astype(BF16)
        k_new, v_new = k_ref[0, :, cols], v_ref[0, :, cols]
        q = jnp.concatenate([q_ref[0, :, cols], zpad], axis=0)
        kn = jnp.concatenate([k_new, zpad], axis=0).astype(BF16)
        vn = jnp.concatenate([v_new, zpad], axis=0).astype(BF16)
        ti = lax.broadcasted_iota(jnp.int32, (mq, hist), 0)
        pj = lax.broadcasted_iota(jnp.int32, (mq, hist), 1)
        d_old = hist + ti - pj
        ok_old = (d_old <= reach) & ((d_old & (dil - 1)) == 0)
        d_new = tn - jn
        ok_new = (d_new >= 0) & (d_new <= reach) & ((d_new & (dil - 1)) == 0) & (jn < t)
        o_h, l_h = [], []
        for hs in range(2):
            hm = head0 if hs == 0 else jnp.logical_not(head0)
            qm = jnp.where(hm, q, 0.0).astype(BF16)
            s_old = jnp.where(ok_old, jnp.dot(qm, kt2, preferred_element_type=F32), NEG_BIG)
            s_new = jnp.where(ok_new, lax.dot_general(qm, kn, NT, preferred_element_type=F32), NEG_BIG)
            mx = jnp.maximum(jnp.max(s_old, axis=-1, keepdims=True), jnp.max(s_new, axis=-1, keepdims=True))
            e_old = jnp.exp(s_old - mx)
            e_new = jnp.exp(s_new - mx)
            den = jnp.sum(e_old, axis=-1, keepdims=True) + jnp.sum(e_new, axis=-1, keepdims=True)
            num = (lax.dot_general(e_old.astype(BF16), vt2, NT, preferred_element_type=F32)
                   + jnp.dot(e_new.astype(BF16), vn, preferred_element_type=F32))
            o_h.append(num / den)
            l_h.append(jnp.broadcast_to(mx + jnp.log(den), num.shape))
        outs.append(jnp.where(head0, o_h[0], o_h[1])[:t])
        lses.append(jnp.where(head0, l_h[0], l_h[1])[:t])

        shifted = pltpu.roll(ct, hist - t, 1)
        fresh = jnp.concatenate([jnp.zeros((LANES - t, 2 * ATT_PAIR), F32),
                                 jnp.concatenate([k_new, v_new], axis=1)], axis=0).T
        lane_c = lax.broadcasted_iota(jnp.int32, (2 * ATT_PAIR, LANES), 1)
        if hist > LANES:
            new_refs[gi][0, 0, :, :hist - LANES] = shifted[:, :hist - LANES]
        new_refs[gi][0, 0, :, hist - LANES:] = jnp.where(lane_c >= LANES - t, fresh, shifted[:, hist - LANES:])
    y_ref[0] = _merge_groups(outs, lses)


def _sample_attn(q3, k3, v3, caches_t, accs, layer):
    n, t, w = q3.shape
    ng = len(DIL_GROUPS)
    for (win, dil), c in zip(DIL_GROUPS, caches_t):
        assert c.shape[3] == win and win % LANES == 0 and dil & (dil - 1) == 0 and t <= SUBLANES
    new = pl.BlockSpec((1, t, w), lambda b: (b, 0, 0))
    blk = lambda c: pl.BlockSpec((1, 1) + c.shape[2:], lambda b: (layer, b, 0, 0))
    res = pl.pallas_call(
        _sample_attn_kernel,
        grid=(n,),
        in_specs=[new, new, new] + [blk(c) for c in caches_t]
        + [pl.BlockSpec(memory_space=pl.ANY)] * ng,
        out_specs=[pl.BlockSpec((1, t, LANES), lambda b: (b, 0, 0))] + [blk(c) for c in accs],
        out_shape=[jax.ShapeDtypeStruct((n, t, LANES), F32)]
        + [jax.ShapeDtypeStruct(a.shape, F32) for a in accs],
        input_output_aliases={3 + ng + g: 1 + g for g in range(ng)},
        compiler_params=_params("parallel"),
        name="sample_attn",
    )(q3, k3, v3, *caches_t, *accs)
    return res[0], list(res[1:])


def _rwkv_prep_kernel(prev8_ref, sp_ref, cur_ref, mu_ref, wwa_ref, w0_ref, a0_ref, g2_ref,
                      kk_ref, ka_ref, rk_ref, ones_ref, tril_ref,
                      at_ref, bt_ref, kt_ref, rt_ref, v_ref, pl_ref, g_ref, bonus_ref,
                      sh_ref, p_ref, *, chunk):
    bn, tm, w = cur_ref.shape
    j = pl.program_id(1)
    cur = cur_ref[...]
    prev_row = jnp.where(j == 0, sp_ref[...], prev8_ref[:, SUBLANES - 1:SUBLANES, :])
    sh_ref[:, SUBLANES - 1:SUBLANES, :] = prev_row
    sh_ref[:, SUBLANES:, :] = cur
    prev = sh_ref[:, SUBLANES - 1:SUBLANES - 1 + tm, :]
    m = bn * tm
    zs = (cur + (prev - cur) * mu_ref[...]).reshape(m, w)

    r = zs[:, 0:RWKV_W]
    k = zs[:, RWKV_W:2 * RWKV_W]
    v = zs[:, 2 * RWKV_W:3 * RWKV_W]
    lo = 3 * RWKV_W
    z_wa = zs[:, lo:lo + LANES]
    z_g = zs[:, lo + LANES:lo + 2 * LANES]
    lane = lax.broadcasted_iota(jnp.int32, z_wa.shape, 1)
    u = jnp.where(lane < LANES // 2, jnp.tanh(z_wa), z_wa)
    lora = _bdot(u, wwa_ref[...])
    xw = w0_ref[...] + lora[:, :RWKV_W]
    w_log = -(jnp.maximum(-xw, 0.0) + jnp.log(1.0 + jnp.exp(-jnp.abs(xw)))) - 0.5
    e = jnp.exp(w_log)
    a = _sigmoid(a0_ref[...] + lora[:, RWKV_W:])
    g_ref[...] = _bdot(_sigmoid(z_g), g2_ref[...])

    kk = k * kk_ref[...]
    kk = kk * lax.rsqrt(jnp.maximum(_head_sums(kk * kk, ones_ref), 1e-24))
    k = k * (1.0 + (a - 1.0) * ka_ref[...])
    bonus_ref[...] = _head_sums(r * k * rk_ref[...], ones_ref) * v

    tril = tril_ref[...]
    tb = tril.shape[0]
    e1 = e.astype(BF16)
    rem = e - e1.astype(F32)
    e2 = rem.astype(BF16)
    e3 = (rem - e2.astype(F32)).astype(BF16)
    e123 = jnp.concatenate([e1, e2, e3], axis=1)
    cum3 = jnp.concatenate([jnp.dot(tril, e123[b * tb:(b + 1) * tb], preferred_element_type=F32)
                            for b in range(m // tb)], axis=0)
    cum = cum3[:, :RWKV_W] + (cum3[:, RWKV_W:2 * RWKV_W] + cum3[:, 2 * RWKV_W:])
    p_inc = jnp.exp(-cum)
    p_inv = jnp.exp(cum)

    def put(ref, val):
        ref[:, :tm, :] = val.reshape(bn, tm, RWKV_W)
        if ref.shape[1] > tm:
            ref[:, tm:, :] = jnp.zeros((bn, ref.shape[1] - tm, RWKV_W), F32)

    put(at_ref, -kk * jnp.exp(e - cum))
    put(bt_ref, kk * a * p_inv)
    put(kt_ref, k * p_inv)
    put(rt_ref, r * p_inc)
    put(v_ref, v)
    p_ref[...] = p_inc
    for c in range(m // chunk):
        pl_ref[c] = p_ref[(c + 1) * chunk - 1:(c + 1) * chunk, :]


def _rwkv_prep(zr3, shift_prev, state_layer, wts, layer, *, bn, tm, chunk, t_out):
    n, t, w = zr3.shape
    m = bn * tm
    per = tm // SUBLANES
    assert t_out == t or t == tm
    rowblk = pl.BlockSpec((m, RWKV_W), lambda b, j: (b * (t // tm) + j, 0))
    scanblk = pl.BlockSpec((bn, tm if t_out == t else t_out, RWKV_W), lambda b, j: (b, j, 0))
    tb = min(m, MXU_TILE)
    assert tb % chunk == 0 and m % tb == 0
    tril = (jnp.arange(tb)[:, None] >= jnp.arange(tb)[None, :]) & \
           (jnp.arange(tb)[:, None] // chunk == jnp.arange(tb)[None, :] // chunk)
    nch = m // chunk
    names = ('rwkv_mu', 'rwkv_wwa', 'rwkv_w0', 'rwkv_a0', 'rwkv_g2', 'rwkv_k_k', 'rwkv_k_a', 'rwkv_r_k')
    consts = [wts[k] for k in names] + [wts['ones_bd'], tril.astype(BF16)]
    const_specs = [_wspec(wts[k], layer) for k in names] + [_wspec(wts['ones_bd']), _wspec(consts[-1])]
    outs = pl.pallas_call(
        functools.partial(_rwkv_prep_kernel, chunk=chunk),
        grid=(n // bn, t // tm),
        in_specs=[pl.BlockSpec((bn, SUBLANES, w), lambda b, j: (b, jnp.maximum(j * per - 1, 0), 0)),
                  pl.BlockSpec((None, bn, 1, w), lambda b, j: (state_layer, b, 0, 0)),
                  pl.BlockSpec((bn, tm, w), lambda b, j: (b, j, 0))]
        + const_specs,
        out_specs=[scanblk] * 5
        + [pl.BlockSpec((nch, 1, RWKV_W), lambda b, j: (b * (t // tm) + j, 0, 0)), rowblk, rowblk],
        out_shape=[jax.ShapeDtypeStruct((n, t_out, RWKV_W), F32)] * 5
        + [jax.ShapeDtypeStruct((n * t // chunk, 1, RWKV_W), F32)]
        + [jax.ShapeDtypeStruct((n * t, RWKV_W), F32)] * 2,
        scratch_shapes=[pltpu.VMEM((bn, tm + SUBLANES, w), F32), pltpu.VMEM((m, RWKV_W), F32)],
        compiler_params=_params("parallel", "parallel"),
        name="rwkv_prep",
    )(zr3, shift_prev, zr3, *consts)
    return outs


def _rwkv_scan_kernel(at_ref, bt_ref, kt_ref, rt_ref, v_ref, pl_ref, s0_ref, y_ref, sT_ref, s_ref):
    c = pl.program_id(1)
    chunk = RWKV_HD
    nb, nsub = at_ref.shape[0], at_ref.shape[1] // chunk
    npair = RWKV_W // LANES
    assert 2 * chunk == LANES

    @pl.when(c == 0)
    def _():
        s_ref[...] = s0_ref[...]

    lane = lax.broadcasted_iota(jnp.int32, (chunk, LANES), 1)
    head0 = lane < RWKV_HD
    row2 = lax.broadcasted_iota(jnp.int32, (chunk, 2 * LANES), 0)
    col2 = lax.broadcasted_iota(jnp.int32, (chunk, 2 * LANES), 1) & (chunk - 1)
    strict2 = col2 < row2
    incl2 = col2 <= row2
    row1 = lax.broadcasted_iota(jnp.int32, (chunk, LANES), 0)
    eye_ls = jnp.where((lane & (chunk - 1)) == row1, 1.0, 0.0).astype(F32)
    ri = lax.broadcasted_iota(jnp.int32, (LANES, LANES), 0)
    ci = lax.broadcasted_iota(jnp.int32, (LANES, LANES), 1)
    same = (ri < chunk) == (ci < chunk)
    eye = jnp.where(ri == ci, 1.0, 0.0).astype(F32)

    def stack(x):
        return jnp.concatenate([jnp.where(head0, x, 0.0), jnp.where(head0, 0.0, x)], axis=0)

    def bdiag(x):
        return jnp.where(same, jnp.concatenate([x, x], axis=0), 0.0)

    def bdiag_bf(x):
        return bdiag(x).astype(BF16)

    def mm(a, b, dims=NN):
        (ah, al), (bh, bl) = a, b
        lhs = jnp.concatenate([ah, ah, al], axis=0 if dims == TN else 1)
        rhs = jnp.concatenate([bh, bl, bh], axis=1 if dims == NT else 0)
        return lax.dot_general(lhs, rhs, dims, preferred_element_type=F32)

    jobs = [(sq, p, slice(p * LANES, (p + 1) * LANES), slice(ch * chunk, (ch + 1) * chunk), ch)
            for sq in range(nb) for p in range(npair) for ch in range(nsub)]
    states = {(sq, p): s_ref[sq, p] for sq in range(nb) for p in range(npair)}
    at = [at_ref[sq, rs, cs] for sq, p, cs, rs, ch in jobs]
    rt = [rt_ref[sq, rs, cs] for sq, p, cs, rs, ch in jobs]
    vv = [v_ref[sq, rs, cs] for sq, p, cs, rs, ch in jobs]
    gram_a, gram_r = [], []
    for i, (sq, p, cs, rs, ch) in enumerate(jobs):
        bk = _split(jnp.concatenate([stack(bt_ref[sq, rs, cs]), stack(kt_ref[sq, rs, cs])], axis=0))
        gram_a.append(mm(_split(at[i]), bk, NT))
        gram_r.append(lax.dot_general(rt[i].astype(BF16), bk[0], NT, preferred_element_type=F32))
    m_ab = [jnp.where(strict2, g, 0.0)[:, :LANES] for g in gram_a]
    m_ak = [jnp.where(strict2, g, 0.0)[:, LANES:] for g in gram_a]
    m_r = [jnp.where(incl2, g, 0.0).astype(BF16) for g in gram_r]
    zero_bf = jnp.zeros((LANES, LANES), BF16)
    nlev = int(math.log2(chunk))
    npow = [jnp.dot(m.astype(BF16), bdiag_bf(m), preferred_element_type=F32) for m in m_ab]
    inv = [eye_ls + m for m in m_ab]
    for lev in range(1, nlev):
        for i in range(len(jobs)):
            if lev < nlev - 1:
                rhs = jnp.concatenate([bdiag_bf(npow[i]), bdiag_bf(inv[i])], axis=1)
                out = jnp.dot(npow[i].astype(BF16), rhs, preferred_element_type=F32)
                npow[i], inv[i] = out[:, :LANES], inv[i] + out[:, LANES:]
            else:
                inv[i] = inv[i] + jnp.dot(npow[i].astype(BF16), bdiag_bf(inv[i]), preferred_element_type=F32)
    mv = [mm(_split(m), _split(bdiag(x))) for m, x in zip(m_ak, vv)]
    wu = [mm(_split(t), _split(jnp.concatenate([bdiag(x), bdiag(a)], axis=1))) for t, x, a in zip(inv, mv, at)]
    qy = []
    for i in range(len(jobs)):
        w_m, u_m = wu[i][:, :LANES], wu[i][:, LANES:]
        zq = jnp.concatenate([jnp.concatenate([bdiag_bf(u_m), bdiag_bf(w_m)], axis=1),
                              jnp.concatenate([zero_bf, bdiag_bf(vv[i])], axis=1)], axis=0)
        qy.append(jnp.dot(m_r[i], zq, preferred_element_type=F32))
    ad = []
    for i, (sq, p, cs, rs, ch) in enumerate(jobs):
        decay = pl_ref[sq, ch, :, cs]
        bk = jnp.concatenate([bt_ref[sq, rs, cs], kt_ref[sq, rs, cs]], axis=0) * decay
        z = jnp.concatenate([jnp.concatenate([wu[i][:, LANES:], wu[i][:, :LANES]], axis=1),
                             jnp.concatenate([jnp.zeros_like(vv[i]), vv[i]], axis=1)], axis=0)
        ad.append(mm(_split(bk), _split(z), TN))
    a_t = [_split(eye * pl_ref[sq, ch, :, cs] + jnp.where(same, ad[i][:, :LANES], 0.0))
           for i, (sq, p, cs, rs, ch) in enumerate(jobs)]
    d_t = [jnp.where(same, ad[i][:, LANES:], 0.0) for i in range(len(jobs))]
    q_t = [(rt[i] + qy[i][:, :LANES]).astype(BF16) for i in range(len(jobs))]
    for i, (sq, p, cs, rs, ch) in sorted(enumerate(jobs), key=lambda e: e[1][4]):
        st = _split(states[sq, p])
        y_ref[sq, rs, cs] = jnp.dot(q_t[i], st[0], preferred_element_type=F32) + qy[i][:, LANES:]
        states[sq, p] = mm(a_t[i], st) + d_t[i]
    for (sq, p), val in states.items():
        s_ref[sq, p] = val

    @pl.when(c == pl.num_programs(1) - 1)
    def _():
        sT_ref[...] = s_ref[...]


def _rwkv_scan(at, bt, kt, rt, v, plast, s0, state_layer, *, n, t, chunk, nb):
    assert chunk == RWKV_HD and at.shape == (n, t, RWKV_W)
    npair = RWKV_W // LANES
    nch = t // chunk
    nsub = 2 if nch % 2 == 0 else 1
    tok = pl.BlockSpec((nb, nsub * chunk, RWKV_W), lambda b, c: (b, c, 0))
    st = pl.BlockSpec((nb, npair, LANES, LANES), lambda b, c: (b, 0, 0, 0))
    st_in = pl.BlockSpec((None, nb, npair, LANES, LANES), lambda b, c: (state_layer, b, 0, 0, 0))
    y, s_t = pl.pallas_call(
        _rwkv_scan_kernel,
        grid=(n // nb, nch // nsub),
        in_specs=[tok] * 5 + [pl.BlockSpec((nb, nsub, 1, RWKV_W), lambda b, c: (b, c, 0, 0)), st_in],
        out_specs=[tok, st],
        out_shape=[jax.ShapeDtypeStruct((n, t, RWKV_W), F32),
                   jax.ShapeDtypeStruct((n, npair, LANES, LANES), F32)],
        scratch_shapes=[pltpu.VMEM((nb, npair, LANES, LANES), F32)],
        compiler_params=_params("parallel", "arbitrary"),
        name="rwkv_scan",
    )(at, bt, kt, rt, v, plast.reshape(n, nch, 1, RWKV_W), s0)
    return y, s_t


def _merge_kernel(x_ref, yp_ref, ya_ref, ys_ref, bonus_ref, g_ref, gate_ref,
                  pp_ref, pa_ref, pr_ref, wo_ref, lng_ref, lnb_ref, mean_ref, out_ref):
    d = x_ref.shape[1]
    y_attn = ya_ref[...]

    ys = ys_ref[...]
    mu = _head_sums(ys, mean_ref)
    dev = ys - mu
    var = _head_sums(dev * dev, mean_ref)
    yn = dev * lax.rsqrt(var + RWKV_LN_EPS) * lng_ref[...] + lnb_ref[...]
    y_rwkv = (yn + bonus_ref[...]) * g_ref[...]

    gate = lambda c: gate_ref[:, c * d:(c + 1) * d].astype(F32)
    merged = (gate(0) * _bdot(yp_ref[...], pp_ref[...])
              + gate(1) * _bdot(y_attn, pa_ref[...])
              + gate(2) * _bdot(y_rwkv, pr_ref[...]))
    out_ref[...] = x_ref[...] + _bdot(merged, wo_ref[...])


def _merge(x2d, yp, ya, ys, bonus, g, gates, wts, layer, tm):
    m, d = x2d.shape
    row = lambda w: pl.BlockSpec((tm, w), lambda i: (i, 0))
    names = ('proj_pool', 'proj_attn', 'proj_rwkv', 'w_out', 'rwkv_ln_g', 'rwkv_ln_b')
    consts = [wts[k] for k in names] + [wts['mean_bd']]
    return pl.pallas_call(
        _merge_kernel,
        grid=(m // tm,),
        in_specs=[row(d), row(POOL_W), row(LANES)] + [row(RWKV_W)] * 3 + [row(3 * d)]
        + [_wspec(wts[k], layer) for k in names] + [_wspec(wts['mean_bd'])],
        out_specs=row(d),
        out_shape=jax.ShapeDtypeStruct((m, d), F32),
        compiler_params=_params("parallel"),
        name="merge_out_proj",
    )(x2d, yp, ya, ys, bonus, g, gates, *consts)


def _ffn_ple_kernel(x_ref, p_ref, gf_ref, g1_ref, w1_ref, w3_ref, w2_ref, g2_ref, wg_ref, wp_ref,
                    out_ref, *, final_norm):
    x = x_ref[...]
    h = _rms(x, g1_ref[...]).astype(BF16)
    hidden = w1_ref.shape[1]
    cut = pl.cdiv(hidden // MXU_TILE, 2) * MXU_TILE
    y = x
    for cs in (slice(0, cut), slice(cut, hidden)):
        h1 = jnp.dot(h, w1_ref[:, cs], preferred_element_type=F32)
        h3 = jnp.dot(h, w3_ref[:, cs], preferred_element_type=F32)
        y = y + _bdot(h1 * _sigmoid(h1) * h3, w2_ref[cs, :])
    gate = _sigmoid(_bdot(_rms(y, g2_ref[...]), wg_ref[...]))
    y = y + _bdot(p_ref[...], wp_ref[...]) * gate
    if final_norm:
        y = _rms(y, gf_ref[...])
    out_ref[...] = y


def _ffn_ple(x2d, p_all, wts, layer, g_final, tm, final_norm):
    m, d = x2d.shape
    row = pl.BlockSpec((tm, d), lambda i: (i, 0))
    names = ('norm_ffn_g', 'ffn_w1', 'ffn_w3', 'ffn_w2', 'norm_ple_g', 'ple_gate', 'ple_proj')
    return pl.pallas_call(
        functools.partial(_ffn_ple_kernel, final_norm=final_norm),
        grid=(m // tm,),
        in_specs=[row, pl.BlockSpec((None, tm, p_all.shape[2]), lambda i: (layer, i, 0)), _wspec(g_final)]
        + [_wspec(wts[k], layer) for k in names],
        out_specs=row,
        out_shape=jax.ShapeDtypeStruct((m, d), F32),
        compiler_params=_params("parallel"),
        name="swiglu_ple_final" if final_norm else "swiglu_ple",
    )(x2d, p_all, g_final, *[wts[k] for k in names])


def _rope_tables(pos, reps):
    half = ATT_HD // 2
    inv = ROPE_THETA ** (-2.0 * jnp.arange(half, dtype=F32) / ATT_HD)
    ang = pos.astype(F32)[:, None] * inv[None, :]
    cos = jnp.cos(ang)
    sin = jnp.sin(ang)
    cos = jnp.tile(jnp.concatenate([cos, cos], axis=1), (reps, LANES // ATT_HD))
    sin = jnp.tile(jnp.concatenate([-sin, sin], axis=1), (reps, LANES // ATT_HD))
    return cos, sin


def _block_diag(blocks):
    g, a, b = blocks.shape[-3:]
    rows = []
    for i in range(g):
        parts = [blocks[..., i, :, :] if j == i else jnp.zeros_like(blocks[..., i, :, :]) for j in range(g)]
        rows.append(jnp.concatenate(parts, axis=-1))
    return jnp.concatenate(rows, axis=-2)


def _stacked_weights(w):
    row = lambda a: a.reshape(a.shape[0], 1, -1)
    bf = lambda a: a.astype(BF16)
    heads = RWKV_W // RWKV_HD
    zero = jnp.zeros_like(w['rwkv_w2'])
    wwa = jnp.concatenate([jnp.concatenate([w['rwkv_w2'], zero], axis=2),
                           jnp.concatenate([zero, w['rwkv_a2']], axis=2)], axis=1)
    ones = jnp.ones((heads, RWKV_HD, RWKV_HD), F32)
    return {
        'norm_mix_g': row(w['norm_mix_g']), 'w_in': bf(w['w_in']),
        'pool_w_bd': bf(_block_diag(w['pool_w_grp'])), 'pool_scale': row(w['pool_scale']),
        'rwkv_mu': row(w['rwkv_mu']), 'rwkv_wwa': bf(wwa),
        'rwkv_w0': row(w['rwkv_w0']), 'rwkv_a0': row(w['rwkv_a0']), 'rwkv_g2': bf(w['rwkv_g2']),
        'rwkv_k_k': row(w['rwkv_k_k']), 'rwkv_k_a': row(w['rwkv_k_a']), 'rwkv_r_k': row(w['rwkv_r_k']),
        'rwkv_ln_g': row(w['rwkv_ln_g']), 'rwkv_ln_b': row(w['rwkv_ln_b']),
        'ones_bd': bf(_block_diag(ones)), 'mean_bd': bf(_block_diag(ones / RWKV_HD)),
        'proj_pool': bf(w['proj_pool']), 'proj_attn': bf(w['proj_attn']),
        'proj_rwkv': bf(w['proj_rwkv']), 'w_out': bf(w['w_out']),
        'norm_ffn_g': row(w['norm_ffn_g']), 'ffn_w1': bf(w['ffn_w1']), 'ffn_w3': bf(w['ffn_w3']),
        'ffn_w2': bf(w['ffn_w2']), 'norm_ple_g': row(w['norm_ple_g']),
        'ple_proj': bf(w['ple_proj']), 'ple_gate': bf(w['ple_gate']),
    }


def _token_tiles(m):
    tm = next(c for c in (512, 256, 128, m) if m % c == 0)
    return tm, tm


def _pair_states(s):
    st = jnp.swapaxes(s, -1, -2)
    st = st.reshape(st.shape[:-3] + (st.shape[-3] // 2, 2) + st.shape[-2:])
    return _block_diag(st)


def _unpair_states(s2):
    half = s2.shape[-1] // 2
    s = jnp.stack([s2[..., :half, :half], s2[..., half:, half:]], axis=-3)
    s = s.reshape(s.shape[:-4] + (2 * s.shape[-4],) + s.shape[-2:])
    return jnp.swapaxes(s, -1, -2)


def _decoder_layer(x2d, n, t, p_all, wts, layer, g_final, final_norm, tabs, state, *, prompt, tm, tm_in, chunk):
    sl = state['layer']
    zp, q, k, v, zr, gates = _in_proj(x2d, wts, layer, tabs[0], tabs[1], tm_in)
    three = lambda a: a.reshape(n, t, a.shape[-1])
    zp3, q3, k3, v3, zr3 = three(zp), three(q), three(k), three(v), three(zr)

    if prompt:
        yp = _pool(None, zp3, wts, layer, bn=1, tm=tm, pos0=0)
        new_pool = zp3[:, t - POOL_HIST:]
    else:
        yp = _pool(state['pool16'], zp3, wts, layer, bn=n, tm=t, pos0=state['pos0'])
        new_pool = jnp.concatenate([state['pool16'][sl], zp3], axis=1)[:, -POOL_HIST:]

    if prompt:
        ya = _dil_attn(q3, k3, v3)
        new_kv = []
        for gi, (win, dil) in enumerate(DIL_GROUPS):
            cols = slice(gi * ATT_PAIR, (gi + 1) * ATT_PAIR)
            keep = min(win, t)
            kv = jnp.concatenate([k3[:, t - keep:, cols], v3[:, t - keep:, cols]], axis=-1)
            new_kv.append(kv.reshape(n, keep, 2, 2, ATT_HD))
    else:
        ya, new_kv = _sample_attn(q3, k3, v3, state['kv_t'], state['kv_acc'], sl)

    if prompt:
        at, bt, kt, rt, vv, plast, g, bonus = _rwkv_prep(zr3, state['shift'], sl, wts, layer,
                                                         bn=1, tm=tm, chunk=chunk, t_out=t)
        ys, s_t = _rwkv_scan(at, bt, kt, rt, vv, plast, state['wkv'], sl, n=n, t=t, chunk=chunk, nb=n)
    else:
        at, bt, kt, rt, vv, plast, g, bonus = _rwkv_prep(zr3, state['shift'], sl, wts, layer,
                                                         bn=n, tm=t, chunk=t, t_out=chunk)
        ys, s_t = _rwkv_scan(at, bt, kt, rt, vv, plast, state['wkv'], sl,
                             n=n, t=chunk, chunk=chunk, nb=4 if n % 4 == 0 else 1)
        ys = ys[:, :t]
    new_shift = zr3[:, t - 1]

    x2d = _merge(x2d, yp.reshape(n * t, POOL_W), ya.reshape(n * t, LANES), ys.reshape(n * t, RWKV_W),
                 bonus, g, gates, wts, layer, tm)
    x2d = _ffn_ple(x2d, p_all, wts, layer, g_final, tm, final_norm)
    return x2d, (new_pool, new_shift, s_t, new_kv)


def kernel(x_prompt, x_sample, state_pool, state_shift, state_wkv, cache_kv_w128, cache_kv_w512,
           cache_kv_w2048, p_prompt, p_sample, norm_mix_g, w_in, pool_w_grp, pool_scale, rwkv_mu,
           rwkv_w0, rwkv_w2, rwkv_a0, rwkv_a2, rwkv_g2, rwkv_k_k, rwkv_k_a, rwkv_r_k, rwkv_ln_g,
           rwkv_ln_b, proj_pool, proj_attn, proj_rwkv, w_out, norm_ffn_g, ffn_w1, ffn_w3, ffn_w2,
           norm_ple_g, ple_proj, ple_gate, norm_final_g):
    weights = dict(norm_mix_g=norm_mix_g, w_in=w_in, pool_w_grp=pool_w_grp, pool_scale=pool_scale,
                   rwkv_mu=rwkv_mu, rwkv_w0=rwkv_w0, rwkv_w2=rwkv_w2, rwkv_a0=rwkv_a0, rwkv_a2=rwkv_a2,
                   rwkv_g2=rwkv_g2, rwkv_k_k=rwkv_k_k, rwkv_k_a=rwkv_k_a, rwkv_r_k=rwkv_r_k,
                   rwkv_ln_g=rwkv_ln_g, rwkv_ln_b=rwkv_ln_b, proj_pool=proj_pool, proj_attn=proj_attn,
                   proj_rwkv=proj_rwkv, w_out=w_out, norm_ffn_g=norm_ffn_g, ffn_w1=ffn_w1, ffn_w3=ffn_w3,
                   ffn_w2=ffn_w2, norm_ple_g=norm_ple_g, ple_proj=ple_proj, ple_gate=ple_gate)
    depth = w_in.shape[0]
    np_, tp, d = x_prompt.shape
    ns, ts, _ = x_sample.shape
    caches = (cache_kv_w128, cache_kv_w512, cache_kv_w2048)
    past_len = PAST_LEN
    tm_p, tm_in_p = _token_tiles(np_ * tp)
    tm_s = ns * ts
    chunk = 64
    tabs_p = _rope_tables(jnp.arange(tp), 1)
    tabs_s = _rope_tables(past_len + jnp.arange(ts), ns)
    g_final = norm_final_g[None, :]
    wts = _stacked_weights(weights)

    xp = x_prompt.reshape(np_ * tp, d)
    xs = x_sample.reshape(ns * ts, d)
    pp_all = p_prompt.reshape(depth, np_ * tp, -1)
    ps_all = p_sample.reshape(depth, ns * ts, -1)
    heads = RWKV_W // RWKV_HD
    state_p = {
        'layer': 0,
        'shift': jnp.zeros((1, np_, 1, RWKV_PROJ), F32),
        'wkv': jnp.zeros((1, np_, heads // 2, 2 * RWKV_HD, 2 * RWKV_HD), F32),
    }
    caches_t = [jnp.transpose(c, (0, 1, 3, 4, 5, 2)).reshape(c.shape[0], c.shape[1], -1, c.shape[2])
                for c in caches]
    kv_acc = [jnp.zeros(c.shape, F32) for c in caches_t]
    state_s = {
        'pool16': jnp.pad(state_pool, ((0, 0), (0, 0), (POOL_HALO - POOL_HIST, 0), (0, 0))),
        'shift': state_shift[:, :, None, :], 'wkv': _pair_states(state_wkv),
        'kv_t': caches_t, 'pos0': past_len,
    }
    outs_p, outs_s = [], []
    for i in range(depth):
        last = i == depth - 1
        xp, st_p = _decoder_layer(xp, np_, tp, pp_all, wts, i, g_final, last, tabs_p, state_p,
                                  prompt=True, tm=tm_p, tm_in=tm_in_p, chunk=chunk)
        xs, st_s = _decoder_layer(xs, ns, ts, ps_all, wts, i, g_final, last, tabs_s,
                                  dict(state_s, layer=i, kv_acc=kv_acc),
                                  prompt=False, tm=tm_s, tm_in=tm_s, chunk=chunk)
        kv_acc = st_s[3]
        outs_p.append(st_p)
        outs_s.append(st_s)

    stack = lambda outs, f: jnp.stack([f(o) for o in outs])
    res = [xp.reshape(np_, tp, d), xs.reshape(ns, ts, d)]
    for idx in range(2):
        res.append(stack(outs_p, lambda o: o[idx]))
        res.append(stack(outs_s, lambda o: o[idx]))
    res.append(_unpair_states(stack(outs_p, lambda o: o[2])))
    res.append(_unpair_states(stack(outs_s, lambda o: o[2])))
    for gi in range(len(DIL_GROUPS)):
        res.append(stack(outs_p, lambda o: o[3][gi]))
        a = kv_acc[gi]
        a = a.reshape(a.shape[0], a.shape[1], 2, 2, ATT_HD, a.shape[3])
        res.append(jnp.transpose(a, (0, 1, 5, 2, 3, 4)))
    return tuple(res)
```

```python
import functools
import math

import jax
import jax.numpy as jnp
from jax import lax
from jax.experimental import pallas as pl
from jax.experimental.pallas import tpu as pltpu

F32 = jnp.float32
BF16 = jnp.bfloat16

POOL_GC = 64
POOL_W = 256
POOL_WINDOWS = (2, 4, 8, 16)
POOL_HIST = 15
POOL_HALO = 16
ATT_HD = 64
ATT_W = 384
ATT_PAIR = 128
DIL_GROUPS = ((128, 1), (512, 4), (2048, 16))
BAND_BLK = 128
ROPE_THETA = 10000.0
RWKV_HD = 64
RWKV_W = 384
RWKV_PROJ = 1408
RWKV_LN_EPS = 64e-5
RMS_EPS = 1e-6
PAST_LEN = 8192
NEG_BIG = -1e30

LANES = 128
SUBLANES = 8
MXU_TILE = 256
VMEM_LIMIT = 56 * 1024 * 1024

NN = (((1,), (0,)), ((), ()))
NT = (((1,), (1,)), ((), ()))
TN = (((0,), (0,)), ((), ()))


def _params(*sem):
    return pltpu.CompilerParams(dimension_semantics=sem, vmem_limit_bytes=VMEM_LIMIT)


def _wspec(arr, layer=None):
    if layer is None:
        nd = arr.ndim
        return pl.BlockSpec(arr.shape, lambda *_: (0,) * nd, pipeline_mode=pl.Buffered(1))
    nd = arr.ndim - 1
    return pl.BlockSpec((None,) + arr.shape[1:], lambda *_: (layer,) + (0,) * nd,
                        pipeline_mode=pl.Buffered(1))


def _bdot(a, b, dims=NN):
    return lax.dot_general(a.astype(BF16), b.astype(BF16), dims, preferred_element_type=F32)


def _split(a):
    hi = a.astype(BF16)
    lo = (a - hi.astype(F32)).astype(BF16)
    return hi, lo


def _mm3(a, b, dims=NN):
    ah, al = _split(a)
    bh, bl = _split(b)
    f = lambda x, y: lax.dot_general(x, y, dims, preferred_element_type=F32)
    return f(ah, bh) + (f(ah, bl) + f(al, bh))


def _mm2_exact_rhs(a, b_bf16, dims=NN):
    ah, al = _split(a)
    f = lambda x: lax.dot_general(x, b_bf16, dims, preferred_element_type=F32)
    return f(ah) + f(al)


def _head_sums(x, bd_ref):
    w = x.shape[1]
    cuts = [c for c in range(0, w, MXU_TILE)] + [w]
    parts = [_mm2_exact_rhs(x[:, a:b], bd_ref[a:b, a:b]) for a, b in zip(cuts[:-1], cuts[1:])]
    return jnp.concatenate(parts, axis=1)


def _rms(x, g):
    return x * lax.rsqrt(jnp.mean(x * x, axis=-1, keepdims=True) + RMS_EPS) * g


def _sigmoid(x):
    return 1.0 / (1.0 + jnp.exp(-x))


def _pool_from_ext(ext_ref, cur, pos, w_bd, scale):
    bn, tm, w = cur.shape
    lane = lax.broadcasted_iota(jnp.int32, (bn, tm, w), 2)
    group = jnp.right_shift(lane, POOL_GC.bit_length() - 1)
    acc = cur
    win = jnp.zeros_like(cur)
    for s in range(1, POOL_WINDOWS[-1] + 1):
        if s in POOL_WINDOWS:
            win = jnp.where(group == POOL_WINDOWS.index(s), acc, win)
        if s < POOL_WINDOWS[-1]:
            acc = acc + ext_ref[:, POOL_HALO - s:POOL_HALO - s + tm, :]
    cnt = jnp.minimum(pos + 1, jnp.left_shift(2, group)).astype(F32)
    dlt = (win / cnt - cur).reshape(bn * tm, w)
    return _bdot(dlt, w_bd) * scale


def _in_proj_kernel(x_ref, g_ref, w_ref, cos_ref, sin_ref, *rest, tiles_per_seq):
    if tiles_per_seq is None:
        pool_ref, q_ref, k_ref, v_ref, rw_ref, gate_ref = rest
    else:
        pw_ref, ps_ref, pool_ref, q_ref, k_ref, v_ref, rw_ref, gate_ref, tail_ref, ext_ref = rest

        @pl.when(pl.program_id(0) == 0)
        def _():
            ext_ref[...] = jnp.zeros(ext_ref.shape, F32)

    h = _rms(x_ref[...], g_ref[...]).astype(BF16)

    def seg(a, b):
        return jnp.dot(h, w_ref[:, a:b], preferred_element_type=F32)

    cos = cos_ref[...]
    sin = sin_ref[...]
    lane = lax.broadcasted_iota(jnp.int32, cos.shape, 1)
    low_half = (lane & 32) == 0

    def rope(t):
        partner = jnp.where(low_half, pltpu.roll(t, 96, 1), pltpu.roll(t, 32, 1))
        return t * cos + partner * sin

    n_mix = POOL_W + 3 * ATT_W + RWKV_PROJ
    mix = seg(0, n_mix)
    o = 0
    zp = mix[:, o:o + POOL_W]
    if tiles_per_seq is None:
        pool_ref[...] = zp
    else:
        tm = zp.shape[0]
        tile = pl.program_id(0) % tiles_per_seq
        halo = jnp.where(tile == 0, 0.0, ext_ref[:, tm:, :])
        ext_ref[:, :POOL_HALO, :] = halo
        ext_ref[:, POOL_HALO:, :] = zp[None]
        pos = tile * tm + lax.broadcasted_iota(jnp.int32, (1, tm, POOL_W), 1)
        pool_ref[...] = _pool_from_ext(ext_ref, zp[None], pos, pw_ref[...], ps_ref[...])
        tail_ref[...] = zp[None, tm - POOL_HALO:, :]
    o += POOL_W
    for c in range(ATT_W // LANES):
        cs = slice(c * LANES, (c + 1) * LANES)
        q_ref[:, cs] = rope(mix[:, o + c * LANES:o + (c + 1) * LANES]) * (ATT_HD ** -0.5)
        k_ref[:, cs] = rope(mix[:, o + ATT_W + c * LANES:o + ATT_W + (c + 1) * LANES])
    o += 2 * ATT_W
    v_ref[...] = mix[:, o:o + ATT_W]
    o += ATT_W
    rw_ref[...] = mix[:, o:o + RWKV_PROJ]
    gate_ref[...] = _sigmoid(seg(n_mix, w_ref.shape[1])).astype(gate_ref.dtype)


def _in_proj(x2d, wts, layer, cos, sin, tm, seq_len=None):
    m, d = x2d.shape
    g, w_bf = wts['norm_mix_g'], wts['w_in']
    ncol = w_bf.shape[-1]
    ngate = ncol - (POOL_W + 3 * ATT_W + RWKV_PROJ)
    ntab = cos.shape[0] // tm
    row = lambda w: pl.BlockSpec((tm, w), lambda i: (i, 0))
    tab = pl.BlockSpec((tm, LANES), lambda i: (i % ntab, 0))
    widths = (POOL_W, ATT_W, ATT_W, ATT_W, RWKV_PROJ, ngate)
    dtypes = (F32,) * 5 + (BF16,)
    in_specs = [row(d), _wspec(g, layer), _wspec(w_bf, layer), tab, tab]
    args = [x2d, g, w_bf, cos, sin]
    out_specs = [row(w) for w in widths]
    out_shape = [jax.ShapeDtypeStruct((m, w), dt) for w, dt in zip(widths, dtypes)]
    scratch = []
    if seq_len is not None:
        assert seq_len % tm == 0 and tm >= POOL_HALO
        in_specs += [_wspec(wts['pool_w_bd'], layer), _wspec(wts['pool_scale'], layer)]
        args += [wts['pool_w_bd'], wts['pool_scale']]
        out_specs.append(pl.BlockSpec((1, POOL_HALO, POOL_W), lambda i: (i, 0, 0)))
        out_shape.append(jax.ShapeDtypeStruct((m // tm, POOL_HALO, POOL_W), F32))
        scratch.append(pltpu.VMEM((1, tm + POOL_HALO, POOL_W), F32))
    return pl.pallas_call(
        functools.partial(_in_proj_kernel, tiles_per_seq=None if seq_len is None else seq_len // tm),
        grid=(m // tm,),
        in_specs=in_specs,
        out_specs=out_specs,
        out_shape=out_shape,
        scratch_shapes=scratch,
        compiler_params=_params("parallel" if seq_len is None else "arbitrary"),
        name="in_proj",
    )(*args)


def _pool_kernel(prev_ref, cur_ref, w_ref, scale_ref, y_ref, ext_ref, *, pos0):
    bn, tm, w = cur_ref.shape
    cur = cur_ref[...]
    ext_ref[:, :POOL_HALO, :] = prev_ref[...]
    ext_ref[:, POOL_HALO:, :] = cur
    pos = pos0 + lax.broadcasted_iota(jnp.int32, (bn, tm, w), 1)
    y_ref[...] = _pool_from_ext(ext_ref, cur, pos, w_ref[...], scale_ref[...]).reshape(bn, tm, w)


def _pool(hist, cur3, wts, layer, *, pos0):
    n, t, w = cur3.shape
    w_bd, scale = wts['pool_w_bd'], wts['pool_scale']
    return pl.pallas_call(
        functools.partial(_pool_kernel, pos0=pos0),
        grid=(1,),
        in_specs=[pl.BlockSpec((None, n, POOL_HALO, w), lambda b: (layer, 0, 0, 0)),
                  pl.BlockSpec((n, t, w), lambda b: (0, 0, 0)),
                  _wspec(w_bd, layer), _wspec(scale, layer)],
        out_specs=pl.BlockSpec((n, t, w), lambda b: (0, 0, 0)),
        out_shape=jax.ShapeDtypeStruct((n, t, w), F32),
        scratch_shapes=[pltpu.VMEM((n, t + POOL_HALO, w), F32)],
        compiler_params=_params("arbitrary"),
        name="pool_mix",
    )(hist, cur3, w_bd, scale)


def _softmax_pair(q, k, v, valid):
    m_rows = q.shape[0]
    lane = lax.broadcasted_iota(jnp.int32, (m_rows, ATT_PAIR), 1)
    head0 = lane < ATT_HD
    kb = k.astype(BF16)
    vb = v.astype(BF16)
    outs, lses = [], []
    for hs in range(2):
        hm = head0 if hs == 0 else jnp.logical_not(head0)
        qm = jnp.where(hm, q, 0.0).astype(BF16)
        s = lax.dot_general(qm, kb, NT, preferred_element_type=F32)
        s = jnp.where(valid, s, NEG_BIG)
        mx = jnp.max(s, axis=-1, keepdims=True)
        e = jnp.exp(s - mx)
        den = jnp.sum(e, axis=-1, keepdims=True)
        o = jnp.dot(e.astype(BF16), vb, preferred_element_type=F32) / den
        outs.append(o)
        lses.append(jnp.broadcast_to(mx + jnp.log(den), o.shape))
    return jnp.where(head0, outs[0], outs[1]), jnp.where(head0, lses[0], lses[1])


def _merge_groups(outs, lses):
    mx = functools.reduce(jnp.maximum, lses)
    es = [jnp.exp(l - mx) for l in lses]
    num = functools.reduce(lambda a, b: a + b, [e * o for e, o in zip(es, outs)])
    return num / functools.reduce(lambda a, b: a + b, es)


def _rows(start, size, stride):
    return pl.ds(start, size, stride=stride) if stride > 1 else pl.ds(start, size)


def _dil_attn_kernel(*refs):
    ng = len(DIL_GROUPS)
    y_ref, o_scr, l_scr = refs[5 * ng:]
    j = pl.program_id(1)
    tq = y_ref.shape[1]
    qi = lax.broadcasted_iota(jnp.int32, (BAND_BLK, 2 * BAND_BLK), 0)
    kj = lax.broadcasted_iota(jnp.int32, (BAND_BLK, 2 * BAND_BLK), 1)
    dist = BAND_BLK + qi - kj
    has_prev = kj + jnp.where(j > 0, BAND_BLK, 0) >= BAND_BLK
    for gi, (win, dil) in enumerate(DIL_GROUPS):
        band = (dist >= 0) & (dist <= win // dil)
        step = BAND_BLK * dil
        q_ref, k_ref, v_ref, kp_ref, vp_ref = refs[5 * gi:5 * gi + 5]
        for r in range(dil):
            for sb in range(tq // step):
                rows = _rows(r + sb * step, BAND_BLK, dil)
                if sb == 0:
                    prow = _rows(r, BAND_BLK, dil)
                    k_prev, v_prev = kp_ref[0, prow, :], vp_ref[0, prow, :]
                    valid = band & has_prev
                else:
                    prow = _rows(r + (sb - 1) * step, BAND_BLK, dil)
                    k_prev, v_prev = k_ref[0, prow, :], v_ref[0, prow, :]
                    valid = band
                o, lse = _softmax_pair(q_ref[0, rows, :],
                                       jnp.concatenate([k_prev, k_ref[0, rows, :]], axis=0),
                                       jnp.concatenate([v_prev, v_ref[0, rows, :]], axis=0), valid)
                o_scr[gi, rows, :] = o
                l_scr[gi, rows, :] = lse
    y_ref[0] = _merge_groups([o_scr[g] for g in range(ng)], [l_scr[g] for g in range(ng)])


def _dil_attn(q3, k3, v3):
    n, t, w = q3.shape
    tq = BAND_BLK * max(d for _, d in DIL_GROUPS)
    assert t % tq == 0
    specs, args = [], []
    for gi, (_, dil) in enumerate(DIL_GROUPS):
        per = tq // (BAND_BLK * dil)
        cur = pl.BlockSpec((1, tq, LANES), lambda b, j, gi=gi: (b, j, gi))
        prev = pl.BlockSpec((1, BAND_BLK * dil, LANES),
                            lambda b, j, per=per, gi=gi: (b, jnp.maximum(j * per - 1, 0), gi))
        specs += [cur, cur, cur, prev, prev]
        args += [q3, k3, v3, k3, v3]
    ng = len(DIL_GROUPS)
    return pl.pallas_call(
        _dil_attn_kernel,
        grid=(n, t // tq),
        in_specs=specs,
        out_specs=pl.BlockSpec((1, tq, LANES), lambda b, j: (b, j, 0)),
        out_shape=jax.ShapeDtypeStruct((n, t, LANES), F32),
        scratch_shapes=[pltpu.VMEM((ng, tq, LANES), F32), pltpu.VMEM((ng, tq, LANES), F32)],
        compiler_params=_params("parallel", "parallel"),
        name="dil_attn",
    )(*args)


def _sample_attn_kernel(q_ref, k_ref, v_ref, *rest):
    ng = len(DIL_GROUPS)
    cache_refs, y_ref, new_refs = rest[:ng], rest[2 * ng], rest[2 * ng + 1:]
    t = q_ref.shape[1]
    mq = 2 * SUBLANES
    zpad = jnp.zeros((mq - t, ATT_PAIR), F32)
    lane = lax.broadcasted_iota(jnp.int32, (mq, ATT_PAIR), 1)
    head0 = lane < ATT_HD
    tn = lax.broadcasted_iota(jnp.int32, (mq, mq), 0)
    jn = lax.broadcasted_iota(jnp.int32, (mq, mq), 1)
    outs, lses = [], []
    for gi, (win, dil) in enumerate(DIL_GROUPS):
        reach = dil * (win // dil)
        cols = slice(gi * ATT_PAIR, (gi + 1) * ATT_PAIR)
        ct = cache_refs[gi][0, 0]
        hist = ct.shape[1]
        kt2 = ct[:ATT_PAIR].astype(BF16)
        vt2 = ct[ATT_PAIR:].astype(BF16)
        k_new, v_new = k_ref[0, :, cols], v_ref[0, :, cols]
        q = jnp.concatenate([q_ref[0, :, cols], zpad], axis=0)
        kn = jnp.concatenate([k_new, zpad], axis=0).astype(BF16)
        vn = jnp.concatenate([v_new, zpad], axis=0).astype(BF16)
        ti = lax.broadcasted_iota(jnp.int32, (mq, hist), 0)
        pj = lax.broadcasted_iota(jnp.int32, (mq, hist), 1)
        d_old = hist + ti - pj
        ok_old = (d_old <= reach) & ((d_old & (dil - 1)) == 0)
        d_new = tn - jn
        ok_new = (d_new >= 0) & (d_new <= reach) & ((d_new & (dil - 1)) == 0) & (jn < t)
        o_h, l_h = [], []
        for hs in range(2):
            hm = head0 if hs == 0 else jnp.logical_not(head0)
            qm = jnp.where(hm, q, 0.0).astype(BF16)
            s_old = jnp.where(ok_old, jnp.dot(qm, kt2, preferred_element_type=F32), NEG_BIG)
            s_new = jnp.where(ok_new, lax.dot_general(qm, kn, NT, preferred_element_type=F32), NEG_BIG)
            mx = jnp.maximum(jnp.max(s_old, axis=-1, keepdims=True), jnp.max(s_new, axis=-1, keepdims=True))
            e_old = jnp.exp(s_old - mx)
            e_new = jnp.exp(s_new - mx)
            den = jnp.sum(e_old, axis=-1, keepdims=True) + jnp.sum(e_new, axis=-1, keepdims=True)
            num = (lax.dot_general(e_old.astype(BF16), vt2, NT, preferred_element_type=F32)
                   + jnp.dot(e_new.astype(BF16), vn, preferred_element_type=F32))
            o_h.append(num / den)
            l_h.append(jnp.broadcast_to(mx + jnp.log(den), num.shape))
        outs.append(jnp.where(head0, o_h[0], o_h[1])[:t])
        lses.append(jnp.where(head0, l_h[0], l_h[1])[:t])

        shifted = pltpu.roll(ct, hist - t, 1)
        fresh = jnp.concatenate([jnp.zeros((LANES - t, 2 * ATT_PAIR), F32),
                                 jnp.concatenate([k_new, v_new], axis=1)], axis=0).T
        lane_c = lax.broadcasted_iota(jnp.int32, (2 * ATT_PAIR, LANES), 1)
        if hist > LANES:
            new_refs[gi][0, 0, :, :hist - LANES] = shifted[:, :hist - LANES]
        new_refs[gi][0, 0, :, hist - LANES:] = jnp.where(lane_c >= LANES - t, fresh, shifted[:, hist - LANES:])
    y_ref[0] = _merge_groups(outs, lses)


def _sample_attn(q3, k3, v3, caches_t, accs, layer):
    n, t, w = q3.shape
    ng = len(DIL_GROUPS)
    for (win, dil), c in zip(DIL_GROUPS, caches_t):
        assert c.shape[3] == win and win % LANES == 0 and dil & (dil - 1) == 0 and t <= SUBLANES
    new = pl.BlockSpec((1, t, w), lambda b: (b, 0, 0))
    blk = lambda c: pl.BlockSpec((1, 1) + c.shape[2:], lambda b: (layer, b, 0, 0))
    res = pl.pallas_call(
        _sample_attn_kernel,
        grid=(n,),
        in_specs=[new, new, new] + [blk(c) for c in caches_t]
        + [pl.BlockSpec(memory_space=pl.ANY)] * ng,
        out_specs=[pl.BlockSpec((1, t, LANES), lambda b: (b, 0, 0))] + [blk(c) for c in accs],
        out_shape=[jax.ShapeDtypeStruct((n, t, LANES), F32)]
        + [jax.ShapeDtypeStruct(a.shape, F32) for a in accs],
        input_output_aliases={3 + ng + g: 1 + g for g in range(ng)},
        compiler_params=_params("parallel"),
        name="sample_attn",
    )(q3, k3, v3, *caches_t, *accs)
    return res[0], list(res[1:])


def _rwkv_prep_kernel(prev8_ref, sp_ref, cur_ref, mu_ref, wwa_ref, w0_ref, a0_ref, g2_ref,
                      kk_ref, ka_ref, rk_ref, ones_ref, tril_ref,
                      at_ref, bt_ref, kt_ref, rt_ref, v_ref, pl_ref, g_ref, bonus_ref,
                      sh_ref, p_ref, *, chunk):
    bn, tm, w = cur_ref.shape
    j = pl.program_id(1)
    cur = cur_ref[...]
    prev_row = jnp.where(j == 0, sp_ref[...], prev8_ref[:, SUBLANES - 1:SUBLANES, :])
    sh_ref[:, SUBLANES - 1:SUBLANES, :] = prev_row
    sh_ref[:, SUBLANES:, :] = cur
    prev = sh_ref[:, SUBLANES - 1:SUBLANES - 1 + tm, :]
    m = bn * tm
    zs = (cur + (prev - cur) * mu_ref[...]).reshape(m, w)

    r = zs[:, 0:RWKV_W]
    k = zs[:, RWKV_W:2 * RWKV_W]
    v = zs[:, 2 * RWKV_W:3 * RWKV_W]
    lo = 3 * RWKV_W
    z_wa = zs[:, lo:lo + LANES]
    z_g = zs[:, lo + LANES:lo + 2 * LANES]
    lane = lax.broadcasted_iota(jnp.int32, z_wa.shape, 1)
    u = jnp.where(lane < LANES // 2, jnp.tanh(z_wa), z_wa)
    lora = _bdot(u, wwa_ref[...])
    xw = w0_ref[...] + lora[:, :RWKV_W]
    w_log = -(jnp.maximum(-xw, 0.0) + jnp.log(1.0 + jnp.exp(-jnp.abs(xw)))) - 0.5
    e = jnp.exp(w_log)
    a = _sigmoid(a0_ref[...] + lora[:, RWKV_W:])
    g_ref[...] = _bdot(_sigmoid(z_g), g2_ref[...])

    kk = k * kk_ref[...]
    kk = kk * lax.rsqrt(jnp.maximum(_head_sums(kk * kk, ones_ref), 1e-24))
    k = k * (1.0 + (a - 1.0) * ka_ref[...])
    bonus_ref[...] = _head_sums(r * k * rk_ref[...], ones_ref) * v

    tril = tril_ref[...]
    tb = tril.shape[0]
    e1 = e.astype(BF16)
    rem = e - e1.astype(F32)
    e2 = rem.astype(BF16)
    e3 = (rem - e2.astype(F32)).astype(BF16)
    e123 = jnp.concatenate([e1, e2, e3], axis=1)
    cum3 = jnp.concatenate([jnp.dot(tril, e123[b * tb:(b + 1) * tb], preferred_element_type=F32)
                            for b in range(m // tb)], axis=0)
    cum = cum3[:, :RWKV_W] + (cum3[:, RWKV_W:2 * RWKV_W] + cum3[:, 2 * RWKV_W:])
    p_inc = jnp.exp(-cum)
    p_inv = jnp.exp(cum)

    def put(ref, val):
        ref[:, :tm, :] = val.reshape(bn, tm, RWKV_W)
        if ref.shape[1] > tm:
            ref[:, tm:, :] = jnp.zeros((bn, ref.shape[1] - tm, RWKV_W), F32)

    put(at_ref, -kk * jnp.exp(e - cum))
    put(bt_ref, kk * a * p_inv)
    put(kt_ref, k * p_inv)
    put(rt_ref, r * p_inc)
    put(v_ref, v)
    p_ref[...] = p_inc
    for c in range(m // chunk):
        pl_ref[c] = p_ref[(c + 1) * chunk - 1:(c + 1) * chunk, :]


def _rwkv_prep(zr3, shift_prev, state_layer, wts, layer, *, bn, tm, chunk, t_out):
    n, t, w = zr3.shape
    m = bn * tm
    per = tm // SUBLANES
    assert t_out == t or t == tm
    rowblk = pl.BlockSpec((m, RWKV_W), lambda b, j: (b * (t // tm) + j, 0))
    scanblk = pl.BlockSpec((bn, tm if t_out == t else t_out, RWKV_W), lambda b, j: (b, j, 0))
    tb = min(m, MXU_TILE)
    assert tb % chunk == 0 and m % tb == 0
    tril = (jnp.arange(tb)[:, None] >= jnp.arange(tb)[None, :]) & \
           (jnp.arange(tb)[:, None] // chunk == jnp.arange(tb)[None, :] // chunk)
    nch = m // chunk
    names = ('rwkv_mu', 'rwkv_wwa', 'rwkv_w0', 'rwkv_a0', 'rwkv_g2', 'rwkv_k_k', 'rwkv_k_a', 'rwkv_r_k')
    consts = [wts[k] for k in names] + [wts['ones_bd'], tril.astype(BF16)]
    const_specs = [_wspec(wts[k], layer) for k in names] + [_wspec(wts['ones_bd']), _wspec(consts[-1])]
    outs = pl.pallas_call(
        functools.partial(_rwkv_prep_kernel, chunk=chunk),
        grid=(n // bn, t // tm),
        in_specs=[pl.BlockSpec((bn, SUBLANES, w), lambda b, j: (b, jnp.maximum(j * per - 1, 0), 0)),
                  pl.BlockSpec((None, bn, 1, w), lambda b, j: (state_layer, b, 0, 0)),
                  pl.BlockSpec((bn, tm, w), lambda b, j: (b, j, 0))]
        + const_specs,
        out_specs=[scanblk] * 5
        + [pl.BlockSpec((nch, 1, RWKV_W), lambda b, j: (b * (t // tm) + j, 0, 0)), rowblk, rowblk],
        out_shape=[jax.ShapeDtypeStruct((n, t_out, RWKV_W), F32)] * 5
        + [jax.ShapeDtypeStruct((n * t // chunk, 1, RWKV_W), F32)]
        + [jax.ShapeDtypeStruct((n * t, RWKV_W), F32)] * 2,
        scratch_shapes=[pltpu.VMEM((bn, tm + SUBLANES, w), F32), pltpu.VMEM((m, RWKV_W), F32)],
        compiler_params=_params("parallel", "parallel"),
        name="rwkv_prep",
    )(zr3, shift_prev, zr3, *consts)
    return outs


def _rwkv_scan_kernel(at_ref, bt_ref, kt_ref, rt_ref, v_ref, pl_ref, s0_ref, y_ref, sT_ref, s_ref):
    c = pl.program_id(1)
    chunk = RWKV_HD
    nb, nsub = at_ref.shape[0], at_ref.shape[1] // chunk
    npair = RWKV_W // LANES
    assert 2 * chunk == LANES

    @pl.when(c == 0)
    def _():
        s_ref[...] = s0_ref[...]

    lane = lax.broadcasted_iota(jnp.int32, (chunk, LANES), 1)
    head0 = lane < RWKV_HD
    row2 = lax.broadcasted_iota(jnp.int32, (chunk, 2 * LANES), 0)
    col2 = lax.broadcasted_iota(jnp.int32, (chunk, 2 * LANES), 1) & (chunk - 1)
    strict2 = col2 < row2
    incl2 = col2 <= row2
    row1 = lax.broadcasted_iota(jnp.int32, (chunk, LANES), 0)
    eye_ls = jnp.where((lane & (chunk - 1)) == row1, 1.0, 0.0).astype(F32)
    ri = lax.broadcasted_iota(jnp.int32, (LANES, LANES), 0)
    ci = lax.broadcasted_iota(jnp.int32, (LANES, LANES), 1)
    same = (ri < chunk) == (ci < chunk)
    eye = jnp.where(ri == ci, 1.0, 0.0).astype(F32)

    def stack(x):
        return jnp.concatenate([jnp.where(head0, x, 0.0), jnp.where(head0, 0.0, x)], axis=0)

    def bdiag(x):
        return jnp.where(same, jnp.concatenate([x, x], axis=0), 0.0)

    def bdiag_bf(x):
        return bdiag(x).astype(BF16)

    def mm(a, b, dims=NN):
        (ah, al), (bh, bl) = a, b
        lhs = jnp.concatenate([ah, ah, al], axis=0 if dims == TN else 1)
        rhs = jnp.concatenate([bh, bl, bh], axis=1 if dims == NT else 0)
        return lax.dot_general(lhs, rhs, dims, preferred_element_type=F32)

    jobs = [(sq, p, slice(p * LANES, (p + 1) * LANES), slice(ch * chunk, (ch + 1) * chunk), ch)
            for sq in range(nb) for p in range(npair) for ch in range(nsub)]
    states = {(sq, p): s_ref[sq, p] for sq in range(nb) for p in range(npair)}
    at = [at_ref[sq, rs, cs] for sq, p, cs, rs, ch in jobs]
    rt = [rt_ref[sq, rs, cs] for sq, p, cs, rs, ch in jobs]
    vv = [v_ref[sq, rs, cs] for sq, p, cs, rs, ch in jobs]
    gram_a, gram_r = [], []
    for i, (sq, p, cs, rs, ch) in enumerate(jobs):
        bk = _split(jnp.concatenate([stack(bt_ref[sq, rs, cs]), stack(kt_ref[sq, rs, cs])], axis=0))
        gram_a.append(mm(_split(at[i]), bk, NT))
        gram_r.append(lax.dot_general(rt[i].astype(BF16), bk[0], NT, preferred_element_type=F32))
    m_ab = [jnp.where(strict2, g, 0.0)[:, :LANES] for g in gram_a]
    m_ak = [jnp.where(strict2, g, 0.0)[:, LANES:] for g in gram_a]
    m_r = [jnp.where(incl2, g, 0.0).astype(BF16) for g in gram_r]
    zero_bf = jnp.zeros((LANES, LANES), BF16)
    nlev = int(math.log2(chunk))
    npow = [jnp.dot(m.astype(BF16), bdiag_bf(m), preferred_element_type=F32) for m in m_ab]
    inv = [eye_ls + m for m in m_ab]
    for lev in range(1, nlev):
        for i in range(len(jobs)):
            if lev < nlev - 1:
                rhs = jnp.concatenate([bdiag_bf(npow[i]), bdiag_bf(inv[i])], axis=1)
                out = jnp.dot(npow[i].astype(BF16), rhs, preferred_element_type=F32)
                npow[i], inv[i] = out[:, :LANES], inv[i] + out[:, LANES:]
            else:
                inv[i] = inv[i] + jnp.dot(npow[i].astype(BF16), bdiag_bf(inv[i]), preferred_element_type=F32)
    mv = [mm(_split(m), _split(bdiag(x))) for m, x in zip(m_ak, vv)]
    wu = [mm(_split(t), _split(jnp.concatenate([bdiag(x), bdiag(a)], axis=1))) for t, x, a in zip(inv, mv, at)]
    qy = []
    for i in range(len(jobs)):
        w_m, u_m = wu[i][:, :LANES], wu[i][:, LANES:]
        zq = jnp.concatenate([jnp.concatenate([bdiag_bf(u_m), bdiag_bf(w_m)], axis=1),
                              jnp.concatenate([zero_bf, bdiag_bf(vv[i])], axis=1)], axis=0)
        qy.append(jnp.dot(m_r[i], zq, preferred_element_type=F32))
    ad = []
    for i, (sq, p, cs, rs, ch) in enumerate(jobs):
        decay = pl_ref[sq, ch, :, cs]
        bk = jnp.concatenate([bt_ref[sq, rs, cs], kt_ref[sq, rs, cs]], axis=0) * decay
        z = jnp.concatenate([jnp.concatenate([wu[i][:, LANES:], wu[i][:, :LANES]], axis=1),
                             jnp.concatenate([jnp.zeros_like(vv[i]), vv[i]], axis=1)], axis=0)
        ad.append(mm(_split(bk), _split(z), TN))
    a_t = [_split(eye * pl_ref[sq, ch, :, cs] + jnp.where(same, ad[i][:, :LANES], 0.0))
           for i, (sq, p, cs, rs, ch) in enumerate(jobs)]
    d_t = [jnp.where(same, ad[i][:, LANES:], 0.0) for i in range(len(jobs))]
    q_t = [(rt[i] + qy[i][:, :LANES]).astype(BF16) for i in range(len(jobs))]
    for i, (sq, p, cs, rs, ch) in sorted(enumerate(jobs), key=lambda e: e[1][4]):
        st = _split(states[sq, p])
        y_ref[sq, rs, cs] = jnp.dot(q_t[i], st[0], preferred_element_type=F32) + qy[i][:, LANES:]
        states[sq, p] = mm(a_t[i], st) + d_t[i]
    for (sq, p), val in states.items():
        s_ref[sq, p] = val

    @pl.when(c == pl.num_programs(1) - 1)
    def _():
        sT_ref[...] = s_ref[...]


def _rwkv_scan(at, bt, kt, rt, v, plast, s0, state_layer, *, n, t, chunk, nb):
    assert chunk == RWKV_HD and at.shape == (n, t, RWKV_W)
    npair = RWKV_W // LANES
    nch = t // chunk
    nsub = 2 if nch % 2 == 0 else 1
    tok = pl.BlockSpec((nb, nsub * chunk, RWKV_W), lambda b, c: (b, c, 0))
    st = pl.BlockSpec((nb, npair, LANES, LANES), lambda b, c: (b, 0, 0, 0))
    st_in = pl.BlockSpec((None, nb, npair, LANES, LANES), lambda b, c: (state_layer, b, 0, 0, 0))
    y, s_t = pl.pallas_call(
        _rwkv_scan_kernel,
        grid=(n // nb, nch // nsub),
        in_specs=[tok] * 5 + [pl.BlockSpec((nb, nsub, 1, RWKV_W), lambda b, c: (b, c, 0, 0)), st_in],
        out_specs=[tok, st],
        out_shape=[jax.ShapeDtypeStruct((n, t, RWKV_W), F32),
                   jax.ShapeDtypeStruct((n, npair, LANES, LANES), F32)],
        scratch_shapes=[pltpu.VMEM((nb, npair, LANES, LANES), F32)],
        compiler_params=_params("parallel", "arbitrary"),
        name="rwkv_scan",
    )(at, bt, kt, rt, v, plast.reshape(n, nch, 1, RWKV_W), s0)
    return y, s_t


def _merge_kernel(x_ref, yp_ref, ya_ref, ys_ref, bonus_ref, g_ref, gate_ref,
                  pp_ref, pa_ref, pr_ref, wo_ref, lng_ref, lnb_ref, mean_ref, out_ref):
    d = x_ref.shape[1]
    y_attn = ya_ref[...]

    ys = ys_ref[...]
    mu = _head_sums(ys, mean_ref)
    dev = ys - mu
    var = _head_sums(dev * dev, mean_ref)
    yn = dev * lax.rsqrt(var + RWKV_LN_EPS) * lng_ref[...] + lnb_ref[...]
    y_rwkv = (yn + bonus_ref[...]) * g_ref[...]

    gate = lambda c: gate_ref[:, c * d:(c + 1) * d].astype(F32)
    merged = (gate(0) * _bdot(yp_ref[...], pp_ref[...])
              + gate(1) * _bdot(y_attn, pa_ref[...])
              + gate(2) * _bdot(y_rwkv, pr_ref[...]))
    out_ref[...] = x_ref[...] + _bdot(merged, wo_ref[...])


def _merge(x2d, yp, ya, ys, bonus, g, gates, wts, layer, tm):
    m, d = x2d.shape
    row = lambda w: pl.BlockSpec((tm, w), lambda i: (i, 0))
    names = ('proj_pool', 'proj_attn', 'proj_rwkv', 'w_out', 'rwkv_ln_g', 'rwkv_ln_b')
    consts = [wts[k] for k in names] + [wts['mean_bd']]
    return pl.pallas_call(
        _merge_kernel,
        grid=(m // tm,),
        in_specs=[row(d), row(POOL_W), row(LANES)] + [row(RWKV_W)] * 3 + [row(3 * d)]
        + [_wspec(wts[k], layer) for k in names] + [_wspec(wts['mean_bd'])],
        out_specs=row(d),
        out_shape=jax.ShapeDtypeStruct((m, d), F32),
        compiler_params=_params("parallel"),
        name="merge_out_proj",
    )(x2d, yp, ya, ys, bonus, g, gates, *consts)


def _ffn_ple_kernel(x_ref, p_ref, gf_ref, g1_ref, w1_ref, w3_ref, w2_ref, g2_ref, wg_ref, wp_ref,
                    out_ref, *, final_norm):
    x = x_ref[...]
    h = _rms(x, g1_ref[...]).astype(BF16)
    hidden = w1_ref.shape[1]
    cut = pl.cdiv(hidden // MXU_TILE, 2) * MXU_TILE
    y = x
    for cs in (slice(0, cut), slice(cut, hidden)):
        h1 = jnp.dot(h, w1_ref[:, cs], preferred_element_type=F32)
        h3 = jnp.dot(h, w3_ref[:, cs], preferred_element_type=F32)
        y = y + _bdot(h1 * _sigmoid(h1) * h3, w2_ref[cs, :])
    gate = _sigmoid(_bdot(_rms(y, g2_ref[...]), wg_ref[...]))
    y = y + _bdot(p_ref[...], wp_ref[...]) * gate
    if final_norm:
        y = _rms(y, gf_ref[...])
    out_ref[...] = y


def _ffn_ple(x2d, p_all, wts, layer, g_final, tm, final_norm):
    m, d = x2d.shape
    row = pl.BlockSpec((tm, d), lambda i: (i, 0))
    names = ('norm_ffn_g', 'ffn_w1', 'ffn_w3', 'ffn_w2', 'norm_ple_g', 'ple_gate', 'ple_proj')
    return pl.pallas_call(
        functools.partial(_ffn_ple_kernel, final_norm=final_norm),
        grid=(m // tm,),
        in_specs=[row, pl.BlockSpec((None, tm, p_all.shape[2]), lambda i: (layer, i, 0)), _wspec(g_final)]
        + [_wspec(wts[k], layer) for k in names],
        out_specs=row,
        out_shape=jax.ShapeDtypeStruct((m, d), F32),
        compiler_params=_params("parallel"),
        name="swiglu_ple_final" if final_norm else "swiglu_ple",
    )(x2d, p_all, g_final, *[wts[k] for k in names])


def _rope_tables(pos, reps):
    half = ATT_HD // 2
    inv = ROPE_THETA ** (-2.0 * jnp.arange(half, dtype=F32) / ATT_HD)
    ang = pos.astype(F32)[:, None] * inv[None, :]
    cos = jnp.cos(ang)
    sin = jnp.sin(ang)
    cos = jnp.tile(jnp.concatenate([cos, cos], axis=1), (reps, LANES // ATT_HD))
    sin = jnp.tile(jnp.concatenate([-sin, sin], axis=1), (reps, LANES // ATT_HD))
    return cos, sin


def _block_diag(blocks):
    g, a, b = blocks.shape[-3:]
    rows = []
    for i in range(g):
        parts = [blocks[..., i, :, :] if j == i else jnp.zeros_like(blocks[..., i, :, :]) for j in range(g)]
        rows.append(jnp.concatenate(parts, axis=-1))
    return jnp.concatenate(rows, axis=-2)


def _stacked_weights(w):
    row = lambda a: a.reshape(a.shape[0], 1, -1)
    bf = lambda a: a.astype(BF16)
    heads = RWKV_W // RWKV_HD
    zero = jnp.zeros_like(w['rwkv_w2'])
    wwa = jnp.concatenate([jnp.concatenate([w['rwkv_w2'], zero], axis=2),
                           jnp.concatenate([zero, w['rwkv_a2']], axis=2)], axis=1)
    ones = jnp.ones((heads, RWKV_HD, RWKV_HD), F32)
    return {
        'norm_mix_g': row(w['norm_mix_g']), 'w_in': bf(w['w_in']),
        'pool_w_bd': bf(_block_diag(w['pool_w_grp'])), 'pool_scale': row(w['pool_scale']),
        'rwkv_mu': row(w['rwkv_mu']), 'rwkv_wwa': bf(wwa),
        'rwkv_w0': row(w['rwkv_w0']), 'rwkv_a0': row(w['rwkv_a0']), 'rwkv_g2': bf(w['rwkv_g2']),
        'rwkv_k_k': row(w['rwkv_k_k']), 'rwkv_k_a': row(w['rwkv_k_a']), 'rwkv_r_k': row(w['rwkv_r_k']),
        'rwkv_ln_g': row(w['rwkv_ln_g']), 'rwkv_ln_b': row(w['rwkv_ln_b']),
        'ones_bd': bf(_block_diag(ones)), 'mean_bd': bf(_block_diag(ones / RWKV_HD)),
        'proj_pool': bf(w['proj_pool']), 'proj_attn': bf(w['proj_attn']),
        'proj_rwkv': bf(w['proj_rwkv']), 'w_out': bf(w['w_out']),
        'norm_ffn_g': row(w['norm_ffn_g']), 'ffn_w1': bf(w['ffn_w1']), 'ffn_w3': bf(w['ffn_w3']),
        'ffn_w2': bf(w['ffn_w2']), 'norm_ple_g': row(w['norm_ple_g']),
        'ple_proj': bf(w['ple_proj']), 'ple_gate': bf(w['ple_gate']),
    }


def _token_tiles(m):
    tm = next(c for c in (512, 256, 128, m) if m % c == 0)
    return tm, tm


def _pair_states(s):
    st = jnp.swapaxes(s, -1, -2)
    st = st.reshape(st.shape[:-3] + (st.shape[-3] // 2, 2) + st.shape[-2:])
    return _block_diag(st)


def _unpair_states(s2):
    half = s2.shape[-1] // 2
    s = jnp.stack([s2[..., :half, :half], s2[..., half:, half:]], axis=-3)
    s = s.reshape(s.shape[:-4] + (2 * s.shape[-4],) + s.shape[-2:])
    return jnp.swapaxes(s, -1, -2)


def _decoder_layer(x2d, n, t, p_all, wts, layer, g_final, final_norm, tabs, state, *, prompt, tm, tm_in, chunk):
    sl = state['layer']
    three = lambda a: a.reshape(n, t, a.shape[-1])

    if prompt:
        yp, q, k, v, zr, gates, tails = _in_proj(x2d, wts, layer, tabs[0], tabs[1], tm_in, seq_len=t)
        per_seq = t // tm_in
        new_pool = tails.reshape(n, per_seq, POOL_HALO, POOL_W)[:, -1, POOL_HALO - POOL_HIST:]
    else:
        zp, q, k, v, zr, gates = _in_proj(x2d, wts, layer, tabs[0], tabs[1], tm_in)
        zp3 = three(zp)
        yp = _pool(state['pool16'], zp3, wts, layer, pos0=state['pos0'])
        new_pool = jnp.concatenate([state['pool16'][sl], zp3], axis=1)[:, -POOL_HIST:]
    q3, k3, v3, zr3 = three(q), three(k), three(v), three(zr)

    if prompt:
        ya = _dil_attn(q3, k3, v3)
        new_kv = []
        for gi, (win, dil) in enumerate(DIL_GROUPS):
            cols = slice(gi * ATT_PAIR, (gi + 1) * ATT_PAIR)
            keep = min(win, t)
            kv = jnp.concatenate([k3[:, t - keep:, cols], v3[:, t - keep:, cols]], axis=-1)
            new_kv.append(kv.reshape(n, keep, 2, 2, ATT_HD))
    else:
        ya, new_kv = _sample_attn(q3, k3, v3, state['kv_t'], state['kv_acc'], sl)

    if prompt:
        at, bt, kt, rt, vv, plast, g, bonus = _rwkv_prep(zr3, state['shift'], sl, wts, layer,
                                                         bn=1, tm=tm, chunk=chunk, t_out=t)
        ys, s_t = _rwkv_scan(at, bt, kt, rt, vv, plast, state['wkv'], sl, n=n, t=t, chunk=chunk, nb=n)
    else:
        at, bt, kt, rt, vv, plast, g, bonus = _rwkv_prep(zr3, state['shift'], sl, wts, layer,
                                                         bn=n, tm=t, chunk=t, t_out=chunk)
        ys, s_t = _rwkv_scan(at, bt, kt, rt, vv, plast, state['wkv'], sl,
                             n=n, t=chunk, chunk=chunk, nb=4 if n % 4 == 0 else 1)
        ys = ys[:, :t]
    new_shift = zr3[:, t - 1]

    x2d = _merge(x2d, yp.reshape(n * t, POOL_W), ya.reshape(n * t, LANES), ys.reshape(n * t, RWKV_W),
                 bonus, g, gates, wts, layer, tm)
    x2d = _ffn_ple(x2d, p_all, wts, layer, g_final, tm, final_norm)
    return x2d, (new_pool, new_shift, s_t, new_kv)


def kernel(x_prompt, x_sample, state_pool, state_shift, state_wkv, cache_kv_w128, cache_kv_w512,
           cache_kv_w2048, p_prompt, p_sample, norm_mix_g, w_in, pool_w_grp, pool_scale, rwkv_mu,
           rwkv_w0, rwkv_w2, rwkv_a0, rwkv_a2, rwkv_g2, rwkv_k_k, rwkv_k_a, rwkv_r_k, rwkv_ln_g,
           rwkv_ln_b, proj_pool, proj_attn, proj_rwkv, w_out, norm_ffn_g, ffn_w1, ffn_w3, ffn_w2,
           norm_ple_g, ple_proj, ple_gate, norm_final_g):
    weights = dict(norm_mix_g=norm_mix_g, w_in=w_in, pool_w_grp=pool_w_grp, pool_scale=pool_scale,
                   rwkv_mu=rwkv_mu, rwkv_w0=rwkv_w0, rwkv_w2=rwkv_w2, rwkv_a0=rwkv_a0, rwkv_a2=rwkv_a2,
                   rwkv_g2=rwkv_g2, rwkv_k_k=rwkv_k_k, rwkv_k_a=rwkv_k_a, rwkv_r_k=rwkv_r_k,
                   rwkv_ln_g=rwkv_ln_g, rwkv_ln_b=rwkv_ln_b, proj_pool=proj_pool, proj_attn=proj_attn,
                   proj_rwkv=proj_rwkv, w_out=w_out, norm_ffn_g=norm_ffn_g, ffn_w1=ffn_w1, ffn_w3=ffn_w3,
                   ffn_w2=ffn_w2, norm_ple_g=norm_ple_g, ple_proj=ple_proj, ple_gate=ple_gate)
    depth = w_in.shape[0]
    np_, tp, d = x_prompt.shape
    ns, ts, _ = x_sample.shape
    caches = (cache_kv_w128, cache_kv_w512, cache_kv_w2048)
    past_len = PAST_LEN
    tm_p, tm_in_p = _token_tiles(np_ * tp)
    tm_s = ns * ts
    chunk = 64
    tabs_p = _rope_tables(jnp.arange(tp), 1)
    tabs_s = _rope_tables(past_len + jnp.arange(ts), ns)
    g_final = norm_final_g[None, :]
    wts = _stacked_weights(weights)

    xp = x_prompt.reshape(np_ * tp, d)
    xs = x_sample.reshape(ns * ts, d)
    pp_all = p_prompt.reshape(depth, np_ * tp, -1)
    ps_all = p_sample.reshape(depth, ns * ts, -1)
    heads = RWKV_W // RWKV_HD
    state_p = {
        'layer': 0,
        'shift': jnp.zeros((1, np_, 1, RWKV_PROJ), F32),
        'wkv': jnp.zeros((1, np_, heads // 2, 2 * RWKV_HD, 2 * RWKV_HD), F32),
    }
    caches_t = [jnp.transpose(c, (0, 1, 3, 4, 5, 2)).reshape(c.shape[0], c.shape[1], -1, c.shape[2])
                for c in caches]
    kv_acc = [jnp.zeros(c.shape, F32) for c in caches_t]
    state_s = {
        'pool16': jnp.pad(state_pool, ((0, 0), (0, 0), (POOL_HALO - POOL_HIST, 0), (0, 0))),
        'shift': state_shift[:, :, None, :], 'wkv': _pair_states(state_wkv),
        'kv_t': caches_t, 'pos0': past_len,
    }
    outs_p, outs_s = [], []
    for i in range(depth):
        last = i == depth - 1
        xp, st_p = _decoder_layer(xp, np_, tp, pp_all, wts, i, g_final, last, tabs_p, state_p,
                                  prompt=True, tm=tm_p, tm_in=tm_in_p, chunk=chunk)
        xs, st_s = _decoder_layer(xs, ns, ts, ps_all, wts, i, g_final, last, tabs_s,
                                  dict(state_s, layer=i, kv_acc=kv_acc),
                                  prompt=False, tm=tm_s, tm_in=tm_s, chunk=chunk)
        kv_acc = st_s[3]
        outs_p.append(st_p)
        outs_s.append(st_s)

    stack = lambda outs, f: jnp.stack([f(o) for o in outs])
    res = [xp.reshape(np_, tp, d), xs.reshape(ns, ts, d)]
    for idx in range(2):
        res.append(stack(outs_p, lambda o: o[idx]))
        res.append(stack(outs_s, lambda o: o[idx]))
    res.append(_unpair_states(stack(outs_p, lambda o: o[2])))
    res.append(_unpair_states(stack(outs_s, lambda o: o[2])))
    for gi in range(len(DIL_GROUPS)):
        res.append(stack(outs_p, lambda o: o[3][gi]))
        a = kv_acc[gi]
        a = a.reshape(a.shape[0], a.shape[1], 2, 2, ATT_HD, a.shape[3])
        res.append(jnp.transpose(a, (0, 1, 5, 2, 3, 4)))
    return tuple(res)
```

```python
import functools
import math

import jax
import jax.numpy as jnp
from jax import lax
from jax.experimental import pallas as pl
from jax.experimental.pallas import tpu as pltpu

F32 = jnp.float32
BF16 = jnp.bfloat16

POOL_GC = 64
POOL_W = 256
POOL_WINDOWS = (2, 4, 8, 16)
POOL_HIST = 15
POOL_HALO = 16
ATT_HD = 64
ATT_W = 384
ATT_PAIR = 128
DIL_GROUPS = ((128, 1), (512, 4), (2048, 16))
BAND_BLK = 128
ROPE_THETA = 10000.0
RWKV_HD = 64
RWKV_W = 384
RWKV_PROJ = 1408
RWKV_LN_EPS = 64e-5
RMS_EPS = 1e-6
PAST_LEN = 8192
NEG_BIG = -1e30

LANES = 128
SUBLANES = 8
MXU_TILE = 256
VMEM_LIMIT = 56 * 1024 * 1024

NN = (((1,), (0,)), ((), ()))
NT = (((1,), (1,)), ((), ()))
TN = (((0,), (0,)), ((), ()))


def _params(*sem):
    return pltpu.CompilerParams(dimension_semantics=sem, vmem_limit_bytes=VMEM_LIMIT)


def _wspec(arr, layer=None):
    if layer is None:
        nd = arr.ndim
        return pl.BlockSpec(arr.shape, lambda *_: (0,) * nd, pipeline_mode=pl.Buffered(1))
    nd = arr.ndim - 1
    return pl.BlockSpec((None,) + arr.shape[1:], lambda *_: (layer,) + (0,) * nd,
                        pipeline_mode=pl.Buffered(1))


def _bdot(a, b, dims=NN):
    return lax.dot_general(a.astype(BF16), b.astype(BF16), dims, preferred_element_type=F32)


def _split(a):
    hi = a.astype(BF16)
    lo = (a - hi.astype(F32)).astype(BF16)
    return hi, lo


def _mm3(a, b, dims=NN):
    ah, al = _split(a)
    bh, bl = _split(b)
    f = lambda x, y: lax.dot_general(x, y, dims, preferred_element_type=F32)
    return f(ah, bh) + (f(ah, bl) + f(al, bh))


def _mm2_exact_rhs(a, b_bf16, dims=NN):
    ah, al = _split(a)
    f = lambda x: lax.dot_general(x, b_bf16, dims, preferred_element_type=F32)
    return f(ah) + f(al)


def _head_sums(x, bd_ref):
    w = x.shape[1]
    cuts = [c for c in range(0, w, MXU_TILE)] + [w]
    parts = [_mm2_exact_rhs(x[:, a:b], bd_ref[a:b, a:b]) for a, b in zip(cuts[:-1], cuts[1:])]
    return jnp.concatenate(parts, axis=1)


def _rms(x, g):
    return x * lax.rsqrt(jnp.mean(x * x, axis=-1, keepdims=True) + RMS_EPS) * g


def _sigmoid(x):
    return 1.0 / (1.0 + jnp.exp(-x))


def _pool_from_ext(ext_ref, cur, pos, w_bd, scale):
    bn, tm, w = cur.shape
    lane = lax.broadcasted_iota(jnp.int32, (bn, tm, w), 2)
    group = jnp.right_shift(lane, POOL_GC.bit_length() - 1)
    acc = cur
    win = jnp.zeros_like(cur)
    for s in range(1, POOL_WINDOWS[-1] + 1):
        if s in POOL_WINDOWS:
            win = jnp.where(group == POOL_WINDOWS.index(s), acc, win)
        if s < POOL_WINDOWS[-1]:
            acc = acc + ext_ref[:, POOL_HALO - s:POOL_HALO - s + tm, :]
    cnt = jnp.minimum(pos + 1, jnp.left_shift(2, group)).astype(F32)
    dlt = (win / cnt - cur).reshape(bn * tm, w)
    return _bdot(dlt, w_bd) * scale


def _in_proj_kernel(x_ref, g_ref, w_ref, cos_ref, sin_ref, *rest, tiles_per_seq):
    if tiles_per_seq is None:
        pool_ref, q_ref, k_ref, v_ref, rw_ref, gate_ref = rest
    else:
        pw_ref, ps_ref, pool_ref, q_ref, k_ref, v_ref, rw_ref, gate_ref, tail_ref, ext_ref = rest

        @pl.when(pl.program_id(0) == 0)
        def _():
            ext_ref[...] = jnp.zeros(ext_ref.shape, F32)

    h = _rms(x_ref[...], g_ref[...]).astype(BF16)

    def seg(a, b):
        return jnp.dot(h, w_ref[:, a:b], preferred_element_type=F32)

    cos = cos_ref[...]
    sin = sin_ref[...]
    lane = lax.broadcasted_iota(jnp.int32, cos.shape, 1)
    low_half = (lane & 32) == 0

    def rope(t):
        partner = jnp.where(low_half, pltpu.roll(t, 96, 1), pltpu.roll(t, 32, 1))
        return t * cos + partner * sin

    n_mix = POOL_W + 3 * ATT_W + RWKV_PROJ
    mix = seg(0, n_mix)
    o = 0
    zp = mix[:, o:o + POOL_W]
    if tiles_per_seq is None:
        pool_ref[...] = zp
    else:
        tm = zp.shape[0]
        tile = pl.program_id(0) % tiles_per_seq
        halo = jnp.where(tile == 0, 0.0, ext_ref[:, tm:, :])
        ext_ref[:, :POOL_HALO, :] = halo
        ext_ref[:, POOL_HALO:, :] = zp[None]
        pos = tile * tm + lax.broadcasted_iota(jnp.int32, (1, tm, POOL_W), 1)
        pool_ref[...] = _pool_from_ext(ext_ref, zp[None], pos, pw_ref[...], ps_ref[...])
        tail_ref[...] = zp[None, tm - POOL_HALO:, :]
    o += POOL_W
    for c in range(ATT_W // LANES):
        cs = slice(c * LANES, (c + 1) * LANES)
        q_ref[:, cs] = rope(mix[:, o + c * LANES:o + (c + 1) * LANES]) * (ATT_HD ** -0.5)
        k_ref[:, cs] = rope(mix[:, o + ATT_W + c * LANES:o + ATT_W + (c + 1) * LANES])
    o += 2 * ATT_W
    v_ref[...] = mix[:, o:o + ATT_W]
    o += ATT_W
    rw_ref[...] = mix[:, o:o + RWKV_PROJ]
    gate_ref[...] = _sigmoid(seg(n_mix, w_ref.shape[1])).astype(gate_ref.dtype)


def _in_proj(x2d, wts, layer, cos, sin, tm, seq_len=None):
    m, d = x2d.shape
    g, w_bf = wts['norm_mix_g'], wts['w_in']
    ncol = w_bf.shape[-1]
    ngate = ncol - (POOL_W + 3 * ATT_W + RWKV_PROJ)
    ntab = cos.shape[0] // tm
    row = lambda w: pl.BlockSpec((tm, w), lambda i: (i, 0))
    tab = pl.BlockSpec((tm, LANES), lambda i: (i % ntab, 0))
    widths = (POOL_W, ATT_W, ATT_W, ATT_W, RWKV_PROJ, ngate)
    dtypes = (F32,) * 5 + (BF16,)
    in_specs = [row(d), _wspec(g, layer), _wspec(w_bf, layer), tab, tab]
    args = [x2d, g, w_bf, cos, sin]
    out_specs = [row(w) for w in widths]
    out_shape = [jax.ShapeDtypeStruct((m, w), dt) for w, dt in zip(widths, dtypes)]
    scratch = []
    if seq_len is not None:
        assert seq_len % tm == 0 and tm >= POOL_HALO
        in_specs += [_wspec(wts['pool_w_bd'], layer), _wspec(wts['pool_scale'], layer)]
        args += [wts['pool_w_bd'], wts['pool_scale']]
        out_specs.append(pl.BlockSpec((1, POOL_HALO, POOL_W), lambda i: (i, 0, 0)))
        out_shape.append(jax.ShapeDtypeStruct((m // tm, POOL_HALO, POOL_W), F32))
        scratch.append(pltpu.VMEM((1, tm + POOL_HALO, POOL_W), F32))
    return pl.pallas_call(
        functools.partial(_in_proj_kernel, tiles_per_seq=None if seq_len is None else seq_len // tm),
        grid=(m // tm,),
        in_specs=in_specs,
        out_specs=out_specs,
        out_shape=out_shape,
        scratch_shapes=scratch,
        compiler_params=_params("parallel" if seq_len is None else "arbitrary"),
        name="in_proj",
    )(*args)


def _pool_kernel(prev_ref, cur_ref, w_ref, scale_ref, y_ref, ext_ref, *, pos0):
    bn, tm, w = cur_ref.shape
    cur = cur_ref[...]
    ext_ref[:, :POOL_HALO, :] = prev_ref[...]
    ext_ref[:, POOL_HALO:, :] = cur
    pos = pos0 + lax.broadcasted_iota(jnp.int32, (bn, tm, w), 1)
    y_ref[...] = _pool_from_ext(ext_ref, cur, pos, w_ref[...], scale_ref[...]).reshape(bn, tm, w)


def _pool(hist, cur3, wts, layer, *, pos0):
    n, t, w = cur3.shape
    w_bd, scale = wts['pool_w_bd'], wts['pool_scale']
    return pl.pallas_call(
        functools.partial(_pool_kernel, pos0=pos0),
        grid=(1,),
        in_specs=[pl.BlockSpec((None, n, POOL_HALO, w), lambda b: (layer, 0, 0, 0)),
                  pl.BlockSpec((n, t, w), lambda b: (0, 0, 0)),
                  _wspec(w_bd, layer), _wspec(scale, layer)],
        out_specs=pl.BlockSpec((n, t, w), lambda b: (0, 0, 0)),
        out_shape=jax.ShapeDtypeStruct((n, t, w), F32),
        scratch_shapes=[pltpu.VMEM((n, t + POOL_HALO, w), F32)],
        compiler_params=_params("arbitrary"),
        name="pool_mix",
    )(hist, cur3, w_bd, scale)


def _softmax_pair(q, k, v, valid):
    m_rows = q.shape[0]
    lane = lax.broadcasted_iota(jnp.int32, (m_rows, ATT_PAIR), 1)
    head0 = lane < ATT_HD
    kb = k.astype(BF16)
    vb = v.astype(BF16)
    outs, lses = [], []
    for hs in range(2):
        hm = head0 if hs == 0 else jnp.logical_not(head0)
        qm = jnp.where(hm, q, 0.0).astype(BF16)
        s = lax.dot_general(qm, kb, NT, preferred_element_type=F32)
        s = jnp.where(valid, s, NEG_BIG)
        mx = jnp.max(s, axis=-1, keepdims=True)
        e = jnp.exp(s - mx)
        den = jnp.sum(e, axis=-1, keepdims=True)
        o = jnp.dot(e.astype(BF16), vb, preferred_element_type=F32) / den
        outs.append(o)
        lses.append(jnp.broadcast_to(mx + jnp.log(den), o.shape))
    return jnp.where(head0, outs[0], outs[1]), jnp.where(head0, lses[0], lses[1])


def _merge_groups(outs, lses):
    mx = functools.reduce(jnp.maximum, lses)
    es = [jnp.exp(l - mx) for l in lses]
    num = functools.reduce(lambda a, b: a + b, [e * o for e, o in zip(es, outs)])
    return num / functools.reduce(lambda a, b: a + b, es)


def _rows(start, size, stride):
    return pl.ds(start, size, stride=stride) if stride > 1 else pl.ds(start, size)


def _dil_attn_kernel(*refs):
    ng = len(DIL_GROUPS)
    y_ref, o_scr, l_scr = refs[5 * ng:]
    j = pl.program_id(1)
    tq = y_ref.shape[1]
    qi = lax.broadcasted_iota(jnp.int32, (BAND_BLK, 2 * BAND_BLK), 0)
    kj = lax.broadcasted_iota(jnp.int32, (BAND_BLK, 2 * BAND_BLK), 1)
    dist = BAND_BLK + qi - kj
    has_prev = kj + jnp.where(j > 0, BAND_BLK, 0) >= BAND_BLK
    for gi, (win, dil) in enumerate(DIL_GROUPS):
        band = (dist >= 0) & (dist <= win // dil)
        step = BAND_BLK * dil
        q_ref, k_ref, v_ref, kp_ref, vp_ref = refs[5 * gi:5 * gi + 5]
        for r in range(dil):
            for sb in range(tq // step):
                rows = _rows(r + sb * step, BAND_BLK, dil)
                if sb == 0:
                    prow = _rows(r, BAND_BLK, dil)
                    k_prev, v_prev = kp_ref[0, prow, :], vp_ref[0, prow, :]
                    valid = band & has_prev
                else:
                    prow = _rows(r + (sb - 1) * step, BAND_BLK, dil)
                    k_prev, v_prev = k_ref[0, prow, :], v_ref[0, prow, :]
                    valid = band
                o, lse = _softmax_pair(q_ref[0, rows, :],
                                       jnp.concatenate([k_prev, k_ref[0, rows, :]], axis=0),
                                       jnp.concatenate([v_prev, v_ref[0, rows, :]], axis=0), valid)
                o_scr[gi, rows, :] = o
                l_scr[gi, rows, :] = lse
    y_ref[0] = _merge_groups([o_scr[g] for g in range(ng)], [l_scr[g] for g in range(ng)])


def _dil_attn(q3, k3, v3):
    n, t, w = q3.shape
    tq = BAND_BLK * max(d for _, d in DIL_GROUPS)
    assert t % tq == 0
    specs, args = [], []
    for gi, (_, dil) in enumerate(DIL_GROUPS):
        per = tq // (BAND_BLK * dil)
        cur = pl.BlockSpec((1, tq, LANES), lambda b, j, gi=gi: (b, j, gi))
        prev = pl.BlockSpec((1, BAND_BLK * dil, LANES),
                            lambda b, j, per=per, gi=gi: (b, jnp.maximum(j * per - 1, 0), gi))
        specs += [cur, cur, cur, prev, prev]
        args += [q3, k3, v3, k3, v3]
    ng = len(DIL_GROUPS)
    return pl.pallas_call(
        _dil_attn_kernel,
        grid=(n, t // tq),
        in_specs=specs,
        out_specs=pl.BlockSpec((1, tq, LANES), lambda b, j: (b, j, 0)),
        out_shape=jax.ShapeDtypeStruct((n, t, LANES), F32),
        scratch_shapes=[pltpu.VMEM((ng, tq, LANES), F32), pltpu.VMEM((ng, tq, LANES), F32)],
        compiler_params=_params("parallel", "parallel"),
        name="dil_attn",
    )(*args)


def _sample_attn_kernel(q_ref, k_ref, v_ref, *rest):
    ng = len(DIL_GROUPS)
    cache_refs, y_ref, new_refs = rest[:ng], rest[2 * ng], rest[2 * ng + 1:]
    t = q_ref.shape[1]
    mq = 2 * SUBLANES
    zpad = jnp.zeros((mq - t, ATT_PAIR), F32)
    lane = lax.broadcasted_iota(jnp.int32, (mq, ATT_PAIR), 1)
    head0 = lane < ATT_HD
    tn = lax.broadcasted_iota(jnp.int32, (mq, mq), 0)
    jn = lax.broadcasted_iota(jnp.int32, (mq, mq), 1)
    outs, lses = [], []
    for gi, (win, dil) in enumerate(DIL_GROUPS):
        reach = dil * (win // dil)
        cols = slice(gi * ATT_PAIR, (gi + 1) * ATT_PAIR)
        ct = cache_refs[gi][0, 0]
        hist = ct.shape[1]
        kt2 = ct[:ATT_PAIR].astype(BF16)
        vt2 = ct[ATT_PAIR:].astype(BF16)
        k_new, v_new = k_ref[0, :, cols], v_ref[0, :, cols]
        q = jnp.concatenate([q_ref[0, :, cols], zpad], axis=0)
        kn = jnp.concatenate([k_new, zpad], axis=0).astype(BF16)
        vn = jnp.concatenate([v_new, zpad], axis=0).astype(BF16)
        ti = lax.broadcasted_iota(jnp.int32, (mq, hist), 0)
        pj = lax.broadcasted_iota(jnp.int32, (mq, hist), 1)
        d_old = hist + ti - pj
        ok_old = (d_old <= reach) & ((d_old & (dil - 1)) == 0)
        d_new = tn - jn
        ok_new = (d_new >= 0) & (d_new <= reach) & ((d_new & (dil - 1)) == 0) & (jn < t)
        o_h, l_h = [], []
        for hs in range(2):
            hm = head0 if hs == 0 else jnp.logical_not(head0)
            qm = jnp.where(hm, q, 0.0).astype(BF16)
            s_old = jnp.where(ok_old, jnp.dot(qm, kt2, preferred_element_type=F32), NEG_BIG)
            s_new = jnp.where(ok_new, lax.dot_general(qm, kn, NT, preferred_element_type=F32), NEG_BIG)
            mx = jnp.maximum(jnp.max(s_old, axis=-1, keepdims=True), jnp.max(s_new, axis=-1, keepdims=True))
            e_old = jnp.exp(s_old - mx)
            e_new = jnp.exp(s_new - mx)
            den = jnp.sum(e_old, axis=-1, keepdims=True) + jnp.sum(e_new, axis=-1, keepdims=True)
            num = (lax.dot_general(e_old.astype(BF16), vt2, NT, preferred_element_type=F32)
                   + jnp.dot(e_new.astype(BF16), vn, preferred_element_type=F32))
            o_h.append(num / den)
            l_h.append(jnp.broadcast_to(mx + jnp.log(den), num.shape))
        outs.append(jnp.where(head0, o_h[0], o_h[1])[:t])
        lses.append(jnp.where(head0, l_h[0], l_h[1])[:t])

        shifted = pltpu.roll(ct, hist - t, 1)
        fresh = jnp.concatenate([jnp.zeros((LANES - t, 2 * ATT_PAIR), F32),
                                 jnp.concatenate([k_new, v_new], axis=1)], axis=0).T
        lane_c = lax.broadcasted_iota(jnp.int32, (2 * ATT_PAIR, LANES), 1)
        if hist > LANES:
            new_refs[gi][0, 0, :, :hist - LANES] = shifted[:, :hist - LANES]
        new_refs[gi][0, 0, :, hist - LANES:] = jnp.where(lane_c >= LANES - t, fresh, shifted[:, hist - LANES:])
    y_ref[0] = _merge_groups(outs, lses)


def _sample_attn(q3, k3, v3, caches_t, accs, layer):
    n, t, w = q3.shape
    ng = len(DIL_GROUPS)
    for (win, dil), c in zip(DIL_GROUPS, caches_t):
        assert c.shape[3] == win and win % LANES == 0 and dil & (dil - 1) == 0 and t <= SUBLANES
    new = pl.BlockSpec((1, t, w), lambda b: (b, 0, 0))
    blk = lambda c: pl.BlockSpec((1, 1) + c.shape[2:], lambda b: (layer, b, 0, 0))
    res = pl.pallas_call(
        _sample_attn_kernel,
        grid=(n,),
        in_specs=[new, new, new] + [blk(c) for c in caches_t]
        + [pl.BlockSpec(memory_space=pl.ANY)] * ng,
        out_specs=[pl.BlockSpec((1, t, LANES), lambda b: (b, 0, 0))] + [blk(c) for c in accs],
        out_shape=[jax.ShapeDtypeStruct((n, t, LANES), F32)]
        + [jax.ShapeDtypeStruct(a.shape, F32) for a in accs],
        input_output_aliases={3 + ng + g: 1 + g for g in range(ng)},
        compiler_params=_params("parallel"),
        name="sample_attn",
    )(q3, k3, v3, *caches_t, *accs)
    return res[0], list(res[1:])


def _rwkv_prep_kernel(prev8_ref, sp_ref, cur_ref, mu_ref, wwa_ref, w0_ref, a0_ref, g2_ref,
                      kk_ref, ka_ref, rk_ref, ones_ref, tril_ref,
                      at_ref, bt_ref, kt_ref, rt_ref, v_ref, pl_ref, g_ref, bonus_ref,
                      sh_ref, p_ref, *, chunk):
    bn, tm, w = cur_ref.shape
    j = pl.program_id(1)
    cur = cur_ref[...]
    prev_row = jnp.where(j == 0, sp_ref[...], prev8_ref[:, SUBLANES - 1:SUBLANES, :])
    sh_ref[:, SUBLANES - 1:SUBLANES, :] = prev_row
    sh_ref[:, SUBLANES:, :] = cur
    prev = sh_ref[:, SUBLANES - 1:SUBLANES - 1 + tm, :]
    m = bn * tm
    zs = (cur + (prev - cur) * mu_ref[...]).reshape(m, w)

    r = zs[:, 0:RWKV_W]
    k = zs[:, RWKV_W:2 * RWKV_W]
    v = zs[:, 2 * RWKV_W:3 * RWKV_W]
    lo = 3 * RWKV_W
    z_wa = zs[:, lo:lo + LANES]
    z_g = zs[:, lo + LANES:lo + 2 * LANES]
    lane = lax.broadcasted_iota(jnp.int32, z_wa.shape, 1)
    u = jnp.where(lane < LANES // 2, jnp.tanh(z_wa), z_wa)
    lora = _bdot(u, wwa_ref[...])
    xw = w0_ref[...] + lora[:, :RWKV_W]
    w_log = -(jnp.maximum(-xw, 0.0) + jnp.log(1.0 + jnp.exp(-jnp.abs(xw)))) - 0.5
    e = jnp.exp(w_log)
    a = _sigmoid(a0_ref[...] + lora[:, RWKV_W:])
    g_ref[...] = _bdot(_sigmoid(z_g), g2_ref[...])

    kk = k * kk_ref[...]
    kk = kk * lax.rsqrt(jnp.maximum(_head_sums(kk * kk, ones_ref), 1e-24))
    k = k * (1.0 + (a - 1.0) * ka_ref[...])
    bonus_ref[...] = _head_sums(r * k * rk_ref[...], ones_ref) * v

    tril = tril_ref[...]
    tb = tril.shape[0]
    e1 = e.astype(BF16)
    rem = e - e1.astype(F32)
    e2 = rem.astype(BF16)
    e3 = (rem - e2.astype(F32)).astype(BF16)
    e123 = jnp.concatenate([e1, e2, e3], axis=1)
    cum3 = jnp.concatenate([jnp.dot(tril, e123[b * tb:(b + 1) * tb], preferred_element_type=F32)
                            for b in range(m // tb)], axis=0)
    cum = cum3[:, :RWKV_W] + (cum3[:, RWKV_W:2 * RWKV_W] + cum3[:, 2 * RWKV_W:])
    p_inc = jnp.exp(-cum)
    p_inv = jnp.exp(cum)

    def put(ref, val):
        ref[:, :tm, :] = val.reshape(bn, tm, RWKV_W)
        if ref.shape[1] > tm:
            ref[:, tm:, :] = jnp.zeros((bn, ref.shape[1] - tm, RWKV_W), F32)

    put(at_ref, -kk * jnp.exp(e - cum))
    put(bt_ref, kk * a * p_inv)
    put(kt_ref, k * p_inv)
    put(rt_ref, r * p_inc)
    put(v_ref, v)
    p_ref[...] = p_inc
    for c in range(m // chunk):
        pl_ref[c] = p_ref[(c + 1) * chunk - 1:(c + 1) * chunk, :]


def _rwkv_prep(zr3, shift_prev, state_layer, wts, layer, *, bn, tm, chunk, t_out):
    n, t, w = zr3.shape
    m = bn * tm
    per = tm // SUBLANES
    assert t_out == t or t == tm
    rowblk = pl.BlockSpec((m, RWKV_W), lambda b, j: (b * (t // tm) + j, 0))
    scanblk = pl.BlockSpec((bn, tm if t_out == t else t_out, RWKV_W), lambda b, j: (b, j, 0))
    tb = min(m, MXU_TILE)
    assert tb % chunk == 0 and m % tb == 0
    tril = (jnp.arange(tb)[:, None] >= jnp.arange(tb)[None, :]) & \
           (jnp.arange(tb)[:, None] // chunk == jnp.arange(tb)[None, :] // chunk)
    nch = m // chunk
    names = ('rwkv_mu', 'rwkv_wwa', 'rwkv_w0', 'rwkv_a0', 'rwkv_g2', 'rwkv_k_k', 'rwkv_k_a', 'rwkv_r_k')
    consts = [wts[k] for k in names] + [wts['ones_bd'], tril.astype(BF16)]
    const_specs = [_wspec(wts[k], layer) for k in names] + [_wspec(wts['ones_bd']), _wspec(consts[-1])]
    outs = pl.pallas_call(
        functools.partial(_rwkv_prep_kernel, chunk=chunk),
        grid=(n // bn, t // tm),
        in_specs=[pl.BlockSpec((bn, SUBLANES, w), lambda b, j: (b, jnp.maximum(j * per - 1, 0), 0)),
                  pl.BlockSpec((None, bn, 1, w), lambda b, j: (state_layer, b, 0, 0)),
                  pl.BlockSpec((bn, tm, w), lambda b, j: (b, j, 0))]
        + const_specs,
        out_specs=[scanblk] * 5
        + [pl.BlockSpec((nch, 1, RWKV_W), lambda b, j: (b * (t // tm) + j, 0, 0)), rowblk, rowblk],
        out_shape=[jax.ShapeDtypeStruct((n, t_out, RWKV_W), F32)] * 5
        + [jax.ShapeDtypeStruct((n * t // chunk, 1, RWKV_W), F32)]
        + [jax.ShapeDtypeStruct((n * t, RWKV_W), F32)] * 2,
        scratch_shapes=[pltpu.VMEM((bn, tm + SUBLANES, w), F32), pltpu.VMEM((m, RWKV_W), F32)],
        compiler_params=_params("parallel", "parallel"),
        name="rwkv_prep",
    )(zr3, shift_prev, zr3, *consts)
    return outs


def _rwkv_scan_kernel(at_ref, bt_ref, kt_ref, rt_ref, v_ref, pl_ref, s0_ref, y_ref, sT_ref, s_ref):
    c = pl.program_id(1)
    chunk = RWKV_HD
    nb, nsub = at_ref.shape[0], at_ref.shape[1] // chunk
    npair = RWKV_W // LANES
    assert 2 * chunk == LANES

    @pl.when(c == 0)
    def _():
        s_ref[...] = s0_ref[...]

    lane = lax.broadcasted_iota(jnp.int32, (chunk, LANES), 1)
    head0 = lane < RWKV_HD
    row2 = lax.broadcasted_iota(jnp.int32, (chunk, 2 * LANES), 0)
    col2 = lax.broadcasted_iota(jnp.int32, (chunk, 2 * LANES), 1) & (chunk - 1)
    strict2 = col2 < row2
    incl2 = col2 <= row2
    row1 = lax.broadcasted_iota(jnp.int32, (chunk, LANES), 0)
    eye_ls = jnp.where((lane & (chunk - 1)) == row1, 1.0, 0.0).astype(F32)
    ri = lax.broadcasted_iota(jnp.int32, (LANES, LANES), 0)
    ci = lax.broadcasted_iota(jnp.int32, (LANES, LANES), 1)
    same = (ri < chunk) == (ci < chunk)
    eye = jnp.where(ri == ci, 1.0, 0.0).astype(F32)

    def stack(x):
        return jnp.concatenate([jnp.where(head0, x, 0.0), jnp.where(head0, 0.0, x)], axis=0)

    def bdiag(x):
        return jnp.where(same, jnp.concatenate([x, x], axis=0), 0.0)

    def bdiag_bf(x):
        return bdiag(x).astype(BF16)

    def mm(a, b, dims=NN):
        (ah, al), (bh, bl) = a, b
        lhs = jnp.concatenate([ah, ah, al], axis=0 if dims == TN else 1)
        rhs = jnp.concatenate([bh, bl, bh], axis=1 if dims == NT else 0)
        return lax.dot_general(lhs, rhs, dims, preferred_element_type=F32)

    jobs = [(sq, p, slice(p * LANES, (p + 1) * LANES), slice(ch * chunk, (ch + 1) * chunk), ch)
            for sq in range(nb) for p in range(npair) for ch in range(nsub)]
    states = {(sq, p): s_ref[sq, p] for sq in range(nb) for p in range(npair)}
    at = [at_ref[sq, rs, cs] for sq, p, cs, rs, ch in jobs]
    rt = [rt_ref[sq, rs, cs] for sq, p, cs, rs, ch in jobs]
    vv = [v_ref[sq, rs, cs] for sq, p, cs, rs, ch in jobs]
    gram_a, gram_r = [], []
    for i, (sq, p, cs, rs, ch) in enumerate(jobs):
        bk = _split(jnp.concatenate([stack(bt_ref[sq, rs, cs]), stack(kt_ref[sq, rs, cs])], axis=0))
        gram_a.append(mm(_split(at[i]), bk, NT))
        gram_r.append(lax.dot_general(rt[i].astype(BF16), bk[0], NT, preferred_element_type=F32))
    m_ab = [jnp.where(strict2, g, 0.0)[:, :LANES] for g in gram_a]
    m_ak = [jnp.where(strict2, g, 0.0)[:, LANES:] for g in gram_a]
    m_r = [jnp.where(incl2, g, 0.0).astype(BF16) for g in gram_r]
    zero_bf = jnp.zeros((LANES, LANES), BF16)
    nlev = int(math.log2(chunk))
    npow = [jnp.dot(m.astype(BF16), bdiag_bf(m), preferred_element_type=F32) for m in m_ab]
    inv = [eye_ls + m for m in m_ab]
    for lev in range(1, nlev):
        for i in range(len(jobs)):
            if lev < nlev - 1:
                rhs = jnp.concatenate([bdiag_bf(npow[i]), bdiag_bf(inv[i])], axis=1)
                out = jnp.dot(npow[i].astype(BF16), rhs, preferred_element_type=F32)
                npow[i], inv[i] = out[:, :LANES], inv[i] + out[:, LANES:]
            else:
                inv[i] = inv[i] + jnp.dot(npow[i].astype(BF16), bdiag_bf(inv[i]), preferred_element_type=F32)
    mv = [mm(_split(m), _split(bdiag(x))) for m, x in zip(m_ak, vv)]
    wu = [mm(_split(t), _split(jnp.concatenate([bdiag(x), bdiag(a)], axis=1))) for t, x, a in zip(inv, mv, at)]
    qy = []
    for i in range(len(jobs)):
        w_m, u_m = wu[i][:, :LANES], wu[i][:, LANES:]
        zq = jnp.concatenate([jnp.concatenate([bdiag_bf(u_m), bdiag_bf(w_m)], axis=1),
                              jnp.concatenate([zero_bf, bdiag_bf(vv[i])], axis=1)], axis=0)
        qy.append(jnp.dot(m_r[i], zq, preferred_element_type=F32))
    ad = []
    for i, (sq, p, cs, rs, ch) in enumerate(jobs):
        decay = pl_ref[sq, ch, :, cs]
        bk = jnp.concatenate([bt_ref[sq, rs, cs], kt_ref[sq, rs, cs]], axis=0) * decay
        z = jnp.concatenate([jnp.concatenate([wu[i][:, LANES:], wu[i][:, :LANES]], axis=1),
                             jnp.concatenate([jnp.zeros_like(vv[i]), vv[i]], axis=1)], axis=0)
        ad.append(mm(_split(bk), _split(z), TN))
    a_t = [_split(eye * pl_ref[sq, ch, :, cs] + jnp.where(same, ad[i][:, :LANES], 0.0))
           for i, (sq, p, cs, rs, ch) in enumerate(jobs)]
    d_t = [jnp.where(same, ad[i][:, LANES:], 0.0) for i in range(len(jobs))]
    q_t = [(rt[i] + qy[i][:, :LANES]).astype(BF16) for i in range(len(jobs))]
    for i, (sq, p, cs, rs, ch) in sorted(enumerate(jobs), key=lambda e: e[1][4]):
        st = _split(states[sq, p])
        y_ref[sq, rs, cs] = jnp.dot(q_t[i], st[0], preferred_element_type=F32) + qy[i][:, LANES:]
        states[sq, p] = mm(a_t[i], st) + d_t[i]
    for (sq, p), val in states.items():
        s_ref[sq, p] = val

    @pl.when(c == pl.num_programs(1) - 1)
    def _():
        sT_ref[...] = s_ref[...]


def _rwkv_scan(at, bt, kt, rt, v, plast, s0, state_layer, *, n, t, chunk, nb):
    assert chunk == RWKV_HD and at.shape == (n, t, RWKV_W)
    npair = RWKV_W // LANES
    nch = t // chunk
    nsub = 2 if nch % 2 == 0 else 1
    tok = pl.BlockSpec((nb, nsub * chunk, RWKV_W), lambda b, c: (b, c, 0))
    st = pl.BlockSpec((nb, npair, LANES, LANES), lambda b, c: (b, 0, 0, 0))
    st_in = pl.BlockSpec((None, nb, npair, LANES, LANES), lambda b, c: (state_layer, b, 0, 0, 0))
    y, s_t = pl.pallas_call(
        _rwkv_scan_kernel,
        grid=(n // nb, nch // nsub),
        in_specs=[tok] * 5 + [pl.BlockSpec((nb, nsub, 1, RWKV_W), lambda b, c: (b, c, 0, 0)), st_in],
        out_specs=[tok, st],
        out_shape=[jax.ShapeDtypeStruct((n, t, RWKV_W), F32),
                   jax.ShapeDtypeStruct((n, npair, LANES, LANES), F32)],
        scratch_shapes=[pltpu.VMEM((nb, npair, LANES, LANES), F32)],
        compiler_params=_params("parallel", "arbitrary"),
        name="rwkv_scan",
    )(at, bt, kt, rt, v, plast.reshape(n, nch, 1, RWKV_W), s0)
    return y, s_t


def _post_mix_kernel(x_ref, yp_ref, ya_ref, ys_ref, bonus_ref, g_ref, gate_ref, p_ref, gf_ref,
                     pp_ref, pa_ref, pr_ref, wo_ref, lng_ref, lnb_ref, mean_ref,
                     g1_ref, w1_ref, w3_ref, w2_ref, g2_ref, wg_ref, wp_ref, out_ref, *, final_norm):
    d = x_ref.shape[1]
    ys = ys_ref[...]
    mu = _head_sums(ys, mean_ref)
    dev = ys - mu
    var = _head_sums(dev * dev, mean_ref)
    yn = dev * lax.rsqrt(var + RWKV_LN_EPS) * lng_ref[...] + lnb_ref[...]
    y_rwkv = (yn + bonus_ref[...]) * g_ref[...]

    gate = lambda c: gate_ref[:, c * d:(c + 1) * d].astype(F32)
    merged = (gate(0) * _bdot(yp_ref[...], pp_ref[...])
              + gate(1) * _bdot(ya_ref[...], pa_ref[...])
              + gate(2) * _bdot(y_rwkv, pr_ref[...]))
    x = x_ref[...] + _bdot(merged, wo_ref[...])

    h = _rms(x, g1_ref[...]).astype(BF16)
    hidden = w1_ref.shape[1]
    cut = pl.cdiv(hidden // MXU_TILE, 2) * MXU_TILE
    y = x
    for cs in (slice(0, cut), slice(cut, hidden)):
        h1 = jnp.dot(h, w1_ref[:, cs], preferred_element_type=F32)
        h3 = jnp.dot(h, w3_ref[:, cs], preferred_element_type=F32)
        y = y + _bdot(h1 * _sigmoid(h1) * h3, w2_ref[cs, :])
    gate = _sigmoid(_bdot(_rms(y, g2_ref[...]), wg_ref[...]))
    y = y + _bdot(p_ref[...], wp_ref[...]) * gate
    if final_norm:
        y = _rms(y, gf_ref[...])
    out_ref[...] = y


def _post_mix(x2d, yp, ya, ys, bonus, g, gates, p_all, wts, layer, g_final, tm, final_norm):
    m, d = x2d.shape
    row = lambda w: pl.BlockSpec((tm, w), lambda i: (i, 0))
    names = ('proj_pool', 'proj_attn', 'proj_rwkv', 'w_out', 'rwkv_ln_g', 'rwkv_ln_b')
    names2 = ('norm_ffn_g', 'ffn_w1', 'ffn_w3', 'ffn_w2', 'norm_ple_g', 'ple_gate', 'ple_proj')
    return pl.pallas_call(
        functools.partial(_post_mix_kernel, final_norm=final_norm),
        grid=(m // tm,),
        in_specs=[row(d), row(POOL_W), row(LANES)] + [row(RWKV_W)] * 3 + [row(3 * d)]
        + [pl.BlockSpec((None, tm, p_all.shape[2]), lambda i: (layer, i, 0)), _wspec(g_final)]
        + [_wspec(wts[k], layer) for k in names] + [_wspec(wts['mean_bd'])]
        + [_wspec(wts[k], layer) for k in names2],
        out_specs=row(d),
        out_shape=jax.ShapeDtypeStruct((m, d), F32),
        compiler_params=_params("parallel"),
        name="post_mix_final" if final_norm else "post_mix",
    )(x2d, yp, ya, ys, bonus, g, gates, p_all, g_final,
      *[wts[k] for k in names], wts['mean_bd'], *[wts[k] for k in names2])


def _rope_tables(pos, reps):
    half = ATT_HD // 2
    inv = ROPE_THETA ** (-2.0 * jnp.arange(half, dtype=F32) / ATT_HD)
    ang = pos.astype(F32)[:, None] * inv[None, :]
    cos = jnp.cos(ang)
    sin = jnp.sin(ang)
    cos = jnp.tile(jnp.concatenate([cos, cos], axis=1), (reps, LANES // ATT_HD))
    sin = jnp.tile(jnp.concatenate([-sin, sin], axis=1), (reps, LANES // ATT_HD))
    return cos, sin


def _block_diag(blocks):
    g, a, b = blocks.shape[-3:]
    rows = []
    for i in range(g):
        parts = [blocks[..., i, :, :] if j == i else jnp.zeros_like(blocks[..., i, :, :]) for j in range(g)]
        rows.append(jnp.concatenate(parts, axis=-1))
    return jnp.concatenate(rows, axis=-2)


def _stacked_weights(w):
    row = lambda a: a.reshape(a.shape[0], 1, -1)
    bf = lambda a: a.astype(BF16)
    heads = RWKV_W // RWKV_HD
    zero = jnp.zeros_like(w['rwkv_w2'])
    wwa = jnp.concatenate([jnp.concatenate([w['rwkv_w2'], zero], axis=2),
                           jnp.concatenate([zero, w['rwkv_a2']], axis=2)], axis=1)
    ones = jnp.ones((heads, RWKV_HD, RWKV_HD), F32)
    return {
        'norm_mix_g': row(w['norm_mix_g']), 'w_in': bf(w['w_in']),
        'pool_w_bd': bf(_block_diag(w['pool_w_grp'])), 'pool_scale': row(w['pool_scale']),
        'rwkv_mu': row(w['rwkv_mu']), 'rwkv_wwa': bf(wwa),
        'rwkv_w0': row(w['rwkv_w0']), 'rwkv_a0': row(w['rwkv_a0']), 'rwkv_g2': bf(w['rwkv_g2']),
        'rwkv_k_k': row(w['rwkv_k_k']), 'rwkv_k_a': row(w['rwkv_k_a']), 'rwkv_r_k': row(w['rwkv_r_k']),
        'rwkv_ln_g': row(w['rwkv_ln_g']), 'rwkv_ln_b': row(w['rwkv_ln_b']),
        'ones_bd': bf(_block_diag(ones)), 'mean_bd': bf(_block_diag(ones / RWKV_HD)),
        'proj_pool': bf(w['proj_pool']), 'proj_attn': bf(w['proj_attn']),
        'proj_rwkv': bf(w['proj_rwkv']), 'w_out': bf(w['w_out']),
        'norm_ffn_g': row(w['norm_ffn_g']), 'ffn_w1': bf(w['ffn_w1']), 'ffn_w3': bf(w['ffn_w3']),
        'ffn_w2': bf(w['ffn_w2']), 'norm_ple_g': row(w['norm_ple_g']),
        'ple_proj': bf(w['ple_proj']), 'ple_gate': bf(w['ple_gate']),
    }


def _token_tiles(m):
    tm = next(c for c in (512, 256, 128, m) if m % c == 0)
    return tm, tm, min(tm, 256)


def _pair_states(s):
    st = jnp.swapaxes(s, -1, -2)
    st = st.reshape(st.shape[:-3] + (st.shape[-3] // 2, 2) + st.shape[-2:])
    return _block_diag(st)


def _unpair_states(s2):
    half = s2.shape[-1] // 2
    s = jnp.stack([s2[..., :half, :half], s2[..., half:, half:]], axis=-3)
    s = s.reshape(s.shape[:-4] + (2 * s.shape[-4],) + s.shape[-2:])
    return jnp.swapaxes(s, -1, -2)


def _decoder_layer(x2d, n, t, p_all, wts, layer, g_final, final_norm, tabs, state, *, prompt, tiles, chunk):
    tm, tm_in, tm_post = tiles
    sl = state['layer']
    three = lambda a: a.reshape(n, t, a.shape[-1])

    if prompt:
        yp, q, k, v, zr, gates, tails = _in_proj(x2d, wts, layer, tabs[0], tabs[1], tm_in, seq_len=t)
        per_seq = t // tm_in
        new_pool = tails.reshape(n, per_seq, POOL_HALO, POOL_W)[:, -1, POOL_HALO - POOL_HIST:]
    else:
        zp, q, k, v, zr, gates = _in_proj(x2d, wts, layer, tabs[0], tabs[1], tm_in)
        zp3 = three(zp)
        yp = _pool(state['pool16'], zp3, wts, layer, pos0=state['pos0'])
        new_pool = jnp.concatenate([state['pool16'][sl], zp3], axis=1)[:, -POOL_HIST:]
    q3, k3, v3, zr3 = three(q), three(k), three(v), three(zr)

    if prompt:
        ya = _dil_attn(q3, k3, v3)
        new_kv = []
        for gi, (win, dil) in enumerate(DIL_GROUPS):
            cols = slice(gi * ATT_PAIR, (gi + 1) * ATT_PAIR)
            keep = min(win, t)
            kv = jnp.concatenate([k3[:, t - keep:, cols], v3[:, t - keep:, cols]], axis=-1)
            new_kv.append(kv.reshape(n, keep, 2, 2, ATT_HD))
    else:
        ya, new_kv = _sample_attn(q3, k3, v3, state['kv_t'], state['kv_acc'], sl)

    if prompt:
        at, bt, kt, rt, vv, plast, g, bonus = _rwkv_prep(zr3, state['shift'], sl, wts, layer,
                                                         bn=1, tm=tm, chunk=chunk, t_out=t)
        ys, s_t = _rwkv_scan(at, bt, kt, rt, vv, plast, state['wkv'], sl, n=n, t=t, chunk=chunk, nb=n)
    else:
        at, bt, kt, rt, vv, plast, g, bonus = _rwkv_prep(zr3, state['shift'], sl, wts, layer,
                                                         bn=n, tm=t, chunk=t, t_out=chunk)
        ys, s_t = _rwkv_scan(at, bt, kt, rt, vv, plast, state['wkv'], sl,
                             n=n, t=chunk, chunk=chunk, nb=4 if n % 4 == 0 else 1)
        ys = ys[:, :t]
    new_shift = zr3[:, t - 1]

    x2d = _post_mix(x2d, yp.reshape(n * t, POOL_W), ya.reshape(n * t, LANES), ys.reshape(n * t, RWKV_W),
                    bonus, g, gates, p_all, wts, layer, g_final, tm_post, final_norm)
    return x2d, (new_pool, new_shift, s_t, new_kv)


def kernel(x_prompt, x_sample, state_pool, state_shift, state_wkv, cache_kv_w128, cache_kv_w512,
           cache_kv_w2048, p_prompt, p_sample, norm_mix_g, w_in, pool_w_grp, pool_scale, rwkv_mu,
           rwkv_w0, rwkv_w2, rwkv_a0, rwkv_a2, rwkv_g2, rwkv_k_k, rwkv_k_a, rwkv_r_k, rwkv_ln_g,
           rwkv_ln_b, proj_pool, proj_attn, proj_rwkv, w_out, norm_ffn_g, ffn_w1, ffn_w3, ffn_w2,
           norm_ple_g, ple_proj, ple_gate, norm_final_g):
    weights = dict(norm_mix_g=norm_mix_g, w_in=w_in, pool_w_grp=pool_w_grp, pool_scale=pool_scale,
                   rwkv_mu=rwkv_mu, rwkv_w0=rwkv_w0, rwkv_w2=rwkv_w2, rwkv_a0=rwkv_a0, rwkv_a2=rwkv_a2,
                   rwkv_g2=rwkv_g2, rwkv_k_k=rwkv_k_k, rwkv_k_a=rwkv_k_a, rwkv_r_k=rwkv_r_k,
                   rwkv_ln_g=rwkv_ln_g, rwkv_ln_b=rwkv_ln_b, proj_pool=proj_pool, proj_attn=proj_attn,
                   proj_rwkv=proj_rwkv, w_out=w_out, norm_ffn_g=norm_ffn_g, ffn_w1=ffn_w1, ffn_w3=ffn_w3,
                   ffn_w2=ffn_w2, norm_ple_g=norm_ple_g, ple_proj=ple_proj, ple_gate=ple_gate)
    depth = w_in.shape[0]
    np_, tp, d = x_prompt.shape
    ns, ts, _ = x_sample.shape
    caches = (cache_kv_w128, cache_kv_w512, cache_kv_w2048)
    past_len = PAST_LEN
    tiles_p = _token_tiles(np_ * tp)
    tm_s = ns * ts
    chunk = 64
    tabs_p = _rope_tables(jnp.arange(tp), 1)
    tabs_s = _rope_tables(past_len + jnp.arange(ts), ns)
    g_final = norm_final_g[None, :]
    wts = _stacked_weights(weights)

    xp = x_prompt.reshape(np_ * tp, d)
    xs = x_sample.reshape(ns * ts, d)
    pp_all = p_prompt.reshape(depth, np_ * tp, -1)
    ps_all = p_sample.reshape(depth, ns * ts, -1)
    heads = RWKV_W // RWKV_HD
    state_p = {
        'layer': 0,
        'shift': jnp.zeros((1, np_, 1, RWKV_PROJ), F32),
        'wkv': jnp.zeros((1, np_, heads // 2, 2 * RWKV_HD, 2 * RWKV_HD), F32),
    }
    caches_t = [jnp.transpose(c, (0, 1, 3, 4, 5, 2)).reshape(c.shape[0], c.shape[1], -1, c.shape[2])
                for c in caches]
    kv_acc = [jnp.zeros(c.shape, F32) for c in caches_t]
    state_s = {
        'pool16': jnp.pad(state_pool, ((0, 0), (0, 0), (POOL_HALO - POOL_HIST, 0), (0, 0))),
        'shift': state_shift[:, :, None, :], 'wkv': _pair_states(state_wkv),
        'kv_t': caches_t, 'pos0': past_len,
    }
    outs_p, outs_s = [], []
    for i in range(depth):
        last = i == depth - 1
        xp, st_p = _decoder_layer(xp, np_, tp, pp_all, wts, i, g_final, last, tabs_p, state_p,
                                  prompt=True, tiles=tiles_p, chunk=chunk)
        xs, st_s = _decoder_layer(xs, ns, ts, ps_all, wts, i, g_final, last, tabs_s,
                                  dict(state_s, layer=i, kv_acc=kv_acc),
                                  prompt=False, tiles=(tm_s, tm_s, tm_s), chunk=chunk)
        kv_acc = st_s[3]
        outs_p.append(st_p)
        outs_s.append(st_s)

    stack = lambda outs, f: jnp.stack([f(o) for o in outs])
    res = [xp.reshape(np_, tp, d), xs.reshape(ns, ts, d)]
    for idx in range(2):
        res.append(stack(outs_p, lambda o: o[idx]))
        res.append(stack(outs_s, lambda o: o[idx]))
    res.append(_unpair_states(stack(outs_p, lambda o: o[2])))
    res.append(_unpair_states(stack(outs_s, lambda o: o[2])))
    for gi in range(len(DIL_GROUPS)):
        res.append(stack(outs_p, lambda o: o[3][gi]))
        a = kv_acc[gi]
        a = a.reshape(a.shape[0], a.shape[1], 2, 2, ATT_HD, a.shape[3])
        res.append(jnp.transpose(a, (0, 1, 5, 2, 3, 4)))
    return tuple(res)
```

```python
import functools
import math

import jax
import jax.numpy as jnp
from jax import lax
from jax.experimental import pallas as pl
from jax.experimental.pallas import tpu as pltpu

F32 = jnp.float32
BF16 = jnp.bfloat16

POOL_GC = 64
POOL_W = 256
POOL_WINDOWS = (2, 4, 8, 16)
POOL_HIST = 15
POOL_HALO = 16
ATT_HD = 64
ATT_W = 384
ATT_PAIR = 128
DIL_GROUPS = ((128, 1), (512, 4), (2048, 16))
BAND_BLK = 128
ROPE_THETA = 10000.0
RWKV_HD = 64
RWKV_W = 384
RWKV_PROJ = 1408
RWKV_LN_EPS = 64e-5
RMS_EPS = 1e-6
PAST_LEN = 8192
NEG_BIG = -1e30

LANES = 128
SUBLANES = 8
MXU_TILE = 256
VMEM_LIMIT = 56 * 1024 * 1024

NN = (((1,), (0,)), ((), ()))
NT = (((1,), (1,)), ((), ()))
TN = (((0,), (0,)), ((), ()))


def _params(*sem):
    return pltpu.CompilerParams(dimension_semantics=sem, vmem_limit_bytes=VMEM_LIMIT)


def _wspec(arr, layer=None):
    if layer is None:
        nd = arr.ndim
        return pl.BlockSpec(arr.shape, lambda *_: (0,) * nd, pipeline_mode=pl.Buffered(1))
    nd = arr.ndim - 1
    return pl.BlockSpec((None,) + arr.shape[1:], lambda *_: (layer,) + (0,) * nd,
                        pipeline_mode=pl.Buffered(1))


def _bdot(a, b, dims=NN):
    return lax.dot_general(a.astype(BF16), b.astype(BF16), dims, preferred_element_type=F32)


def _split(a):
    hi = a.astype(BF16)
    lo = (a - hi.astype(F32)).astype(BF16)
    return hi, lo


def _mm3(a, b, dims=NN):
    ah, al = _split(a)
    bh, bl = _split(b)
    f = lambda x, y: lax.dot_general(x, y, dims, preferred_element_type=F32)
    return f(ah, bh) + (f(ah, bl) + f(al, bh))


def _mm2_exact_rhs(a, b_bf16, dims=NN):
    ah, al = _split(a)
    f = lambda x: lax.dot_general(x, b_bf16, dims, preferred_element_type=F32)
    return f(ah) + f(al)


def _head_sums(x, bd_ref):
    w = x.shape[1]
    cuts = [c for c in range(0, w, MXU_TILE)] + [w]
    parts = [_mm2_exact_rhs(x[:, a:b], bd_ref[a:b, a:b]) for a, b in zip(cuts[:-1], cuts[1:])]
    return jnp.concatenate(parts, axis=1)


def _rms(x, g):
    return x * lax.rsqrt(jnp.mean(x * x, axis=-1, keepdims=True) + RMS_EPS) * g


def _sigmoid(x):
    return 1.0 / (1.0 + jnp.exp(-x))


def _pool_from_ext(ext_ref, cur, pos, w_bd, scale):
    bn, tm, w = cur.shape
    lane = lax.broadcasted_iota(jnp.int32, (bn, tm, w), 2)
    group = jnp.right_shift(lane, POOL_GC.bit_length() - 1)
    acc = cur
    win = jnp.zeros_like(cur)
    for s in range(1, POOL_WINDOWS[-1] + 1):
        if s in POOL_WINDOWS:
            win = jnp.where(group == POOL_WINDOWS.index(s), acc, win)
        if s < POOL_WINDOWS[-1]:
            acc = acc + ext_ref[:, POOL_HALO - s:POOL_HALO - s + tm, :]
    cnt = jnp.minimum(pos + 1, jnp.left_shift(2, group)).astype(F32)
    dlt = (win / cnt - cur).reshape(bn * tm, w)
    return _bdot(dlt, w_bd) * scale


def _in_proj_kernel(x_ref, g_ref, w_ref, cos_ref, sin_ref, *rest, tiles_per_seq):
    if tiles_per_seq is None:
        pool_ref, q_ref, k_ref, v_ref, rw_ref, gate_ref = rest
    else:
        pw_ref, ps_ref, pool_ref, q_ref, k_ref, v_ref, rw_ref, gate_ref, tail_ref, ext_ref = rest

        @pl.when(pl.program_id(0) == 0)
        def _():
            ext_ref[...] = jnp.zeros(ext_ref.shape, F32)

    h = _rms(x_ref[...], g_ref[...]).astype(BF16)

    def seg(a, b):
        return jnp.dot(h, w_ref[:, a:b], preferred_element_type=F32)

    cos = cos_ref[...]
    sin = sin_ref[...]
    lane = lax.broadcasted_iota(jnp.int32, cos.shape, 1)
    low_half = (lane & 32) == 0

    def rope(t):
        partner = jnp.where(low_half, pltpu.roll(t, 96, 1), pltpu.roll(t, 32, 1))
        return t * cos + partner * sin

    n_mix = POOL_W + 3 * ATT_W + RWKV_PROJ
    mix = seg(0, n_mix)
    o = 0
    zp = mix[:, o:o + POOL_W]
    if tiles_per_seq is None:
        pool_ref[...] = zp
    else:
        tm = zp.shape[0]
        tile = pl.program_id(0) % tiles_per_seq
        halo = jnp.where(tile == 0, 0.0, ext_ref[:, tm:, :])
        ext_ref[:, :POOL_HALO, :] = halo
        ext_ref[:, POOL_HALO:, :] = zp[None]
        pos = tile * tm + lax.broadcasted_iota(jnp.int32, (1, tm, POOL_W), 1)
        pool_ref[...] = _pool_from_ext(ext_ref, zp[None], pos, pw_ref[...], ps_ref[...])
        tail_ref[...] = zp[None, tm - POOL_HALO:, :]
    o += POOL_W
    for c in range(ATT_W // LANES):
        cs = slice(c * LANES, (c + 1) * LANES)
        q_ref[:, cs] = rope(mix[:, o + c * LANES:o + (c + 1) * LANES]) * (ATT_HD ** -0.5)
        k_ref[:, cs] = rope(mix[:, o + ATT_W + c * LANES:o + ATT_W + (c + 1) * LANES])
    o += 2 * ATT_W
    v_ref[...] = mix[:, o:o + ATT_W]
    o += ATT_W
    rw_ref[...] = mix[:, o:o + RWKV_PROJ]
    gate_ref[...] = _sigmoid(seg(n_mix, w_ref.shape[1])).astype(gate_ref.dtype)


def _in_proj(x2d, wts, layer, cos, sin, tm, seq_len=None):
    m, d = x2d.shape
    g, w_bf = wts['norm_mix_g'], wts['w_in']
    ncol = w_bf.shape[-1]
    ngate = ncol - (POOL_W + 3 * ATT_W + RWKV_PROJ)
    ntab = cos.shape[0] // tm
    row = lambda w: pl.BlockSpec((tm, w), lambda i: (i, 0))
    tab = pl.BlockSpec((tm, LANES), lambda i: (i % ntab, 0))
    widths = (POOL_W, ATT_W, ATT_W, ATT_W, RWKV_PROJ, ngate)
    dtypes = (F32,) * 5 + (BF16,)
    in_specs = [row(d), _wspec(g, layer), _wspec(w_bf, layer), tab, tab]
    args = [x2d, g, w_bf, cos, sin]
    out_specs = [row(w) for w in widths]
    out_shape = [jax.ShapeDtypeStruct((m, w), dt) for w, dt in zip(widths, dtypes)]
    scratch = []
    if seq_len is not None:
        assert seq_len % tm == 0 and tm >= POOL_HALO
        in_specs += [_wspec(wts['pool_w_bd'], layer), _wspec(wts['pool_scale'], layer)]
        args += [wts['pool_w_bd'], wts['pool_scale']]
        out_specs.append(pl.BlockSpec((1, POOL_HALO, POOL_W), lambda i: (i, 0, 0)))
        out_shape.append(jax.ShapeDtypeStruct((m // tm, POOL_HALO, POOL_W), F32))
        scratch.append(pltpu.VMEM((1, tm + POOL_HALO, POOL_W), F32))
    return pl.pallas_call(
        functools.partial(_in_proj_kernel, tiles_per_seq=None if seq_len is None else seq_len // tm),
        grid=(m // tm,),
        in_specs=in_specs,
        out_specs=out_specs,
        out_shape=out_shape,
        scratch_shapes=scratch,
        compiler_params=_params("parallel" if seq_len is None else "arbitrary"),
        name="in_proj",
    )(*args)


def _pool_kernel(prev_ref, cur_ref, w_ref, scale_ref, y_ref, ext_ref, *, pos0):
    bn, tm, w = cur_ref.shape
    cur = cur_ref[...]
    ext_ref[:, :POOL_HALO, :] = prev_ref[...]
    ext_ref[:, POOL_HALO:, :] = cur
    pos = pos0 + lax.broadcasted_iota(jnp.int32, (bn, tm, w), 1)
    y_ref[...] = _pool_from_ext(ext_ref, cur, pos, w_ref[...], scale_ref[...]).reshape(bn, tm, w)


def _pool(hist, cur3, wts, layer, *, pos0):
    n, t, w = cur3.shape
    w_bd, scale = wts['pool_w_bd'], wts['pool_scale']
    return pl.pallas_call(
        functools.partial(_pool_kernel, pos0=pos0),
        grid=(1,),
        in_specs=[pl.BlockSpec((None, n, POOL_HALO, w), lambda b: (layer, 0, 0, 0)),
                  pl.BlockSpec((n, t, w), lambda b: (0, 0, 0)),
                  _wspec(w_bd, layer), _wspec(scale, layer)],
        out_specs=pl.BlockSpec((n, t, w), lambda b: (0, 0, 0)),
        out_shape=jax.ShapeDtypeStruct((n, t, w), F32),
        scratch_shapes=[pltpu.VMEM((n, t + POOL_HALO, w), F32)],
        compiler_params=_params("arbitrary"),
        name="pool_mix",
    )(hist, cur3, w_bd, scale)


def _softmax_pair(q, k, v, valid):
    m_rows = q.shape[0]
    lane = lax.broadcasted_iota(jnp.int32, (m_rows, ATT_PAIR), 1)
    head0 = lane < ATT_HD
    kb = k.astype(BF16)
    vb = v.astype(BF16)
    outs, lses = [], []
    for hs in range(2):
        hm = head0 if hs == 0 else jnp.logical_not(head0)
        qm = jnp.where(hm, q, 0.0).astype(BF16)
        s = lax.dot_general(qm, kb, NT, preferred_element_type=F32)
        s = jnp.where(valid, s, NEG_BIG)
        mx = jnp.max(s, axis=-1, keepdims=True)
        e = jnp.exp(s - mx)
        den = jnp.sum(e, axis=-1, keepdims=True)
        o = jnp.dot(e.astype(BF16), vb, preferred_element_type=F32) / den
        outs.append(o)
        lses.append(jnp.broadcast_to(mx + jnp.log(den), o.shape))
    return jnp.where(head0, outs[0], outs[1]), jnp.where(head0, lses[0], lses[1])


def _merge_groups(outs, lses):
    mx = functools.reduce(jnp.maximum, lses)
    es = [jnp.exp(l - mx) for l in lses]
    num = functools.reduce(lambda a, b: a + b, [e * o for e, o in zip(es, outs)])
    return num / functools.reduce(lambda a, b: a + b, es)


def _rows(start, size, stride):
    return pl.ds(start, size, stride=stride) if stride > 1 else pl.ds(start, size)


def _dil_attn_kernel(*refs):
    ng = len(DIL_GROUPS)
    y_ref, o_scr, l_scr = refs[5 * ng:]
    j = pl.program_id(1)
    tq = y_ref.shape[1]
    qi = lax.broadcasted_iota(jnp.int32, (BAND_BLK, 2 * BAND_BLK), 0)
    kj = lax.broadcasted_iota(jnp.int32, (BAND_BLK, 2 * BAND_BLK), 1)
    dist = BAND_BLK + qi - kj
    has_prev = kj + jnp.where(j > 0, BAND_BLK, 0) >= BAND_BLK
    for gi, (win, dil) in enumerate(DIL_GROUPS):
        band = (dist >= 0) & (dist <= win // dil)
        step = BAND_BLK * dil
        q_ref, k_ref, v_ref, kp_ref, vp_ref = refs[5 * gi:5 * gi + 5]
        for r in range(dil):
            for sb in range(tq // step):
                rows = _rows(r + sb * step, BAND_BLK, dil)
                if sb == 0:
                    prow = _rows(r, BAND_BLK, dil)
                    k_prev, v_prev = kp_ref[0, prow, :], vp_ref[0, prow, :]
                    valid = band & has_prev
                else:
                    prow = _rows(r + (sb - 1) * step, BAND_BLK, dil)
                    k_prev, v_prev = k_ref[0, prow, :], v_ref[0, prow, :]
                    valid = band
                o, lse = _softmax_pair(q_ref[0, rows, :],
                                       jnp.concatenate([k_prev, k_ref[0, rows, :]], axis=0),
                                       jnp.concatenate([v_prev, v_ref[0, rows, :]], axis=0), valid)
                o_scr[gi, rows, :] = o
                l_scr[gi, rows, :] = lse
    y_ref[0] = _merge_groups([o_scr[g] for g in range(ng)], [l_scr[g] for g in range(ng)])


def _dil_attn(q3, k3, v3):
    n, t, w = q3.shape
    tq = BAND_BLK * max(d for _, d in DIL_GROUPS)
    assert t % tq == 0
    specs, args = [], []
    for gi, (_, dil) in enumerate(DIL_GROUPS):
        per = tq // (BAND_BLK * dil)
        cur = pl.BlockSpec((1, tq, LANES), lambda b, j, gi=gi: (b, j, gi))
        prev = pl.BlockSpec((1, BAND_BLK * dil, LANES),
                            lambda b, j, per=per, gi=gi: (b, jnp.maximum(j * per - 1, 0), gi))
        specs += [cur, cur, cur, prev, prev]
        args += [q3, k3, v3, k3, v3]
    ng = len(DIL_GROUPS)
    return pl.pallas_call(
        _dil_attn_kernel,
        grid=(n, t // tq),
        in_specs=specs,
        out_specs=pl.BlockSpec((1, tq, LANES), lambda b, j: (b, j, 0)),
        out_shape=jax.ShapeDtypeStruct((n, t, LANES), F32),
        scratch_shapes=[pltpu.VMEM((ng, tq, LANES), F32), pltpu.VMEM((ng, tq, LANES), F32)],
        compiler_params=_params("parallel", "parallel"),
        name="dil_attn",
    )(*args)


def _sample_attn_kernel(q_ref, k_ref, v_ref, *rest):
    ng = len(DIL_GROUPS)
    cache_refs, y_ref, new_refs = rest[:ng], rest[2 * ng], rest[2 * ng + 1:]
    t = q_ref.shape[1]
    mq = 2 * SUBLANES
    zpad = jnp.zeros((mq - t, ATT_PAIR), F32)
    lane = lax.broadcasted_iota(jnp.int32, (mq, ATT_PAIR), 1)
    head0 = lane < ATT_HD
    tn = lax.broadcasted_iota(jnp.int32, (mq, mq), 0)
    jn = lax.broadcasted_iota(jnp.int32, (mq, mq), 1)
    outs, lses = [], []
    for gi, (win, dil) in enumerate(DIL_GROUPS):
        reach = dil * (win // dil)
        cols = slice(gi * ATT_PAIR, (gi + 1) * ATT_PAIR)
        ct = cache_refs[gi][0, 0]
        hist = ct.shape[1]
        kt2 = ct[:ATT_PAIR].astype(BF16)
        vt2 = ct[ATT_PAIR:].astype(BF16)
        k_new, v_new = k_ref[0, :, cols], v_ref[0, :, cols]
        q = jnp.concatenate([q_ref[0, :, cols], zpad], axis=0)
        kn = jnp.concatenate([k_new, zpad], axis=0).astype(BF16)
        vn = jnp.concatenate([v_new, zpad], axis=0).astype(BF16)
        ti = lax.broadcasted_iota(jnp.int32, (mq, hist), 0)
        pj = lax.broadcasted_iota(jnp.int32, (mq, hist), 1)
        d_old = hist + ti - pj
        ok_old = (d_old <= reach) & ((d_old & (dil - 1)) == 0)
        d_new = tn - jn
        ok_new = (d_new >= 0) & (d_new <= reach) & ((d_new & (dil - 1)) == 0) & (jn < t)
        o_h, l_h = [], []
        for hs in range(2):
            hm = head0 if hs == 0 else jnp.logical_not(head0)
            qm = jnp.where(hm, q, 0.0).astype(BF16)
            s_old = jnp.where(ok_old, jnp.dot(qm, kt2, preferred_element_type=F32), NEG_BIG)
            s_new = jnp.where(ok_new, lax.dot_general(qm, kn, NT, preferred_element_type=F32), NEG_BIG)
            mx = jnp.maximum(jnp.max(s_old, axis=-1, keepdims=True), jnp.max(s_new, axis=-1, keepdims=True))
            e_old = jnp.exp(s_old - mx)
            e_new = jnp.exp(s_new - mx)
            den = jnp.sum(e_old, axis=-1, keepdims=True) + jnp.sum(e_new, axis=-1, keepdims=True)
            num = (lax.dot_general(e_old.astype(BF16), vt2, NT, preferred_element_type=F32)
                   + jnp.dot(e_new.astype(BF16), vn, preferred_element_type=F32))
            o_h.append(num / den)
            l_h.append(jnp.broadcast_to(mx + jnp.log(den), num.shape))
        outs.append(jnp.where(head0, o_h[0], o_h[1])[:t])
        lses.append(jnp.where(head0, l_h[0], l_h[1])[:t])

        shifted = pltpu.roll(ct, hist - t, 1)
        fresh = jnp.concatenate([jnp.zeros((LANES - t, 2 * ATT_PAIR), F32),
                                 jnp.concatenate([k_new, v_new], axis=1)], axis=0).T
        lane_c = lax.broadcasted_iota(jnp.int32, (2 * ATT_PAIR, LANES), 1)
        if hist > LANES:
            new_refs[gi][0, 0, :, :hist - LANES] = shifted[:, :hist - LANES]
        new_refs[gi][0, 0, :, hist - LANES:] = jnp.where(lane_c >= LANES - t, fresh, shifted[:, hist - LANES:])
    y_ref[0] = _merge_groups(outs, lses)


def _sample_attn(q3, k3, v3, caches_t, accs, layer):
    n, t, w = q3.shape
    ng = len(DIL_GROUPS)
    for (win, dil), c in zip(DIL_GROUPS, caches_t):
        assert c.shape[3] == win and win % LANES == 0 and dil & (dil - 1) == 0 and t <= SUBLANES
    new = pl.BlockSpec((1, t, w), lambda b: (b, 0, 0))
    blk = lambda c: pl.BlockSpec((1, 1) + c.shape[2:], lambda b: (layer, b, 0, 0))
    res = pl.pallas_call(
        _sample_attn_kernel,
        grid=(n,),
        in_specs=[new, new, new] + [blk(c) for c in caches_t]
        + [pl.BlockSpec(memory_space=pl.ANY)] * ng,
        out_specs=[pl.BlockSpec((1, t, LANES), lambda b: (b, 0, 0))] + [blk(c) for c in accs],
        out_shape=[jax.ShapeDtypeStruct((n, t, LANES), F32)]
        + [jax.ShapeDtypeStruct(a.shape, F32) for a in accs],
        input_output_aliases={3 + ng + g: 1 + g for g in range(ng)},
        compiler_params=_params("parallel"),
        name="sample_attn",
    )(q3, k3, v3, *caches_t, *accs)
    return res[0], list(res[1:])


def _rwkv_prep_kernel(prev8_ref, sp_ref, cur_ref, mu_ref, wwa_ref, w0_ref, a0_ref, g2_ref,
                      kk_ref, ka_ref, rk_ref, ones_ref, tril_ref,
                      at_ref, bt_ref, kt_ref, rt_ref, v_ref, pl_ref, g_ref, bonus_ref,
                      sh_ref, p_ref, *, chunk):
    bn, tm, w = cur_ref.shape
    j = pl.program_id(1)
    cur = cur_ref[...]
    prev_row = jnp.where(j == 0, sp_ref[...], prev8_ref[:, SUBLANES - 1:SUBLANES, :])
    sh_ref[:, SUBLANES - 1:SUBLANES, :] = prev_row
    sh_ref[:, SUBLANES:, :] = cur
    prev = sh_ref[:, SUBLANES - 1:SUBLANES - 1 + tm, :]
    m = bn * tm
    zs = (cur + (prev - cur) * mu_ref[...]).reshape(m, w)

    r = zs[:, 0:RWKV_W]
    k = zs[:, RWKV_W:2 * RWKV_W]
    v = zs[:, 2 * RWKV_W:3 * RWKV_W]
    lo = 3 * RWKV_W
    z_wa = zs[:, lo:lo + LANES]
    z_g = zs[:, lo + LANES:lo + 2 * LANES]
    lane = lax.broadcasted_iota(jnp.int32, z_wa.shape, 1)
    u = jnp.where(lane < LANES // 2, jnp.tanh(z_wa), z_wa)
    lora = _bdot(u, wwa_ref[...])
    xw = w0_ref[...] + lora[:, :RWKV_W]
    w_log = -(jnp.maximum(-xw, 0.0) + jnp.log(1.0 + jnp.exp(-jnp.abs(xw)))) - 0.5
    e = jnp.exp(w_log)
    a = _sigmoid(a0_ref[...] + lora[:, RWKV_W:])
    g_ref[...] = _bdot(_sigmoid(z_g), g2_ref[...])

    kk = k * kk_ref[...]
    kk = kk * lax.rsqrt(jnp.maximum(_head_sums(kk * kk, ones_ref), 1e-24))
    k = k * (1.0 + (a - 1.0) * ka_ref[...])
    bonus_ref[...] = _head_sums(r * k * rk_ref[...], ones_ref) * v

    tril = tril_ref[...]
    tb = tril.shape[0]
    e1 = e.astype(BF16)
    rem = e - e1.astype(F32)
    e2 = rem.astype(BF16)
    e3 = (rem - e2.astype(F32)).astype(BF16)
    e123 = jnp.concatenate([e1, e2, e3], axis=1)
    cum3 = jnp.concatenate([jnp.dot(tril, e123[b * tb:(b + 1) * tb], preferred_element_type=F32)
                            for b in range(m // tb)], axis=0)
    cum = cum3[:, :RWKV_W] + (cum3[:, RWKV_W:2 * RWKV_W] + cum3[:, 2 * RWKV_W:])
    p_inc = jnp.exp(-cum)
    p_inv = jnp.exp(cum)

    def put(ref, val):
        ref[:, :tm, :] = val.reshape(bn, tm, RWKV_W)
        if ref.shape[1] > tm:
            ref[:, tm:, :] = jnp.zeros((bn, ref.shape[1] - tm, RWKV_W), F32)

    put(at_ref, -kk * jnp.exp(e - cum))
    put(bt_ref, kk * a * p_inv)
    put(kt_ref, k * p_inv)
    put(rt_ref, r * p_inc)
    put(v_ref, v)
    p_ref[...] = p_inc
    for c in range(m // chunk):
        pl_ref[c] = p_ref[(c + 1) * chunk - 1:(c + 1) * chunk, :]


def _rwkv_prep(zr3, shift_prev, state_layer, wts, layer, *, bn, tm, chunk, t_out):
    n, t, w = zr3.shape
    m = bn * tm
    per = tm // SUBLANES
    assert t_out == t or t == tm
    rowblk = pl.BlockSpec((m, RWKV_W), lambda b, j: (b * (t // tm) + j, 0))
    scanblk = pl.BlockSpec((bn, tm if t_out == t else t_out, RWKV_W), lambda b, j: (b, j, 0))
    tb = min(m, MXU_TILE)
    assert tb % chunk == 0 and m % tb == 0
    tril = (jnp.arange(tb)[:, None] >= jnp.arange(tb)[None, :]) & \
           (jnp.arange(tb)[:, None] // chunk == jnp.arange(tb)[None, :] // chunk)
    nch = m // chunk
    names = ('rwkv_mu', 'rwkv_wwa', 'rwkv_w0', 'rwkv_a0', 'rwkv_g2', 'rwkv_k_k', 'rwkv_k_a', 'rwkv_r_k')
    consts = [wts[k] for k in names] + [wts['ones_bd'], tril.astype(BF16)]
    const_specs = [_wspec(wts[k], layer) for k in names] + [_wspec(wts['ones_bd']), _wspec(consts[-1])]
    outs = pl.pallas_call(
        functools.partial(_rwkv_prep_kernel, chunk=chunk),
        grid=(n // bn, t // tm),
        in_specs=[pl.BlockSpec((bn, SUBLANES, w), lambda b, j: (b, jnp.maximum(j * per - 1, 0), 0)),
                  pl.BlockSpec((None, bn, 1, w), lambda b, j: (state_layer, b, 0, 0)),
                  pl.BlockSpec((bn, tm, w), lambda b, j: (b, j, 0))]
        + const_specs,
        out_specs=[scanblk] * 5
        + [pl.BlockSpec((nch, 1, RWKV_W), lambda b, j: (b * (t // tm) + j, 0, 0)), rowblk, rowblk],
        out_shape=[jax.ShapeDtypeStruct((n, t_out, RWKV_W), F32)] * 5
        + [jax.ShapeDtypeStruct((n * t // chunk, 1, RWKV_W), F32)]
        + [jax.ShapeDtypeStruct((n * t, RWKV_W), F32)] * 2,
        scratch_shapes=[pltpu.VMEM((bn, tm + SUBLANES, w), F32), pltpu.VMEM((m, RWKV_W), F32)],
        compiler_params=_params("parallel", "parallel"),
        name="rwkv_prep",
    )(zr3, shift_prev, zr3, *consts)
    return outs


def _rwkv_scan_kernel(at_ref, bt_ref, kt_ref, rt_ref, v_ref, pl_ref, s0_ref, y_ref, sT_ref, s_ref):
    c = pl.program_id(1)
    chunk = RWKV_HD
    nb, nsub = at_ref.shape[0], at_ref.shape[1] // chunk
    npair = RWKV_W // LANES
    assert 2 * chunk == LANES

    @pl.when(c == 0)
    def _():
        s_ref[...] = s0_ref[...]

    lane = lax.broadcasted_iota(jnp.int32, (chunk, LANES), 1)
    head0 = lane < RWKV_HD
    row2 = lax.broadcasted_iota(jnp.int32, (chunk, 2 * LANES), 0)
    col2 = lax.broadcasted_iota(jnp.int32, (chunk, 2 * LANES), 1) & (chunk - 1)
    strict2 = col2 < row2
    incl2 = col2 <= row2
    row1 = lax.broadcasted_iota(jnp.int32, (chunk, LANES), 0)
    eye_ls = jnp.where((lane & (chunk - 1)) == row1, 1.0, 0.0).astype(F32)
    ri = lax.broadcasted_iota(jnp.int32, (LANES, LANES), 0)
    ci = lax.broadcasted_iota(jnp.int32, (LANES, LANES), 1)
    same = (ri < chunk) == (ci < chunk)
    eye = jnp.where(ri == ci, 1.0, 0.0).astype(F32)

    def stack(x):
        return jnp.concatenate([jnp.where(head0, x, 0.0), jnp.where(head0, 0.0, x)], axis=0)

    def bdiag(x):
        return jnp.where(same, jnp.concatenate([x, x], axis=0), 0.0)

    def bdiag_bf(x):
        return bdiag(x).astype(BF16)

    def mm(a, b, dims=NN):
        (ah, al), (bh, bl) = a, b
        lhs = jnp.concatenate([ah, ah, al], axis=0 if dims == TN else 1)
        rhs = jnp.concatenate([bh, bl, bh], axis=1 if dims == NT else 0)
        return lax.dot_general(lhs, rhs, dims, preferred_element_type=F32)

    jobs = [(sq, p, slice(p * LANES, (p + 1) * LANES), slice(ch * chunk, (ch + 1) * chunk), ch)
            for sq in range(nb) for p in range(npair) for ch in range(nsub)]
    states = {(sq, p): s_ref[sq, p] for sq in range(nb) for p in range(npair)}
    at = [at_ref[sq, rs, cs] for sq, p, cs, rs, ch in jobs]
    rt = [rt_ref[sq, rs, cs] for sq, p, cs, rs, ch in jobs]
    vv = [v_ref[sq, rs, cs] for sq, p, cs, rs, ch in jobs]
    gram_a, gram_r = [], []
    for i, (sq, p, cs, rs, ch) in enumerate(jobs):
        bk = _split(jnp.concatenate([stack(bt_ref[sq, rs, cs]), stack(kt_ref[sq, rs, cs])], axis=0))
        gram_a.append(mm(_split(at[i]), bk, NT))
        gram_r.append(lax.dot_general(rt[i].astype(BF16), bk[0], NT, preferred_element_type=F32))
    m_ab = [jnp.where(strict2, g, 0.0)[:, :LANES] for g in gram_a]
    m_ak = [jnp.where(strict2, g, 0.0)[:, LANES:] for g in gram_a]
    m_r = [jnp.where(incl2, g, 0.0).astype(BF16) for g in gram_r]
    zero_bf = jnp.zeros((LANES, LANES), BF16)
    nlev = int(math.log2(chunk))
    npow = [jnp.dot(m.astype(BF16), bdiag_bf(m), preferred_element_type=F32) for m in m_ab]
    inv = [eye_ls + m for m in m_ab]
    for lev in range(1, nlev):
        for i in range(len(jobs)):
            if lev < nlev - 1:
                rhs = jnp.concatenate([bdiag_bf(npow[i]), bdiag_bf(inv[i])], axis=1)
                out = jnp.dot(npow[i].astype(BF16), rhs, preferred_element_type=F32)
                npow[i], inv[i] = out[:, :LANES], inv[i] + out[:, LANES:]
            else:
                inv[i] = inv[i] + jnp.dot(npow[i].astype(BF16), bdiag_bf(inv[i]), preferred_element_type=F32)
    mv = [mm(_split(m), _split(bdiag(x))) for m, x in zip(m_ak, vv)]
    wu = [mm(_split(t), _split(jnp.concatenate([bdiag(x), bdiag(a)], axis=1))) for t, x, a in zip(inv, mv, at)]
    qy = []
    for i in range(len(jobs)):
        w_m, u_m = wu[i][:, :LANES], wu[i][:, LANES:]
        zq = jnp.concatenate([jnp.concatenate([bdiag_bf(u_m), bdiag_bf(w_m)], axis=1),
                              jnp.concatenate([zero_bf, bdiag_bf(vv[i])], axis=1)], axis=0)
        qy.append(jnp.dot(m_r[i], zq, preferred_element_type=F32))
    ad = []
    for i, (sq, p, cs, rs, ch) in enumerate(jobs):
        decay = pl_ref[sq, ch, :, cs]
        bk = jnp.concatenate([bt_ref[sq, rs, cs], kt_ref[sq, rs, cs]], axis=0) * decay
        z = jnp.concatenate([jnp.concatenate([wu[i][:, LANES:], wu[i][:, :LANES]], axis=1),
                             jnp.concatenate([jnp.zeros_like(vv[i]), vv[i]], axis=1)], axis=0)
        ad.append(mm(_split(bk), _split(z), TN))
    a_t = [_split(eye * pl_ref[sq, ch, :, cs] + jnp.where(same, ad[i][:, :LANES], 0.0))
           for i, (sq, p, cs, rs, ch) in enumerate(jobs)]
    d_t = [jnp.where(same, ad[i][:, LANES:], 0.0) for i in range(len(jobs))]
    q_t = [(rt[i] + qy[i][:, :LANES]).astype(BF16) for i in range(len(jobs))]
    for i, (sq, p, cs, rs, ch) in sorted(enumerate(jobs), key=lambda e: e[1][4]):
        st = _split(states[sq, p])
        y_ref[sq, rs, cs] = jnp.dot(q_t[i], st[0], preferred_element_type=F32) + qy[i][:, LANES:]
        states[sq, p] = mm(a_t[i], st) + d_t[i]
    for (sq, p), val in states.items():
        s_ref[sq, p] = val

    @pl.when(c == pl.num_programs(1) - 1)
    def _():
        sT_ref[...] = s_ref[...]


def _rwkv_scan(at, bt, kt, rt, v, plast, s0, state_layer, *, n, t, chunk, nb):
    assert chunk == RWKV_HD and at.shape == (n, t, RWKV_W)
    npair = RWKV_W // LANES
    nch = t // chunk
    nsub = next(c for c in (4, 2, 1) if nch % c == 0)
    tok = pl.BlockSpec((nb, nsub * chunk, RWKV_W), lambda b, c: (b, c, 0))
    st = pl.BlockSpec((nb, npair, LANES, LANES), lambda b, c: (b, 0, 0, 0))
    st_in = pl.BlockSpec((None, nb, npair, LANES, LANES), lambda b, c: (state_layer, b, 0, 0, 0))
    y, s_t = pl.pallas_call(
        _rwkv_scan_kernel,
        grid=(n // nb, nch // nsub),
        in_specs=[tok] * 5 + [pl.BlockSpec((nb, nsub, 1, RWKV_W), lambda b, c: (b, c, 0, 0)), st_in],
        out_specs=[tok, st],
        out_shape=[jax.ShapeDtypeStruct((n, t, RWKV_W), F32),
                   jax.ShapeDtypeStruct((n, npair, LANES, LANES), F32)],
        scratch_shapes=[pltpu.VMEM((nb, npair, LANES, LANES), F32)],
        compiler_params=_params("parallel", "arbitrary"),
        name="rwkv_scan",
    )(at, bt, kt, rt, v, plast.reshape(n, nch, 1, RWKV_W), s0)
    return y, s_t


def _post_mix_kernel(x_ref, yp_ref, ya_ref, ys_ref, bonus_ref, g_ref, gate_ref, p_ref, gf_ref,
                     pp_ref, pa_ref, pr_ref, wo_ref, lng_ref, lnb_ref, mean_ref,
                     g1_ref, w1_ref, w3_ref, w2_ref, g2_ref, wg_ref, wp_ref, out_ref, *, final_norm):
    d = x_ref.shape[1]
    ys = ys_ref[...]
    mu = _head_sums(ys, mean_ref)
    dev = ys - mu
    var = _head_sums(dev * dev, mean_ref)
    yn = dev * lax.rsqrt(var + RWKV_LN_EPS) * lng_ref[...] + lnb_ref[...]
    y_rwkv = (yn + bonus_ref[...]) * g_ref[...]

    gate = lambda c: gate_ref[:, c * d:(c + 1) * d].astype(F32)
    merged = (gate(0) * _bdot(yp_ref[...], pp_ref[...])
              + gate(1) * _bdot(ya_ref[...], pa_ref[...])
              + gate(2) * _bdot(y_rwkv, pr_ref[...]))
    x = x_ref[...] + _bdot(merged, wo_ref[...])

    h = _rms(x, g1_ref[...]).astype(BF16)
    hidden = w1_ref.shape[1]
    cut = pl.cdiv(hidden // MXU_TILE, 2) * MXU_TILE
    y = x
    for cs in (slice(0, cut), slice(cut, hidden)):
        h1 = jnp.dot(h, w1_ref[:, cs], preferred_element_type=F32)
        h3 = jnp.dot(h, w3_ref[:, cs], preferred_element_type=F32)
        y = y + _bdot(h1 * _sigmoid(h1) * h3, w2_ref[cs, :])
    gate = _sigmoid(_bdot(_rms(y, g2_ref[...]), wg_ref[...]))
    y = y + _bdot(p_ref[...], wp_ref[...]) * gate
    if final_norm:
        y = _rms(y, gf_ref[...])
    out_ref[...] = y


def _post_mix(x2d, yp, ya, ys, bonus, g, gates, p_all, wts, layer, g_final, tm, final_norm):
    m, d = x2d.shape
    row = lambda w: pl.BlockSpec((tm, w), lambda i: (i, 0))
    names = ('proj_pool', 'proj_attn', 'proj_rwkv', 'w_out', 'rwkv_ln_g', 'rwkv_ln_b')
    names2 = ('norm_ffn_g', 'ffn_w1', 'ffn_w3', 'ffn_w2', 'norm_ple_g', 'ple_gate', 'ple_proj')
    return pl.pallas_call(
        functools.partial(_post_mix_kernel, final_norm=final_norm),
        grid=(m // tm,),
        in_specs=[row(d), row(POOL_W), row(LANES)] + [row(RWKV_W)] * 3 + [row(3 * d)]
        + [pl.BlockSpec((None, tm, p_all.shape[2]), lambda i: (layer, i, 0)), _wspec(g_final)]
        + [_wspec(wts[k], layer) for k in names] + [_wspec(wts['mean_bd'])]
        + [_wspec(wts[k], layer) for k in names2],
        out_specs=row(d),
        out_shape=jax.ShapeDtypeStruct((m, d), F32),
        compiler_params=_params("parallel"),
        name="post_mix_final" if final_norm else "post_mix",
    )(x2d, yp, ya, ys, bonus, g, gates, p_all, g_final,
      *[wts[k] for k in names], wts['mean_bd'], *[wts[k] for k in names2])


def _rope_tables(pos, reps):
    half = ATT_HD // 2
    inv = ROPE_THETA ** (-2.0 * jnp.arange(half, dtype=F32) / ATT_HD)
    ang = pos.astype(F32)[:, None] * inv[None, :]
    cos = jnp.cos(ang)
    sin = jnp.sin(ang)
    cos = jnp.tile(jnp.concatenate([cos, cos], axis=1), (reps, LANES // ATT_HD))
    sin = jnp.tile(jnp.concatenate([-sin, sin], axis=1), (reps, LANES // ATT_HD))
    return cos, sin


def _block_diag(blocks):
    g, a, b = blocks.shape[-3:]
    rows = []
    for i in range(g):
        parts = [blocks[..., i, :, :] if j == i else jnp.zeros_like(blocks[..., i, :, :]) for j in range(g)]
        rows.append(jnp.concatenate(parts, axis=-1))
    return jnp.concatenate(rows, axis=-2)


def _stacked_weights(w):
    row = lambda a: a.reshape(a.shape[0], 1, -1)
    bf = lambda a: a.astype(BF16)
    heads = RWKV_W // RWKV_HD
    zero = jnp.zeros_like(w['rwkv_w2'])
    wwa = jnp.concatenate([jnp.concatenate([w['rwkv_w2'], zero], axis=2),
                           jnp.concatenate([zero, w['rwkv_a2']], axis=2)], axis=1)
    ones = jnp.ones((heads, RWKV_HD, RWKV_HD), F32)
    return {
        'norm_mix_g': row(w['norm_mix_g']), 'w_in': bf(w['w_in']),
        'pool_w_bd': bf(_block_diag(w['pool_w_grp'])), 'pool_scale': row(w['pool_scale']),
        'rwkv_mu': row(w['rwkv_mu']), 'rwkv_wwa': bf(wwa),
        'rwkv_w0': row(w['rwkv_w0']), 'rwkv_a0': row(w['rwkv_a0']), 'rwkv_g2': bf(w['rwkv_g2']),
        'rwkv_k_k': row(w['rwkv_k_k']), 'rwkv_k_a': row(w['rwkv_k_a']), 'rwkv_r_k': row(w['rwkv_r_k']),
        'rwkv_ln_g': row(w['rwkv_ln_g']), 'rwkv_ln_b': row(w['rwkv_ln_b']),
        'ones_bd': bf(_block_diag(ones)), 'mean_bd': bf(_block_diag(ones / RWKV_HD)),
        'proj_pool': bf(w['proj_pool']), 'proj_attn': bf(w['proj_attn']),
        'proj_rwkv': bf(w['proj_rwkv']), 'w_out': bf(w['w_out']),
        'norm_ffn_g': row(w['norm_ffn_g']), 'ffn_w1': bf(w['ffn_w1']), 'ffn_w3': bf(w['ffn_w3']),
        'ffn_w2': bf(w['ffn_w2']), 'norm_ple_g': row(w['norm_ple_g']),
        'ple_proj': bf(w['ple_proj']), 'ple_gate': bf(w['ple_gate']),
    }


def _token_tiles(m):
    tm = next(c for c in (512, 256, 128, m) if m % c == 0)
    return tm, tm, min(tm, 256)


def _pair_states(s):
    st = jnp.swapaxes(s, -1, -2)
    st = st.reshape(st.shape[:-3] + (st.shape[-3] // 2, 2) + st.shape[-2:])
    return _block_diag(st)


def _unpair_states(s2):
    half = s2.shape[-1] // 2
    s = jnp.stack([s2[..., :half, :half], s2[..., half:, half:]], axis=-3)
    s = s.reshape(s.shape[:-4] + (2 * s.shape[-4],) + s.shape[-2:])
    return jnp.swapaxes(s, -1, -2)


def _decoder_layer(x2d, n, t, p_all, wts, layer, g_final, final_norm, tabs, state, *, prompt, tiles, chunk):
    tm, tm_in, tm_post = tiles
    sl = state['layer']
    three = lambda a: a.reshape(n, t, a.shape[-1])

    if prompt:
        yp, q, k, v, zr, gates, tails = _in_proj(x2d, wts, layer, tabs[0], tabs[1], tm_in, seq_len=t)
        per_seq = t // tm_in
        new_pool = tails.reshape(n, per_seq, POOL_HALO, POOL_W)[:, -1, POOL_HALO - POOL_HIST:]
    else:
        zp, q, k, v, zr, gates = _in_proj(x2d, wts, layer, tabs[0], tabs[1], tm_in)
        zp3 = three(zp)
        yp = _pool(state['pool16'], zp3, wts, layer, pos0=state['pos0'])
        new_pool = jnp.concatenate([state['pool16'][sl], zp3], axis=1)[:, -POOL_HIST:]
    q3, k3, v3, zr3 = three(q), three(k), three(v), three(zr)

    if prompt:
        ya = _dil_attn(q3, k3, v3)
        new_kv = []
        for gi, (win, dil) in enumerate(DIL_GROUPS):
            cols = slice(gi * ATT_PAIR, (gi + 1) * ATT_PAIR)
            keep = min(win, t)
            kv = jnp.concatenate([k3[:, t - keep:, cols], v3[:, t - keep:, cols]], axis=-1)
            new_kv.append(kv.reshape(n, keep, 2, 2, ATT_HD))
    else:
        ya, new_kv = _sample_attn(q3, k3, v3, state['kv_t'], state['kv_acc'], sl)

    if prompt:
        at, bt, kt, rt, vv, plast, g, bonus = _rwkv_prep(zr3, state['shift'], sl, wts, layer,
                                                         bn=1, tm=tm, chunk=chunk, t_out=t)
        ys, s_t = _rwkv_scan(at, bt, kt, rt, vv, plast, state['wkv'], sl, n=n, t=t, chunk=chunk, nb=n)
    else:
        at, bt, kt, rt, vv, plast, g, bonus = _rwkv_prep(zr3, state['shift'], sl, wts, layer,
                                                         bn=n, tm=t, chunk=t, t_out=chunk)
        ys, s_t = _rwkv_scan(at, bt, kt, rt, vv, plast, state['wkv'], sl,
                             n=n, t=chunk, chunk=chunk, nb=4 if n % 4 == 0 else 1)
        ys = ys[:, :t]
    new_shift = zr3[:, t - 1]

    x2d = _post_mix(x2d, yp.reshape(n * t, POOL_W), ya.reshape(n * t, LANES), ys.reshape(n * t, RWKV_W),
                    bonus, g, gates, p_all, wts, layer, g_final, tm_post, final_norm)
    return x2d, (new_pool, new_shift, s_t, new_kv)


def kernel(x_prompt, x_sample, state_pool, state_shift, state_wkv, cache_kv_w128, cache_kv_w512,
           cache_kv_w2048, p_prompt, p_sample, norm_mix_g, w_in, pool_w_grp, pool_scale, rwkv_mu,
           rwkv_w0, rwkv_w2, rwkv_a0, rwkv_a2, rwkv_g2, rwkv_k_k, rwkv_k_a, rwkv_r_k, rwkv_ln_g,
           rwkv_ln_b, proj_pool, proj_attn, proj_rwkv, w_out, norm_ffn_g, ffn_w1, ffn_w3, ffn_w2,
           norm_ple_g, ple_proj, ple_gate, norm_final_g):
    weights = dict(norm_mix_g=norm_mix_g, w_in=w_in, pool_w_grp=pool_w_grp, pool_scale=pool_scale,
                   rwkv_mu=rwkv_mu, rwkv_w0=rwkv_w0, rwkv_w2=rwkv_w2, rwkv_a0=rwkv_a0, rwkv_a2=rwkv_a2,
                   rwkv_g2=rwkv_g2, rwkv_k_k=rwkv_k_k, rwkv_k_a=rwkv_k_a, rwkv_r_k=rwkv_r_k,
                   rwkv_ln_g=rwkv_ln_g, rwkv_ln_b=rwkv_ln_b, proj_pool=proj_pool, proj_attn=proj_attn,
                   proj_rwkv=proj_rwkv, w_out=w_out, norm_ffn_g=norm_ffn_g, ffn_w1=ffn_w1, ffn_w3=ffn_w3,
                   ffn_w2=ffn_w2, norm_ple_g=norm_ple_g, ple_proj=ple_proj, ple_gate=ple_gate)
    depth = w_in.shape[0]
    np_, tp, d = x_prompt.shape
    ns, ts, _ = x_sample.shape
    caches = (cache_kv_w128, cache_kv_w512, cache_kv_w2048)
    past_len = PAST_LEN
    tiles_p = _token_tiles(np_ * tp)
    tm_s = ns * ts
    chunk = 64
    tabs_p = _rope_tables(jnp.arange(tp), 1)
    tabs_s = _rope_tables(past_len + jnp.arange(ts), ns)
    g_final = norm_final_g[None, :]
    wts = _stacked_weights(weights)

    xp = x_prompt.reshape(np_ * tp, d)
    xs = x_sample.reshape(ns * ts, d)
    pp_all = p_prompt.reshape(depth, np_ * tp, -1)
    ps_all = p_sample.reshape(depth, ns * ts, -1)
    heads = RWKV_W // RWKV_HD
    state_p = {
        'layer': 0,
        'shift': jnp.zeros((1, np_, 1, RWKV_PROJ), F32),
        'wkv': jnp.zeros((1, np_, heads // 2, 2 * RWKV_HD, 2 * RWKV_HD), F32),
    }
    caches_t = [jnp.transpose(c, (0, 1, 3, 4, 5, 2)).reshape(c.shape[0], c.shape[1], -1, c.shape[2])
                for c in caches]
    kv_acc = [jnp.zeros(c.shape, F32) for c in caches_t]
    state_s = {
        'pool16': jnp.pad(state_pool, ((0, 0), (0, 0), (POOL_HALO - POOL_HIST, 0), (0, 0))),
        'shift': state_shift[:, :, None, :], 'wkv': _pair_states(state_wkv),
        'kv_t': caches_t, 'pos0': past_len,
    }
    outs_p, outs_s = [], []
    for i in range(depth):
        last = i == depth - 1
        xp, st_p = _decoder_layer(xp, np_, tp, pp_all, wts, i, g_final, last, tabs_p, state_p,
                                  prompt=True, tiles=tiles_p, chunk=chunk)
        xs, st_s = _decoder_layer(xs, ns, ts, ps_all, wts, i, g_final, last, tabs_s,
                                  dict(state_s, layer=i, kv_acc=kv_acc),
                                  prompt=False, tiles=(tm_s, tm_s, tm_s), chunk=chunk)
        kv_acc = st_s[3]
        outs_p.append(st_p)
        outs_s.append(st_s)

    stack = lambda outs, f: jnp.stack([f(o) for o in outs])
    res = [xp.reshape(np_, tp, d), xs.reshape(ns, ts, d)]
    for idx in range(2):
        res.append(stack(outs_p, lambda o: o[idx]))
        res.append(stack(outs_s, lambda o: o[idx]))
    res.append(_unpair_states(stack(outs_p, lambda o: o[2])))
    res.append(_unpair_states(stack(outs_s, lambda o: o[2])))
    for gi in range(len(DIL_GROUPS)):
        res.append(stack(outs_p, lambda o: o[3][gi]))
        a = kv_acc[gi]
        a = a.reshape(a.shape[0], a.shape[1], 2, 2, ATT_HD, a.shape[3])
        res.append(jnp.transpose(a, (0, 1, 5, 2, 3, 4)))
    return tuple(res)
```

```python
import functools
import math

import jax
import jax.numpy as jnp
from jax import lax
from jax.experimental import pallas as pl
from jax.experimental.pallas import tpu as pltpu

F32 = jnp.float32
BF16 = jnp.bfloat16

POOL_GC = 64
POOL_W = 256
POOL_WINDOWS = (2, 4, 8, 16)
POOL_HIST = 15
POOL_HALO = 16
ATT_HD = 64
ATT_W = 384
ATT_PAIR = 128
DIL_GROUPS = ((128, 1), (512, 4), (2048, 16))
BAND_BLK = 128
ROPE_THETA = 10000.0
RWKV_HD = 64
RWKV_W = 384
RWKV_PROJ = 1408
RWKV_LN_EPS = 64e-5
RMS_EPS = 1e-6
PAST_LEN = 8192
NEG_BIG = -1e30

LANES = 128
SUBLANES = 8
MXU_TILE = 256
VMEM_LIMIT = 56 * 1024 * 1024

NN = (((1,), (0,)), ((), ()))
NT = (((1,), (1,)), ((), ()))
TN = (((0,), (0,)), ((), ()))


def _params(*sem):
    return pltpu.CompilerParams(dimension_semantics=sem, vmem_limit_bytes=VMEM_LIMIT)


def _wspec(arr, layer=None):
    if layer is None:
        nd = arr.ndim
        return pl.BlockSpec(arr.shape, lambda *_: (0,) * nd, pipeline_mode=pl.Buffered(1))
    nd = arr.ndim - 1
    return pl.BlockSpec((None,) + arr.shape[1:], lambda *_: (layer,) + (0,) * nd,
                        pipeline_mode=pl.Buffered(1))


def _bdot(a, b, dims=NN):
    return lax.dot_general(a.astype(BF16), b.astype(BF16), dims, preferred_element_type=F32)


def _split(a):
    hi = a.astype(BF16)
    lo = (a - hi.astype(F32)).astype(BF16)
    return hi, lo


def _mm2_exact_rhs(a, b_bf16, dims=NN):
    ah, al = _split(a)
    f = lambda x: lax.dot_general(x, b_bf16, dims, preferred_element_type=F32)
    return f(ah) + f(al)


def _head_sums(x, bd_ref):
    w = x.shape[1]
    cuts = [c for c in range(0, w, MXU_TILE)] + [w]
    parts = [_mm2_exact_rhs(x[:, a:b], bd_ref[a:b, a:b]) for a, b in zip(cuts[:-1], cuts[1:])]
    return jnp.concatenate(parts, axis=1)


def _rms(x, g):
    return x * lax.rsqrt(jnp.mean(x * x, axis=-1, keepdims=True) + RMS_EPS) * g


def _sigmoid(x):
    return 1.0 / (1.0 + jnp.exp(-x))


def _pool_from_ext(ext_ref, cur, pos, w_bd, scale):
    bn, tm, w = cur.shape
    lane = lax.broadcasted_iota(jnp.int32, (bn, tm, w), 2)
    group = jnp.right_shift(lane, POOL_GC.bit_length() - 1)
    acc = cur
    win = jnp.zeros_like(cur)
    for s in range(1, POOL_WINDOWS[-1] + 1):
        if s in POOL_WINDOWS:
            win = jnp.where(group == POOL_WINDOWS.index(s), acc, win)
        if s < POOL_WINDOWS[-1]:
            acc = acc + ext_ref[:, POOL_HALO - s:POOL_HALO - s + tm, :]
    cnt = jnp.minimum(pos + 1, jnp.left_shift(2, group)).astype(F32)
    dlt = (win / cnt - cur).reshape(bn * tm, w)
    return _bdot(dlt, w_bd) * scale


def _in_proj_kernel(x_ref, g_ref, w_ref, cos_ref, sin_ref, *rest, tiles_per_seq):
    if tiles_per_seq is None:
        pool_ref, q_ref, k_ref, v_ref, rw_ref, gate_ref = rest
    else:
        pw_ref, ps_ref, pool_ref, q_ref, k_ref, v_ref, rw_ref, gate_ref, tail_ref, ext_ref = rest

        @pl.when(pl.program_id(0) == 0)
        def _():
            ext_ref[...] = jnp.zeros(ext_ref.shape, F32)

    h = _rms(x_ref[...], g_ref[...]).astype(BF16)

    def seg(a, b):
        return jnp.dot(h, w_ref[:, a:b], preferred_element_type=F32)

    cos = cos_ref[...]
    sin = sin_ref[...]
    lane = lax.broadcasted_iota(jnp.int32, cos.shape, 1)
    low_half = (lane & 32) == 0

    def rope(t):
        partner = jnp.where(low_half, pltpu.roll(t, 96, 1), pltpu.roll(t, 32, 1))
        return t * cos + partner * sin

    n_mix = POOL_W + 3 * ATT_W + RWKV_PROJ
    mix = seg(0, n_mix)
    o = 0
    zp = mix[:, o:o + POOL_W]
    if tiles_per_seq is None:
        pool_ref[...] = zp
    else:
        tm = zp.shape[0]
        tile = pl.program_id(0) % tiles_per_seq
        halo = jnp.where(tile == 0, 0.0, ext_ref[:, tm:, :])
        ext_ref[:, :POOL_HALO, :] = halo
        ext_ref[:, POOL_HALO:, :] = zp[None]
        pos = tile * tm + lax.broadcasted_iota(jnp.int32, (1, tm, POOL_W), 1)
        pool_ref[...] = _pool_from_ext(ext_ref, zp[None], pos, pw_ref[...], ps_ref[...])
        tail_ref[...] = zp[None, tm - POOL_HALO:, :]
    o += POOL_W
    for c in range(ATT_W // LANES):
        cs = slice(c * LANES, (c + 1) * LANES)
        q_ref[:, cs] = rope(mix[:, o + c * LANES:o + (c + 1) * LANES]) * (ATT_HD ** -0.5)
        k_ref[:, cs] = rope(mix[:, o + ATT_W + c * LANES:o + ATT_W + (c + 1) * LANES])
    o += 2 * ATT_W
    v_ref[...] = mix[:, o:o + ATT_W]
    o += ATT_W
    rw_ref[...] = mix[:, o:o + RWKV_PROJ]
    gate_ref[...] = _sigmoid(seg(n_mix, w_ref.shape[1])).astype(gate_ref.dtype)


def _in_proj(x2d, wts, layer, cos, sin, tm, seq_len=None):
    m, d = x2d.shape
    g, w_bf = wts['norm_mix_g'], wts['w_in']
    ncol = w_bf.shape[-1]
    ngate = ncol - (POOL_W + 3 * ATT_W + RWKV_PROJ)
    ntab = cos.shape[0] // tm
    row = lambda w: pl.BlockSpec((tm, w), lambda i: (i, 0))
    tab = pl.BlockSpec((tm, LANES), lambda i: (i % ntab, 0))
    widths = (POOL_W, ATT_W, ATT_W, ATT_W, RWKV_PROJ, ngate)
    dtypes = (F32,) * 5 + (BF16,)
    in_specs = [row(d), _wspec(g, layer), _wspec(w_bf, layer), tab, tab]
    args = [x2d, g, w_bf, cos, sin]
    out_specs = [row(w) for w in widths]
    out_shape = [jax.ShapeDtypeStruct((m, w), dt) for w, dt in zip(widths, dtypes)]
    scratch = []
    if seq_len is not None:
        assert seq_len % tm == 0 and tm >= POOL_HALO
        in_specs += [_wspec(wts['pool_w_bd'], layer), _wspec(wts['pool_scale'], layer)]
        args += [wts['pool_w_bd'], wts['pool_scale']]
        out_specs.append(pl.BlockSpec((1, POOL_HALO, POOL_W), lambda i: (i, 0, 0)))
        out_shape.append(jax.ShapeDtypeStruct((m // tm, POOL_HALO, POOL_W), F32))
        scratch.append(pltpu.VMEM((1, tm + POOL_HALO, POOL_W), F32))
    return pl.pallas_call(
        functools.partial(_in_proj_kernel, tiles_per_seq=None if seq_len is None else seq_len // tm),
        grid=(m // tm,),
        in_specs=in_specs,
        out_specs=out_specs,
        out_shape=out_shape,
        scratch_shapes=scratch,
        compiler_params=_params("parallel" if seq_len is None else "arbitrary"),
        name="in_proj",
    )(*args)


def _pool_kernel(prev_ref, cur_ref, w_ref, scale_ref, y_ref, ext_ref, *, pos0):
    bn, tm, w = cur_ref.shape
    cur = cur_ref[...]
    ext_ref[:, :POOL_HALO, :] = prev_ref[...]
    ext_ref[:, POOL_HALO:, :] = cur
    pos = pos0 + lax.broadcasted_iota(jnp.int32, (bn, tm, w), 1)
    y_ref[...] = _pool_from_ext(ext_ref, cur, pos, w_ref[...], scale_ref[...]).reshape(bn, tm, w)


def _pool(hist, cur3, wts, layer, *, pos0):
    n, t, w = cur3.shape
    w_bd, scale = wts['pool_w_bd'], wts['pool_scale']
    return pl.pallas_call(
        functools.partial(_pool_kernel, pos0=pos0),
        grid=(1,),
        in_specs=[pl.BlockSpec((None, n, POOL_HALO, w), lambda b: (layer, 0, 0, 0)),
                  pl.BlockSpec((n, t, w), lambda b: (0, 0, 0)),
                  _wspec(w_bd, layer), _wspec(scale, layer)],
        out_specs=pl.BlockSpec((n, t, w), lambda b: (0, 0, 0)),
        out_shape=jax.ShapeDtypeStruct((n, t, w), F32),
        scratch_shapes=[pltpu.VMEM((n, t + POOL_HALO, w), F32)],
        compiler_params=_params("arbitrary"),
        name="pool_mix",
    )(hist, cur3, w_bd, scale)


def _softmax_pair(q, k, v, valid):
    m_rows = q.shape[0]
    lane = lax.broadcasted_iota(jnp.int32, (m_rows, ATT_PAIR), 1)
    head0 = lane < ATT_HD
    kb = k.astype(BF16)
    vb = v.astype(BF16)
    outs, lses = [], []
    for hs in range(2):
        hm = head0 if hs == 0 else jnp.logical_not(head0)
        qm = jnp.where(hm, q, 0.0).astype(BF16)
        s = lax.dot_general(qm, kb, NT, preferred_element_type=F32)
        s = jnp.where(valid, s, NEG_BIG)
        mx = jnp.max(s, axis=-1, keepdims=True)
        e = jnp.exp(s - mx)
        den = jnp.sum(e, axis=-1, keepdims=True)
        o = jnp.dot(e.astype(BF16), vb, preferred_element_type=F32) / den
        outs.append(o)
        lses.append(jnp.broadcast_to(mx + jnp.log(den), o.shape))
    return jnp.where(head0, outs[0], outs[1]), jnp.where(head0, lses[0], lses[1])


def _merge_groups(outs, lses):
    mx = functools.reduce(jnp.maximum, lses)
    es = [jnp.exp(l - mx) for l in lses]
    num = functools.reduce(lambda a, b: a + b, [e * o for e, o in zip(es, outs)])
    return num / functools.reduce(lambda a, b: a + b, es)


def _rows(start, size, stride):
    return pl.ds(start, size, stride=stride) if stride > 1 else pl.ds(start, size)


def _dil_attn_kernel(*refs):
    ng = len(DIL_GROUPS)
    y_ref, o_scr, l_scr = refs[5 * ng:]
    j = pl.program_id(1)
    tq = y_ref.shape[1]
    qi = lax.broadcasted_iota(jnp.int32, (BAND_BLK, 2 * BAND_BLK), 0)
    kj = lax.broadcasted_iota(jnp.int32, (BAND_BLK, 2 * BAND_BLK), 1)
    dist = BAND_BLK + qi - kj
    has_prev = kj + jnp.where(j > 0, BAND_BLK, 0) >= BAND_BLK
    for gi, (win, dil) in enumerate(DIL_GROUPS):
        band = (dist >= 0) & (dist <= win // dil)
        step = BAND_BLK * dil
        q_ref, k_ref, v_ref, kp_ref, vp_ref = refs[5 * gi:5 * gi + 5]
        for r in range(dil):
            for sb in range(tq // step):
                rows = _rows(r + sb * step, BAND_BLK, dil)
                if sb == 0:
                    prow = _rows(r, BAND_BLK, dil)
                    k_prev, v_prev = kp_ref[0, prow, :], vp_ref[0, prow, :]
                    valid = band & has_prev
                else:
                    prow = _rows(r + (sb - 1) * step, BAND_BLK, dil)
                    k_prev, v_prev = k_ref[0, prow, :], v_ref[0, prow, :]
                    valid = band
                o, lse = _softmax_pair(q_ref[0, rows, :],
                                       jnp.concatenate([k_prev, k_ref[0, rows, :]], axis=0),
                                       jnp.concatenate([v_prev, v_ref[0, rows, :]], axis=0), valid)
                o_scr[gi, rows, :] = o
                l_scr[gi, rows, :] = lse
    y_ref[0] = _merge_groups([o_scr[g] for g in range(ng)], [l_scr[g] for g in range(ng)])


def _dil_attn(q3, k3, v3):
    n, t, w = q3.shape
    tq = BAND_BLK * max(d for _, d in DIL_GROUPS)
    assert t % tq == 0
    specs, args = [], []
    for gi, (_, dil) in enumerate(DIL_GROUPS):
        per = tq // (BAND_BLK * dil)
        cur = pl.BlockSpec((1, tq, LANES), lambda b, j, gi=gi: (b, j, gi))
        prev = pl.BlockSpec((1, BAND_BLK * dil, LANES),
                            lambda b, j, per=per, gi=gi: (b, jnp.maximum(j * per - 1, 0), gi))
        specs += [cur, cur, cur, prev, prev]
        args += [q3, k3, v3, k3, v3]
    ng = len(DIL_GROUPS)
    return pl.pallas_call(
        _dil_attn_kernel,
        grid=(n, t // tq),
        in_specs=specs,
        out_specs=pl.BlockSpec((1, tq, LANES), lambda b, j: (b, j, 0)),
        out_shape=jax.ShapeDtypeStruct((n, t, LANES), F32),
        scratch_shapes=[pltpu.VMEM((ng, tq, LANES), F32), pltpu.VMEM((ng, tq, LANES), F32)],
        compiler_params=_params("parallel", "parallel"),
        name="dil_attn",
    )(*args)


def _sample_attn_kernel(q_ref, k_ref, v_ref, *rest):
    ng = len(DIL_GROUPS)
    cache_refs, y_ref, new_refs = rest[:ng], rest[2 * ng], rest[2 * ng + 1:]
    t = q_ref.shape[1]
    mq = 2 * SUBLANES
    zpad = jnp.zeros((mq - t, ATT_PAIR), F32)
    lane = lax.broadcasted_iota(jnp.int32, (mq, ATT_PAIR), 1)
    head0 = lane < ATT_HD
    tn = lax.broadcasted_iota(jnp.int32, (mq, mq), 0)
    jn = lax.broadcasted_iota(jnp.int32, (mq, mq), 1)
    outs, lses = [], []
    for gi, (win, dil) in enumerate(DIL_GROUPS):
        reach = dil * (win // dil)
        cols = slice(gi * ATT_PAIR, (gi + 1) * ATT_PAIR)
        ct = cache_refs[gi][0, 0]
        hist = ct.shape[1]
        kt2 = ct[:ATT_PAIR].astype(BF16)
        vt2 = ct[ATT_PAIR:].astype(BF16)
        k_new, v_new = k_ref[0, :, cols], v_ref[0, :, cols]
        q = jnp.concatenate([q_ref[0, :, cols], zpad], axis=0)
        kn = jnp.concatenate([k_new, zpad], axis=0).astype(BF16)
        vn = jnp.concatenate([v_new, zpad], axis=0).astype(BF16)
        ti = lax.broadcasted_iota(jnp.int32, (mq, hist), 0)
        pj = lax.broadcasted_iota(jnp.int32, (mq, hist), 1)
        d_old = hist + ti - pj
        ok_old = (d_old <= reach) & ((d_old & (dil - 1)) == 0)
        d_new = tn - jn
        ok_new = (d_new >= 0) & (d_new <= reach) & ((d_new & (dil - 1)) == 0) & (jn < t)
        o_h, l_h = [], []
        for hs in range(2):
            hm = head0 if hs == 0 else jnp.logical_not(head0)
            qm = jnp.where(hm, q, 0.0).astype(BF16)
            s_old = jnp.where(ok_old, jnp.dot(qm, kt2, preferred_element_type=F32), NEG_BIG)
            s_new = jnp.where(ok_new, lax.dot_general(qm, kn, NT, preferred_element_type=F32), NEG_BIG)
            mx = jnp.maximum(jnp.max(s_old, axis=-1, keepdims=True), jnp.max(s_new, axis=-1, keepdims=True))
            e_old = jnp.exp(s_old - mx)
            e_new = jnp.exp(s_new - mx)
            den = jnp.sum(e_old, axis=-1, keepdims=True) + jnp.sum(e_new, axis=-1, keepdims=True)
            num = (lax.dot_general(e_old.astype(BF16), vt2, NT, preferred_element_type=F32)
                   + jnp.dot(e_new.astype(BF16), vn, preferred_element_type=F32))
            o_h.append(num / den)
            l_h.append(jnp.broadcast_to(mx + jnp.log(den), num.shape))
        outs.append(jnp.where(head0, o_h[0], o_h[1])[:t])
        lses.append(jnp.where(head0, l_h[0], l_h[1])[:t])

        shifted = pltpu.roll(ct, hist - t, 1)
        fresh = jnp.concatenate([jnp.zeros((LANES - t, 2 * ATT_PAIR), F32),
                                 jnp.concatenate([k_new, v_new], axis=1)], axis=0).T
        lane_c = lax.broadcasted_iota(jnp.int32, (2 * ATT_PAIR, LANES), 1)
        if hist > LANES:
            new_refs[gi][0, 0, :, :hist - LANES] = shifted[:, :hist - LANES]
        new_refs[gi][0, 0, :, hist - LANES:] = jnp.where(lane_c >= LANES - t, fresh, shifted[:, hist - LANES:])
    y_ref[0] = _merge_groups(outs, lses)


def _sample_attn(q3, k3, v3, caches_t, accs, layer):
    n, t, w = q3.shape
    ng = len(DIL_GROUPS)
    for (win, dil), c in zip(DIL_GROUPS, caches_t):
        assert c.shape[3] == win and win % LANES == 0 and dil & (dil - 1) == 0 and t <= SUBLANES
    new = pl.BlockSpec((1, t, w), lambda b: (b, 0, 0))
    blk = lambda c: pl.BlockSpec((1, 1) + c.shape[2:], lambda b: (layer, b, 0, 0))
    res = pl.pallas_call(
        _sample_attn_kernel,
        grid=(n,),
        in_specs=[new, new, new] + [blk(c) for c in caches_t]
        + [pl.BlockSpec(memory_space=pl.ANY)] * ng,
        out_specs=[pl.BlockSpec((1, t, LANES), lambda b: (b, 0, 0))] + [blk(c) for c in accs],
        out_shape=[jax.ShapeDtypeStruct((n, t, LANES), F32)]
        + [jax.ShapeDtypeStruct(a.shape, F32) for a in accs],
        input_output_aliases={3 + ng + g: 1 + g for g in range(ng)},
        compiler_params=_params("parallel"),
        name="sample_attn",
    )(q3, k3, v3, *caches_t, *accs)
    return res[0], list(res[1:])


def _rwkv_prep_kernel(prev8_ref, sp_ref, cur_ref, mu_ref, wwa_ref, w0_ref, a0_ref, g2_ref,
                      kk_ref, ka_ref, rk_ref, ones_ref, tril_ref,
                      at_ref, bt_ref, kt_ref, rt_ref, v_ref, pl_ref, g_ref, bonus_ref,
                      sh_ref, p_ref, *, chunk):
    bn, tm, w = cur_ref.shape
    j = pl.program_id(1)
    cur = cur_ref[...]
    prev_row = jnp.where(j == 0, sp_ref[...], prev8_ref[:, SUBLANES - 1:SUBLANES, :])
    sh_ref[:, SUBLANES - 1:SUBLANES, :] = prev_row
    sh_ref[:, SUBLANES:, :] = cur
    prev = sh_ref[:, SUBLANES - 1:SUBLANES - 1 + tm, :]
    m = bn * tm
    zs = (cur + (prev - cur) * mu_ref[...]).reshape(m, w)

    r = zs[:, 0:RWKV_W]
    k = zs[:, RWKV_W:2 * RWKV_W]
    v = zs[:, 2 * RWKV_W:3 * RWKV_W]
    lo = 3 * RWKV_W
    z_wa = zs[:, lo:lo + LANES]
    z_g = zs[:, lo + LANES:lo + 2 * LANES]
    lane = lax.broadcasted_iota(jnp.int32, z_wa.shape, 1)
    u = jnp.where(lane < LANES // 2, jnp.tanh(z_wa), z_wa)
    lora = _bdot(u, wwa_ref[...])
    xw = w0_ref[...] + lora[:, :RWKV_W]
    w_log = -(jnp.maximum(-xw, 0.0) + jnp.log(1.0 + jnp.exp(-jnp.abs(xw)))) - 0.5
    e = jnp.exp(w_log)
    a = _sigmoid(a0_ref[...] + lora[:, RWKV_W:])
    g_ref[...] = _bdot(_sigmoid(z_g), g2_ref[...])

    kk = k * kk_ref[...]
    kk = kk * lax.rsqrt(jnp.maximum(_head_sums(kk * kk, ones_ref), 1e-24))
    k = k * (1.0 + (a - 1.0) * ka_ref[...])
    bonus_ref[...] = _head_sums(r * k * rk_ref[...], ones_ref) * v

    tril = tril_ref[...]
    tb = tril.shape[0]
    e12 = jnp.concatenate(_split(e), axis=1)
    cum2 = jnp.concatenate([jnp.dot(tril, e12[b * tb:(b + 1) * tb], preferred_element_type=F32)
                            for b in range(m // tb)], axis=0)
    cum = cum2[:, :RWKV_W] + cum2[:, RWKV_W:]
    p_inc = jnp.exp(-cum)
    p_inv = jnp.exp(cum)

    def put(ref, val):
        ref[:, :tm, :] = val.reshape(bn, tm, RWKV_W)
        if ref.shape[1] > tm:
            ref[:, tm:, :] = jnp.zeros((bn, ref.shape[1] - tm, RWKV_W), F32)

    put(at_ref, -kk * jnp.exp(e - cum))
    put(bt_ref, kk * a * p_inv)
    put(kt_ref, k * p_inv)
    put(rt_ref, r * p_inc)
    put(v_ref, v)
    p_ref[...] = p_inc
    for c in range(m // chunk):
        pl_ref[c] = p_ref[(c + 1) * chunk - 1:(c + 1) * chunk, :]


def _rwkv_prep(zr3, shift_prev, state_layer, wts, layer, *, bn, tm, chunk, t_out):
    n, t, w = zr3.shape
    m = bn * tm
    per = tm // SUBLANES
    assert t_out == t or t == tm
    rowblk = pl.BlockSpec((m, RWKV_W), lambda b, j: (b * (t // tm) + j, 0))
    scanblk = pl.BlockSpec((bn, tm if t_out == t else t_out, RWKV_W), lambda b, j: (b, j, 0))
    tb = min(m, MXU_TILE)
    assert tb % chunk == 0 and m % tb == 0
    tril = (jnp.arange(tb)[:, None] >= jnp.arange(tb)[None, :]) & \
           (jnp.arange(tb)[:, None] // chunk == jnp.arange(tb)[None, :] // chunk)
    nch = m // chunk
    names = ('rwkv_mu', 'rwkv_wwa', 'rwkv_w0', 'rwkv_a0', 'rwkv_g2', 'rwkv_k_k', 'rwkv_k_a', 'rwkv_r_k')
    consts = [wts[k] for k in names] + [wts['ones_bd'], tril.astype(BF16)]
    const_specs = [_wspec(wts[k], layer) for k in names] + [_wspec(wts['ones_bd']), _wspec(consts[-1])]
    outs = pl.pallas_call(
        functools.partial(_rwkv_prep_kernel, chunk=chunk),
        grid=(n // bn, t // tm),
        in_specs=[pl.BlockSpec((bn, SUBLANES, w), lambda b, j: (b, jnp.maximum(j * per - 1, 0), 0)),
                  pl.BlockSpec((None, bn, 1, w), lambda b, j: (state_layer, b, 0, 0)),
                  pl.BlockSpec((bn, tm, w), lambda b, j: (b, j, 0))]
        + const_specs,
        out_specs=[scanblk] * 5
        + [pl.BlockSpec((nch, 1, RWKV_W), lambda b, j: (b * (t // tm) + j, 0, 0)), rowblk, rowblk],
        out_shape=[jax.ShapeDtypeStruct((n, t_out, RWKV_W), F32)] * 5
        + [jax.ShapeDtypeStruct((n * t // chunk, 1, RWKV_W), F32)]
        + [jax.ShapeDtypeStruct((n * t, RWKV_W), F32)] * 2,
        scratch_shapes=[pltpu.VMEM((bn, tm + SUBLANES, w), F32), pltpu.VMEM((m, RWKV_W), F32)],
        compiler_params=_params("parallel", "parallel"),
        name="rwkv_prep",
    )(zr3, shift_prev, zr3, *consts)
    return outs


def _rwkv_scan_kernel(at_ref, bt_ref, kt_ref, rt_ref, v_ref, pl_ref, s0_ref, y_ref, sT_ref, s_ref):
    c = pl.program_id(1)
    chunk = RWKV_HD
    nb, nsub = at_ref.shape[0], at_ref.shape[1] // chunk
    npair = RWKV_W // LANES
    assert 2 * chunk == LANES

    @pl.when(c == 0)
    def _():
        s_ref[...] = s0_ref[...]

    lane = lax.broadcasted_iota(jnp.int32, (chunk, LANES), 1)
    head0 = lane < RWKV_HD
    row2 = lax.broadcasted_iota(jnp.int32, (chunk, 2 * LANES), 0)
    col2 = lax.broadcasted_iota(jnp.int32, (chunk, 2 * LANES), 1) & (chunk - 1)
    strict2 = col2 < row2
    incl2 = col2 <= row2
    row1 = lax.broadcasted_iota(jnp.int32, (chunk, LANES), 0)
    eye_ls = jnp.where((lane & (chunk - 1)) == row1, 1.0, 0.0).astype(F32)
    ri = lax.broadcasted_iota(jnp.int32, (LANES, LANES), 0)
    ci = lax.broadcasted_iota(jnp.int32, (LANES, LANES), 1)
    same = (ri < chunk) == (ci < chunk)
    eye = jnp.where(ri == ci, 1.0, 0.0).astype(F32)

    def stack(x):
        return jnp.concatenate([jnp.where(head0, x, 0.0), jnp.where(head0, 0.0, x)], axis=0)

    def bdiag(x):
        return jnp.where(same, jnp.concatenate([x, x], axis=0), 0.0)

    def bdiag_bf(x):
        return bdiag(x).astype(BF16)

    def mm(a, b, dims=NN):
        (ah, al), (bh, bl) = a, b
        lhs = jnp.concatenate([ah, ah, al], axis=0 if dims == TN else 1)
        rhs = jnp.concatenate([bh, bl, bh], axis=1 if dims == NT else 0)
        return lax.dot_general(lhs, rhs, dims, preferred_element_type=F32)

    jobs = [(sq, p, slice(p * LANES, (p + 1) * LANES), slice(ch * chunk, (ch + 1) * chunk), ch)
            for sq in range(nb) for p in range(npair) for ch in range(nsub)]
    states = {(sq, p): s_ref[sq, p] for sq in range(nb) for p in range(npair)}
    at = [at_ref[sq, rs, cs] for sq, p, cs, rs, ch in jobs]
    rt = [rt_ref[sq, rs, cs] for sq, p, cs, rs, ch in jobs]
    vv = [v_ref[sq, rs, cs] for sq, p, cs, rs, ch in jobs]
    gram_a, gram_r = [], []
    for i, (sq, p, cs, rs, ch) in enumerate(jobs):
        bk = _split(jnp.concatenate([stack(bt_ref[sq, rs, cs]), stack(kt_ref[sq, rs, cs])], axis=0))
        gram_a.append(mm(_split(at[i]), bk, NT))
        gram_r.append(lax.dot_general(rt[i].astype(BF16), bk[0], NT, preferred_element_type=F32))
    m_ab = [jnp.where(strict2, g, 0.0)[:, :LANES] for g in gram_a]
    m_ak = [jnp.where(strict2, g, 0.0)[:, LANES:] for g in gram_a]
    m_r = [jnp.where(incl2, g, 0.0).astype(BF16) for g in gram_r]
    zero_bf = jnp.zeros((LANES, LANES), BF16)
    nlev = int(math.log2(chunk))
    npow = [jnp.dot(m.astype(BF16), bdiag_bf(m), preferred_element_type=F32) for m in m_ab]
    inv = [eye_ls + m for m in m_ab]
    for lev in range(1, nlev):
        for i in range(len(jobs)):
            if lev < nlev - 1:
                rhs = jnp.concatenate([bdiag_bf(npow[i]), bdiag_bf(inv[i])], axis=1)
                out = jnp.dot(npow[i].astype(BF16), rhs, preferred_element_type=F32)
                npow[i], inv[i] = out[:, :LANES], inv[i] + out[:, LANES:]
            else:
                inv[i] = inv[i] + jnp.dot(npow[i].astype(BF16), bdiag_bf(inv[i]), preferred_element_type=F32)
    mv = [mm(_split(m), _split(bdiag(x))) for m, x in zip(m_ak, vv)]
    wu = [mm(_split(t), _split(jnp.concatenate([bdiag(x), bdiag(a)], axis=1))) for t, x, a in zip(inv, mv, at)]
    qy = []
    for i in range(len(jobs)):
        w_m, u_m = wu[i][:, :LANES], wu[i][:, LANES:]
        zq = jnp.concatenate([jnp.concatenate([bdiag_bf(u_m), bdiag_bf(w_m)], axis=1),
                              jnp.concatenate([zero_bf, bdiag_bf(vv[i])], axis=1)], axis=0)
        qy.append(jnp.dot(m_r[i], zq, preferred_element_type=F32))
    ad = []
    for i, (sq, p, cs, rs, ch) in enumerate(jobs):
        decay = pl_ref[sq, ch, :, cs]
        bk = jnp.concatenate([bt_ref[sq, rs, cs], kt_ref[sq, rs, cs]], axis=0) * decay
        z = jnp.concatenate([jnp.concatenate([wu[i][:, LANES:], wu[i][:, :LANES]], axis=1),
                             jnp.concatenate([jnp.zeros_like(vv[i]), vv[i]], axis=1)], axis=0)
        ad.append(mm(_split(bk), _split(z), TN))
    a_t = [_split(eye * pl_ref[sq, ch, :, cs] + jnp.where(same, ad[i][:, :LANES], 0.0))
           for i, (sq, p, cs, rs, ch) in enumerate(jobs)]
    d_t = [jnp.where(same, ad[i][:, LANES:], 0.0) for i in range(len(jobs))]
    q_t = [(rt[i] + qy[i][:, :LANES]).astype(BF16) for i in range(len(jobs))]
    for i, (sq, p, cs, rs, ch) in sorted(enumerate(jobs), key=lambda e: e[1][4]):
        st = _split(states[sq, p])
        y_ref[sq, rs, cs] = jnp.dot(q_t[i], st[0], preferred_element_type=F32) + qy[i][:, LANES:]
        states[sq, p] = mm(a_t[i], st) + d_t[i]
    for (sq, p), val in states.items():
        s_ref[sq, p] = val

    @pl.when(c == pl.num_programs(1) - 1)
    def _():
        sT_ref[...] = s_ref[...]


def _rwkv_scan(at, bt, kt, rt, v, plast, s0, state_layer, *, n, t, chunk, nb):
    assert chunk == RWKV_HD and at.shape == (n, t, RWKV_W)
    npair = RWKV_W // LANES
    nch = t // chunk
    nsub = next(c for c in (4, 2, 1) if nch % c == 0)
    tok = pl.BlockSpec((nb, nsub * chunk, RWKV_W), lambda b, c: (b, c, 0))
    st = pl.BlockSpec((nb, npair, LANES, LANES), lambda b, c: (b, 0, 0, 0))
    st_in = pl.BlockSpec((None, nb, npair, LANES, LANES), lambda b, c: (state_layer, b, 0, 0, 0))
    y, s_t = pl.pallas_call(
        _rwkv_scan_kernel,
        grid=(n // nb, nch // nsub),
        in_specs=[tok] * 5 + [pl.BlockSpec((nb, nsub, 1, RWKV_W), lambda b, c: (b, c, 0, 0)), st_in],
        out_specs=[tok, st],
        out_shape=[jax.ShapeDtypeStruct((n, t, RWKV_W), F32),
                   jax.ShapeDtypeStruct((n, npair, LANES, LANES), F32)],
        scratch_shapes=[pltpu.VMEM((nb, npair, LANES, LANES), F32)],
        compiler_params=_params("parallel", "arbitrary"),
        name="rwkv_scan",
    )(at, bt, kt, rt, v, plast.reshape(n, nch, 1, RWKV_W), s0)
    return y, s_t


def _post_mix_kernel(x_ref, yp_ref, ya_ref, ys_ref, bonus_ref, g_ref, gate_ref, p_ref, gf_ref,
                     pp_ref, pa_ref, pr_ref, wo_ref, lng_ref, lnb_ref, mean_ref,
                     g1_ref, w1_ref, w3_ref, w2_ref, g2_ref, wg_ref, wp_ref, out_ref, *, final_norm):
    d = x_ref.shape[1]
    ys = ys_ref[...]
    mu = _head_sums(ys, mean_ref)
    dev = ys - mu
    var = _head_sums(dev * dev, mean_ref)
    yn = dev * lax.rsqrt(var + RWKV_LN_EPS) * lng_ref[...] + lnb_ref[...]
    y_rwkv = (yn + bonus_ref[...]) * g_ref[...]

    gate = lambda c: gate_ref[:, c * d:(c + 1) * d].astype(F32)
    merged = (gate(0) * _bdot(yp_ref[...], pp_ref[...])
              + gate(1) * _bdot(ya_ref[...], pa_ref[...])
              + gate(2) * _bdot(y_rwkv, pr_ref[...]))
    x = x_ref[...] + _bdot(merged, wo_ref[...])

    h = _rms(x, g1_ref[...]).astype(BF16)
    hidden = w1_ref.shape[1]
    cut = pl.cdiv(hidden // MXU_TILE, 2) * MXU_TILE
    y = x
    for cs in (slice(0, cut), slice(cut, hidden)):
        h1 = jnp.dot(h, w1_ref[:, cs], preferred_element_type=F32)
        h3 = jnp.dot(h, w3_ref[:, cs], preferred_element_type=F32)
        y = y + _bdot(h1 * _sigmoid(h1) * h3, w2_ref[cs, :])
    gate = _sigmoid(_bdot(_rms(y, g2_ref[...]), wg_ref[...]))
    y = y + _bdot(p_ref[...], wp_ref[...]) * gate
    if final_norm:
        y = _rms(y, gf_ref[...])
    out_ref[...] = y


def _post_mix(x2d, yp, ya, ys, bonus, g, gates, p_all, wts, layer, g_final, tm, final_norm):
    m, d = x2d.shape
    row = lambda w: pl.BlockSpec((tm, w), lambda i: (i, 0))
    names = ('proj_pool', 'proj_attn', 'proj_rwkv', 'w_out', 'rwkv_ln_g', 'rwkv_ln_b')
    names2 = ('norm_ffn_g', 'ffn_w1', 'ffn_w3', 'ffn_w2', 'norm_ple_g', 'ple_gate', 'ple_proj')
    return pl.pallas_call(
        functools.partial(_post_mix_kernel, final_norm=final_norm),
        grid=(m // tm,),
        in_specs=[row(d), row(POOL_W), row(LANES)] + [row(RWKV_W)] * 3 + [row(3 * d)]
        + [pl.BlockSpec((None, tm, p_all.shape[2]), lambda i: (layer, i, 0)), _wspec(g_final)]
        + [_wspec(wts[k], layer) for k in names] + [_wspec(wts['mean_bd'])]
        + [_wspec(wts[k], layer) for k in names2],
        out_specs=row(d),
        out_shape=jax.ShapeDtypeStruct((m, d), F32),
        compiler_params=_params("parallel"),
        name="post_mix_final" if final_norm else "post_mix",
    )(x2d, yp, ya, ys, bonus, g, gates, p_all, g_final,
      *[wts[k] for k in names], wts['mean_bd'], *[wts[k] for k in names2])


def _rope_tables(pos, reps):
    half = ATT_HD // 2
    inv = ROPE_THETA ** (-2.0 * jnp.arange(half, dtype=F32) / ATT_HD)
    ang = pos.astype(F32)[:, None] * inv[None, :]
    cos = jnp.cos(ang)
    sin = jnp.sin(ang)
    cos = jnp.tile(jnp.concatenate([cos, cos], axis=1), (reps, LANES // ATT_HD))
    sin = jnp.tile(jnp.concatenate([-sin, sin], axis=1), (reps, LANES // ATT_HD))
    return cos, sin


def _block_diag(blocks):
    g, a, b = blocks.shape[-3:]
    rows = []
    for i in range(g):
        parts = [blocks[..., i, :, :] if j == i else jnp.zeros_like(blocks[..., i, :, :]) for j in range(g)]
        rows.append(jnp.concatenate(parts, axis=-1))
    return jnp.concatenate(rows, axis=-2)


def _stacked_weights(w):
    row = lambda a: a.reshape(a.shape[0], 1, -1)
    bf = lambda a: a.astype(BF16)
    heads = RWKV_W // RWKV_HD
    zero = jnp.zeros_like(w['rwkv_w2'])
    wwa = jnp.concatenate([jnp.concatenate([w['rwkv_w2'], zero], axis=2),
                           jnp.concatenate([zero, w['rwkv_a2']], axis=2)], axis=1)
    ones = jnp.ones((heads, RWKV_HD, RWKV_HD), F32)
    return {
        'norm_mix_g': row(w['norm_mix_g']), 'w_in': bf(w['w_in']),
        'pool_w_bd': bf(_block_diag(w['pool_w_grp'])), 'pool_scale': row(w['pool_scale']),
        'rwkv_mu': row(w['rwkv_mu']), 'rwkv_wwa': bf(wwa),
        'rwkv_w0': row(w['rwkv_w0']), 'rwkv_a0': row(w['rwkv_a0']), 'rwkv_g2': bf(w['rwkv_g2']),
        'rwkv_k_k': row(w['rwkv_k_k']), 'rwkv_k_a': row(w['rwkv_k_a']), 'rwkv_r_k': row(w['rwkv_r_k']),
        'rwkv_ln_g': row(w['rwkv_ln_g']), 'rwkv_ln_b': row(w['rwkv_ln_b']),
        'ones_bd': bf(_block_diag(ones)), 'mean_bd': bf(_block_diag(ones / RWKV_HD)),
        'proj_pool': bf(w['proj_pool']), 'proj_attn': bf(w['proj_attn']),
        'proj_rwkv': bf(w['proj_rwkv']), 'w_out': bf(w['w_out']),
        'norm_ffn_g': row(w['norm_ffn_g']), 'ffn_w1': bf(w['ffn_w1']), 'ffn_w3': bf(w['ffn_w3']),
        'ffn_w2': bf(w['ffn_w2']), 'norm_ple_g': row(w['norm_ple_g']),
        'ple_proj': bf(w['ple_proj']), 'ple_gate': bf(w['ple_gate']),
    }


def _token_tiles(m):
    tm = next(c for c in (512, 256, 128, m) if m % c == 0)
    return tm, tm, min(tm, 256)


def _pair_states(s):
    st = jnp.swapaxes(s, -1, -2)
    st = st.reshape(st.shape[:-3] + (st.shape[-3] // 2, 2) + st.shape[-2:])
    return _block_diag(st)


def _unpair_states(s2):
    half = s2.shape[-1] // 2
    s = jnp.stack([s2[..., :half, :half], s2[..., half:, half:]], axis=-3)
    s = s.reshape(s.shape[:-4] + (2 * s.shape[-4],) + s.shape[-2:])
    return jnp.swapaxes(s, -1, -2)


def _decoder_layer(x2d, n, t, p_all, wts, layer, g_final, final_norm, tabs, state, *, prompt, tiles, chunk):
    tm, tm_in, tm_post = tiles
    sl = state['layer']
    three = lambda a: a.reshape(n, t, a.shape[-1])

    if prompt:
        yp, q, k, v, zr, gates, tails = _in_proj(x2d, wts, layer, tabs[0], tabs[1], tm_in, seq_len=t)
        per_seq = t // tm_in
        new_pool = tails.reshape(n, per_seq, POOL_HALO, POOL_W)[:, -1, POOL_HALO - POOL_HIST:]
    else:
        zp, q, k, v, zr, gates = _in_proj(x2d, wts, layer, tabs[0], tabs[1], tm_in)
        zp3 = three(zp)
        yp = _pool(state['pool16'], zp3, wts, layer, pos0=state['pos0'])
        new_pool = jnp.concatenate([state['pool16'][sl], zp3], axis=1)[:, -POOL_HIST:]
    q3, k3, v3, zr3 = three(q), three(k), three(v), three(zr)

    if prompt:
        ya = _dil_attn(q3, k3, v3)
        new_kv = []
        for gi, (win, dil) in enumerate(DIL_GROUPS):
            cols = slice(gi * ATT_PAIR, (gi + 1) * ATT_PAIR)
            keep = min(win, t)
            kv = jnp.concatenate([k3[:, t - keep:, cols], v3[:, t - keep:, cols]], axis=-1)
            new_kv.append(kv.reshape(n, keep, 2, 2, ATT_HD))
    else:
        ya, new_kv = _sample_attn(q3, k3, v3, state['kv_t'], state['kv_acc'], sl)

    if prompt:
        at, bt, kt, rt, vv, plast, g, bonus = _rwkv_prep(zr3, state['shift'], sl, wts, layer,
                                                         bn=1, tm=tm, chunk=chunk, t_out=t)
        ys, s_t = _rwkv_scan(at, bt, kt, rt, vv, plast, state['wkv'], sl, n=n, t=t, chunk=chunk, nb=n)
    else:
        at, bt, kt, rt, vv, plast, g, bonus = _rwkv_prep(zr3, state['shift'], sl, wts, layer,
                                                         bn=n, tm=t, chunk=t, t_out=chunk)
        ys, s_t = _rwkv_scan(at, bt, kt, rt, vv, plast, state['wkv'], sl,
                             n=n, t=chunk, chunk=chunk, nb=4 if n % 4 == 0 else 1)
        ys = ys[:, :t]
    new_shift = zr3[:, t - 1]

    x2d = _post_mix(x2d, yp.reshape(n * t, POOL_W), ya.reshape(n * t, LANES), ys.reshape(n * t, RWKV_W),
                    bonus, g, gates, p_all, wts, layer, g_final, tm_post, final_norm)
    return x2d, (new_pool, new_shift, s_t, new_kv)


def kernel(x_prompt, x_sample, state_pool, state_shift, state_wkv, cache_kv_w128, cache_kv_w512,
           cache_kv_w2048, p_prompt, p_sample, norm_mix_g, w_in, pool_w_grp, pool_scale, rwkv_mu,
           rwkv_w0, rwkv_w2, rwkv_a0, rwkv_a2, rwkv_g2, rwkv_k_k, rwkv_k_a, rwkv_r_k, rwkv_ln_g,
           rwkv_ln_b, proj_pool, proj_attn, proj_rwkv, w_out, norm_ffn_g, ffn_w1, ffn_w3, ffn_w2,
           norm_ple_g, ple_proj, ple_gate, norm_final_g):
    weights = dict(norm_mix_g=norm_mix_g, w_in=w_in, pool_w_grp=pool_w_grp, pool_scale=pool_scale,
                   rwkv_mu=rwkv_mu, rwkv_w0=rwkv_w0, rwkv_w2=rwkv_w2, rwkv_a0=rwkv_a0, rwkv_a2=rwkv_a2,
                   rwkv_g2=rwkv_g2, rwkv_k_k=rwkv_k_k, rwkv_k_a=rwkv_k_a, rwkv_r_k=rwkv_r_k,
                   rwkv_ln_g=rwkv_ln_g, rwkv_ln_b=rwkv_ln_b, proj_pool=proj_pool, proj_attn=proj_attn,
                   proj_rwkv=proj_rwkv, w_out=w_out, norm_ffn_g=norm_ffn_g, ffn_w1=ffn_w1, ffn_w3=ffn_w3,
                   ffn_w2=ffn_w2, norm_ple_g=norm_ple_g, ple_proj=ple_proj, ple_gate=ple_gate)
    depth = w_in.shape[0]
    np_, tp, d = x_prompt.shape
    ns, ts, _ = x_sample.shape
    caches = (cache_kv_w128, cache_kv_w512, cache_kv_w2048)
    past_len = PAST_LEN
    tiles_p = _token_tiles(np_ * tp)
    tm_s = ns * ts
    chunk = 64
    tabs_p = _rope_tables(jnp.arange(tp), 1)
    tabs_s = _rope_tables(past_len + jnp.arange(ts), ns)
    g_final = norm_final_g[None, :]
    wts = _stacked_weights(weights)

    xp = x_prompt.reshape(np_ * tp, d)
    xs = x_sample.reshape(ns * ts, d)
    pp_all = p_prompt.reshape(depth, np_ * tp, -1)
    ps_all = p_sample.reshape(depth, ns * ts, -1)
    heads = RWKV_W // RWKV_HD
    state_p = {
        'layer': 0,
        'shift': jnp.zeros((1, np_, 1, RWKV_PROJ), F32),
        'wkv': jnp.zeros((1, np_, heads // 2, 2 * RWKV_HD, 2 * RWKV_HD), F32),
    }
    caches_t = [jnp.transpose(c, (0, 1, 3, 4, 5, 2)).reshape(c.shape[0], c.shape[1], -1, c.shape[2])
                for c in caches]
    kv_acc = [jnp.zeros(c.shape, F32) for c in caches_t]
    state_s = {
        'pool16': jnp.pad(state_pool, ((0, 0), (0, 0), (POOL_HALO - POOL_HIST, 0), (0, 0))),
        'shift': state_shift[:, :, None, :], 'wkv': _pair_states(state_wkv),
        'kv_t': caches_t, 'pos0': past_len,
    }
    outs_p, outs_s = [], []
    for i in range(depth):
        last = i == depth - 1
        xp, st_p = _decoder_layer(xp, np_, tp, pp_all, wts, i, g_final, last, tabs_p, state_p,
                                  prompt=True, tiles=tiles_p, chunk=chunk)
        xs, st_s = _decoder_layer(xs, ns, ts, ps_all, wts, i, g_final, last, tabs_s,
                                  dict(state_s, layer=i, kv_acc=kv_acc),
                                  prompt=False, tiles=(tm_s, tm_s, tm_s), chunk=chunk)
        kv_acc = st_s[3]
        outs_p.append(st_p)
        outs_s.append(st_s)

    stack = lambda outs, f: jnp.stack([f(o) for o in outs])
    res = [xp.reshape(np_, tp, d), xs.reshape(ns, ts, d)]
    for idx in range(2):
        res.append(stack(outs_p, lambda o: o[idx]))
        res.append(stack(outs_s, lambda o: o[idx]))
    res.append(_unpair_states(stack(outs_p, lambda o: o[2])))
    res.append(_unpair_states(stack(outs_s, lambda o: o[2])))
    for gi in range(len(DIL_GROUPS)):
        res.append(stack(outs_p, lambda o: o[3][gi]))
        a = kv_acc[gi]
        a = a.reshape(a.shape[0], a.shape[1], 2, 2, ATT_HD, a.shape[3])
        res.append(jnp.transpose(a, (0, 1, 5, 2, 3, 4)))
    return tuple(res)
```
